```python
import jax
import jax.numpy as jnp
from jax import lax
import numpy as np

D_MODEL = 1024
BATCH = 16
SEQ = 2048
DEPTH = 1

CTX_LEN = 256
GRID_W = 64
N_HEADS = 16
N_KV_HEADS = 4
GROUP = N_HEADS // N_KV_HEADS
HEAD_DIM = D_MODEL // N_HEADS
ATTN_WIDTH = N_HEADS * HEAD_DIM
KV_WIDTH = N_KV_HEADS * HEAD_DIM
WINDOW = 128
Q_BLOCK = 128
ROPE_BASE = 10000.0
CONV_WIDTH = D_MODEL
CONV_SIZE = 3
N_EXPERTS = 16
EXPERT_FF = D_MODEL
CAPACITY_FACTOR = 2
N_MOD = 6
IN_WIDTH = ATTN_WIDTH + 2 * KV_WIDTH + 3 * CONV_WIDTH + 2 * D_MODEL
EPS = 1e-6
NEG_INF = -1e30

kernel_name = "hybrid_gated_swa_shortconv_ecmoe_dit"


def rmsnorm(x, g):
    x32 = x.astype(jnp.float32)
    y = x32 * lax.rsqrt(jnp.mean(x32 * x32, axis=-1, keepdims=True) + EPS)
    return (y * g.astype(jnp.float32)).astype(x.dtype)


def modulate(h, shift, scale):
    return h * (1 + scale) + shift


def split_projection(p):
    sizes = (ATTN_WIDTH, KV_WIDTH, KV_WIDTH, CONV_WIDTH, CONV_WIDTH, CONV_WIDTH, D_MODEL, D_MODEL)
    parts = []
    off = 0
    for s in sizes:
        parts.append(p[..., off:off + s])
        off += s
    return parts


def axial_rope_tables(n_tokens, dtype):
    rows = n_tokens // GRID_W
    row = jnp.repeat(jnp.arange(rows, dtype=jnp.float32), GRID_W)
    col = jnp.tile(jnp.arange(GRID_W, dtype=jnp.float32), rows)
    n_freq = HEAD_DIM // 4
    inv_freq = ROPE_BASE ** (-jnp.arange(n_freq, dtype=jnp.float32) / n_freq)
    ang_r = row[:, None] * inv_freq[None, :]
    ang_c = col[:, None] * inv_freq[None, :]
    ang = jnp.concatenate([ang_r, ang_r, ang_c, ang_c], axis=-1)[:, None, :]
    return jnp.cos(ang).astype(dtype), jnp.sin(ang).astype(dtype)


def rotate_half(z):
    z1, z2 = jnp.split(z, 2, axis=-1)
    return jnp.concatenate([-z2, z1], axis=-1)


def apply_axial_rope(x, cos, sin):
    xr, xc = jnp.split(x, 2, axis=-1)
    rot = jnp.concatenate([rotate_half(xr), rotate_half(xc)], axis=-1)
    return x * cos + rot * sin


def sink_column(sink, shape_prefix):
    return jnp.broadcast_to(sink.astype(jnp.float32)[None, :, :, None, None], shape_prefix + (1,))


def latent_window_attention(q, k, v, k_ctx, v_ctx, sink):
    B, T = q.shape[0], q.shape[1]
    L = k_ctx.shape[1]
    n_blocks = T // Q_BLOCK
    span = Q_BLOCK + 2 * WINDOW
    scale = HEAD_DIM ** -0.5
    pad = ((0, 0), (WINDOW, WINDOW), (0, 0), (0, 0))
    kp = jnp.pad(k, pad)
    vp = jnp.pad(v, pad)
    key_offset = jnp.arange(span) - WINDOW

    def block(i):
        start = i * Q_BLOCK
        qb = lax.dynamic_slice_in_dim(q, start, Q_BLOCK, axis=1)
        kb = lax.dynamic_slice_in_dim(kp, start, span, axis=1)
        vb = lax.dynamic_slice_in_dim(vp, start, span, axis=1)
        s_lat = jnp.einsum('bqhgd,bshd->bhgqs', qb, kb).astype(jnp.float32) * scale
        s_ctx = jnp.einsum('bqhgd,bshd->bhgqs', qb, k_ctx).astype(jnp.float32) * scale
        qpos = start + jnp.arange(Q_BLOCK)
        kpos = start + key_offset
        valid = (jnp.abs(qpos[:, None] - kpos[None, :]) <= WINDOW) & (kpos >= 0)[None, :] & (kpos < T)[None, :]
        s_lat = jnp.where(valid, s_lat, NEG_INF)
        logits = jnp.concatenate([s_lat, s_ctx, sink_column(sink, s_lat.shape[:-1])], axis=-1)
        p = jax.nn.softmax(logits, axis=-1)
        p_lat = p[..., :span].astype(v.dtype)
        p_ctx = p[..., span:span + L].astype(v.dtype)
        return (jnp.einsum('bhgqs,bshd->bqhgd', p_lat, vb)
                + jnp.einsum('bhgqs,bshd->bqhgd', p_ctx, v_ctx))

    o = lax.map(block, jnp.arange(n_blocks))
    return jnp.transpose(o, (1, 0, 2, 3, 4, 5)).reshape(B, T, ATTN_WIDTH)


def context_attention(q, k, v, sink):
    B, L = q.shape[0], q.shape[1]
    scale = HEAD_DIM ** -0.5
    s = jnp.einsum('bqhgd,bshd->bhgqs', q, k).astype(jnp.float32) * scale
    p = jax.nn.softmax(jnp.concatenate([s, sink_column(sink, s.shape[:-1])], axis=-1), axis=-1)
    o = jnp.einsum('bhgqs,bshd->bqhgd', p[..., :L].astype(v.dtype), v)
    return o.reshape(B, L, ATTN_WIDTH)


def short_conv_mixer(u, gate_b, gate_c, conv_w):
    z = gate_c * u
    zp = jnp.pad(z, ((0, 0), (1, 1), (0, 0)))
    y = conv_w[0] * zp[:, :-2] + conv_w[1] * zp[:, 1:-1] + conv_w[2] * zp[:, 2:]
    return gate_b * y


def gated_merge(o_attn, y_conv, g_attn, g_conv, w_proj_attn, w_proj_conv, w_out):
    a = jnp.einsum('btf,fd->btd', o_attn, w_proj_attn)
    s = jnp.einsum('btf,fd->btd', y_conv, w_proj_conv)
    merged = jax.nn.sigmoid(g_attn) * a + jax.nn.sigmoid(g_conv) * s
    return jnp.einsum('btd,de->bte', merged, w_out)


def expert_choice_moe(h, w_router, w_gate, w_up, w_down):
    B, T, D = h.shape
    cap = CAPACITY_FACTOR * T // N_EXPERTS
    logits = jnp.einsum('btd,de->bte', h, w_router).astype(jnp.float32)
    affinity = jax.nn.softmax(logits, axis=-1)
    gate, idx = lax.top_k(jnp.swapaxes(affinity, 1, 2), cap)
    xe = jax.vmap(lambda hb, ib: hb[ib])(h, idx)
    a = jnp.einsum('becd,edf->becf', xe, w_gate)
    u = jnp.einsum('becd,edf->becf', xe, w_up)
    ye = jnp.einsum('becf,efd->becd', jax.nn.silu(a) * u, w_down) * gate.astype(h.dtype)[..., None]

    def combine(yb, ib):
        return jnp.zeros((T, D), yb.dtype).at[ib.reshape(-1)].add(yb.reshape(-1, D))

    return jax.vmap(combine)(ye, idx)


def setup_inputs(seed: int = 0) -> dict:
    key = jax.random.key(seed)
    ks = jax.random.split(key, 20)
    f32 = jnp.float32

    def nrm(k, shape, scale):
        return jax.random.normal(k, shape, f32) * scale

    return {
        'x': nrm(ks[0], (BATCH, SEQ, D_MODEL), 1.0),
        'c': nrm(ks[1], (BATCH, D_MODEL), 1.0),
        'ctx': nrm(ks[2], (BATCH, CTX_LEN, D_MODEL), 1.0),
        'c_ctx': nrm(ks[3], (D_MODEL,), 1.0),
        'w_mod': nrm(ks[4], (DEPTH, D_MODEL, N_MOD * D_MODEL), 0.5 * D_MODEL ** -0.5),
        'b_mod': nrm(ks[5], (DEPTH, N_MOD * D_MODEL), 0.02),
        'norm1_g': 1.0 + nrm(ks[6], (DEPTH, D_MODEL), 0.02),
        'w_in': nrm(ks[7], (DEPTH, D_MODEL, IN_WIDTH), D_MODEL ** -0.5),
        'attn_sink': nrm(ks[8], (DEPTH, N_KV_HEADS, GROUP), 0.5),
        'conv_w': nrm(ks[9], (DEPTH, CONV_SIZE, CONV_WIDTH), CONV_SIZE ** -0.5),
        'w_proj_attn': nrm(ks[10], (DEPTH, ATTN_WIDTH, D_MODEL), ATTN_WIDTH ** -0.5),
        'w_proj_conv': nrm(ks[11], (DEPTH, CONV_WIDTH, D_MODEL), CONV_WIDTH ** -0.5),
        'w_out': nrm(ks[12], (DEPTH, D_MODEL, D_MODEL), D_MODEL ** -0.5),
        'norm2_g': 1.0 + nrm(ks[13], (DEPTH, D_MODEL), 0.02),
        'w_router': nrm(ks[14], (DEPTH, D_MODEL, N_EXPERTS), D_MODEL ** -0.5),
        'w_exp_gate': nrm(ks[15], (DEPTH, N_EXPERTS, D_MODEL, EXPERT_FF), D_MODEL ** -0.5),
        'w_exp_up': nrm(ks[16], (DEPTH, N_EXPERTS, D_MODEL, EXPERT_FF), D_MODEL ** -0.5),
        'w_exp_down': nrm(ks[17], (DEPTH, N_EXPERTS, EXPERT_FF, D_MODEL), EXPERT_FF ** -0.5),
        'final_norm_g': 1.0 + nrm(ks[18], (D_MODEL,), 0.02),
    }


def reference(x, c, ctx, c_ctx, w_mod, b_mod, norm1_g, w_in, attn_sink, conv_w,
              w_proj_attn, w_proj_conv, w_out, norm2_g, w_router, w_exp_gate, w_exp_up,
              w_exp_down, final_norm_g):
    B, T = x.shape[0], x.shape[1]
    L = ctx.shape[1]
    cos, sin = axial_rope_tables(T, x.dtype)
    h_ctx_stream = ctx
    for layer in range(DEPTH):
        last = layer == DEPTH - 1
        mod = jnp.einsum('bd,de->be', jax.nn.silu(c), w_mod[layer]) + b_mod[layer]
        sh1, sc1, g1, sh2, sc2, g2 = jnp.split(mod[:, None, :], N_MOD, axis=-1)
        mod_c = jnp.einsum('d,de->e', jax.nn.silu(c_ctx), w_mod[layer]) + b_mod[layer]
        csh1, csc1, cg1, csh2, csc2, cg2 = jnp.split(mod_c, N_MOD, axis=-1)

        hc = modulate(rmsnorm(h_ctx_stream, norm1_g[layer]), csh1, csc1)
        if last:
            kv_c = jnp.einsum('btd,df->btf', hc, w_in[layer][:, ATTN_WIDTH:ATTN_WIDTH + 2 * KV_WIDTH])
            k_c, v_c = jnp.split(kv_c, 2, axis=-1)
        else:
            pc = split_projection(jnp.einsum('btd,df->btf', hc, w_in[layer]))
            q_c, k_c, v_c, u_c, b_c, c_c, ga_c, gc_c = pc
        k_ctx = k_c.reshape(B, L, N_KV_HEADS, HEAD_DIM)
        v_ctx = v_c.reshape(B, L, N_KV_HEADS, HEAD_DIM)

        hx = modulate(rmsnorm(x, norm1_g[layer]), sh1, sc1)
        q_x, k_x, v_x, u_x, b_x, c_x, ga_x, gc_x = split_projection(
            jnp.einsum('btd,df->btf', hx, w_in[layer]))
        q = apply_axial_rope(q_x.reshape(B, T, N_HEADS, HEAD_DIM), cos, sin)
        q = q.reshape(B, T, N_KV_HEADS, GROUP, HEAD_DIM)
        k = apply_axial_rope(k_x.reshape(B, T, N_KV_HEADS, HEAD_DIM), cos, sin)
        v = v_x.reshape(B, T, N_KV_HEADS, HEAD_DIM)
        o_attn = latent_window_attention(q, k, v, k_ctx, v_ctx, attn_sink[layer])
        y_conv = short_conv_mixer(u_x, b_x, c_x, conv_w[layer])
        x = x + g1 * gated_merge(o_attn, y_conv, ga_x, gc_x,
                                 w_proj_attn[layer], w_proj_conv[layer], w_out[layer])

        hx2 = modulate(rmsnorm(x, norm2_g[layer]), sh2, sc2)
        x = x + g2 * expert_choice_moe(hx2, w_router[layer], w_exp_gate[layer],
                                       w_exp_up[layer], w_exp_down[layer])

        if not last:
            o_c = context_attention(q_c.reshape(B, L, N_KV_HEADS, GROUP, HEAD_DIM),
                                    k_ctx, v_ctx, attn_sink[layer])
            y_c = short_conv_mixer(u_c, b_c, c_c, conv_w[layer])
            h_ctx_stream = h_ctx_stream + cg1 * gated_merge(
                o_c, y_c, ga_c, gc_c, w_proj_attn[layer], w_proj_conv[layer], w_out[layer])
            hc2 = modulate(rmsnorm(h_ctx_stream, norm2_g[layer]), csh2, csc2)
            h_ctx_stream = h_ctx_stream + cg2 * expert_choice_moe(
                hc2, w_router[layer], w_exp_gate[layer], w_exp_up[layer], w_exp_down[layer])

    return rmsnorm(x, final_norm_g)
```

```python
import functools

import jax
import jax.numpy as jnp
from jax import lax
from jax.experimental import pallas as pl
from jax.experimental.pallas import tpu as pltpu

D_MODEL = 1024
BATCH = 16
SEQ = 2048
CTX_LEN = 256
GRID_W = 64
N_HEADS = 16
N_KV_HEADS = 4
GROUP = N_HEADS // N_KV_HEADS
HEAD_DIM = D_MODEL // N_HEADS
ATTN_WIDTH = N_HEADS * HEAD_DIM
KV_WIDTH = N_KV_HEADS * HEAD_DIM
WINDOW = 128
Q_BLOCK = 128
ROPE_BASE = 10000.0
CONV_WIDTH = D_MODEL
N_EXPERTS = 16
EXPERT_FF = D_MODEL
CAPACITY_FACTOR = 2
N_MOD = 6
EPS = 1e-6
NEG_INF = -1e30

CAP = CAPACITY_FACTOR * SEQ // N_EXPERTS
N_QBLK = SEQ // Q_BLOCK
LANES = 128
BF16_ROWS = 16
TOK_TILE = 256
MOD_ROWS = 32
VMEM_LIMIT = 56 * 1024 * 1024

OFF_Q = 0
OFF_K = OFF_Q + ATTN_WIDTH
OFF_V = OFF_K + KV_WIDTH
OFF_U = OFF_V + KV_WIDTH
OFF_B = OFF_U + CONV_WIDTH
OFF_C = OFF_B + CONV_WIDTH
OFF_GA = OFF_C + CONV_WIDTH
OFF_GC = OFF_GA + D_MODEL
IN_WIDTH = OFF_GC + D_MODEL

F32 = jnp.float32
BF16 = jnp.bfloat16


def _params(*sem):
    return pltpu.CompilerParams(dimension_semantics=sem, vmem_limit_bytes=VMEM_LIMIT)


def _dot(a, b):
    return jnp.dot(a, b, preferred_element_type=F32)


def _dot_nt(a, b):
    return lax.dot_general(a, b, (((1,), (1,)), ((), ())), preferred_element_type=F32)


def _sigmoid(x):
    return 1.0 / (1.0 + jnp.exp(-x))


def _split_bf16(x):
    hi = x.astype(BF16)
    lo = (x - hi.astype(F32)).astype(BF16)
    return hi, lo


def _norm_mod(x, g, shift, scale):
    y = x * lax.rsqrt(jnp.mean(x * x, axis=-1, keepdims=True) + EPS) * g
    return y * (1.0 + scale) + shift


def _mod_kernel(c_ref, w_ref, b_ref, o_ref):
    cv = c_ref[...]
    s_hi, s_lo = _split_bf16(cv * _sigmoid(cv))
    w_hi, w_lo = _split_bf16(w_ref[...])
    o_ref[...] = _dot(s_hi, w_hi) + _dot(s_lo, w_hi) + _dot(s_hi, w_lo) + b_ref[...]


def _mod(cvec, w_mod, b_mod):
    n_out = N_MOD * D_MODEL
    blk = D_MODEL
    return pl.pallas_call(
        _mod_kernel,
        grid=(n_out // blk,),
        in_specs=[
            pl.BlockSpec((MOD_ROWS, D_MODEL), lambda j: (0, 0)),
            pl.BlockSpec((D_MODEL, blk), lambda j: (0, j)),
            pl.BlockSpec((1, blk), lambda j: (0, j)),
        ],
        out_specs=pl.BlockSpec((MOD_ROWS, blk), lambda j: (0, j)),
        out_shape=jax.ShapeDtypeStruct((MOD_ROWS, n_out), F32),
        compiler_params=_params("arbitrary"),
        name="mod",
    )(cvec, w_mod, b_mod)


def _ctx_kernel(ctx_ref, g_ref, sh_ref, sc_ref, w_ref, k_ref, v_ref):
    h = _norm_mod(ctx_ref[0], g_ref[...], sh_ref[...], sc_ref[...]).astype(BF16)
    kv = _dot(h, w_ref[...])
    k_ref[0] = kv[:, :KV_WIDTH].astype(BF16)
    v_ref[0] = kv[:, KV_WIDTH:].astype(BF16)


def _ctx_kv(ctx, g1, csh1, csc1, w_kv):
    row = pl.BlockSpec((1, D_MODEL), lambda b: (0, 0))
    out = pl.BlockSpec((1, CTX_LEN, KV_WIDTH), lambda b: (b, 0, 0))
    shape = jax.ShapeDtypeStruct((BATCH, CTX_LEN, KV_WIDTH), BF16)
    return pl.pallas_call(
        _ctx_kernel,
        grid=(BATCH,),
        in_specs=[
            pl.BlockSpec((1, CTX_LEN, D_MODEL), lambda b: (b, 0, 0)),
            row, row, row,
            pl.BlockSpec((D_MODEL, 2 * KV_WIDTH), lambda b: (0, 0)),
        ],
        out_specs=[out, out],
        out_shape=[shape, shape],
        compiler_params=_params("arbitrary"),
        name="ctx_kv",
    )(ctx, g1, csh1, csc1, w_kv)


def _proj_kernel(x_ref, g_ref, sh_ref, sc_ref, w_ref, cos_ref, sin_ref,
                 q_ref, k_ref, v_ref, z_ref, bg_ref, ga_ref, gc_ref):
    h = _norm_mod(x_ref[0], g_ref[...], sh_ref[0], sc_ref[0]).astype(BF16)
    cos = cos_ref[...]
    sin = sin_ref[...]
    lane = lax.broadcasted_iota(jnp.int32, (TOK_TILE, LANES), 1)
    first_half = (lane & (HEAD_DIM // 4)) == 0

    def rope(a):
        rot = jnp.where(first_half,
                        pltpu.roll(a, LANES - HEAD_DIM // 4, 1),
                        pltpu.roll(a, HEAD_DIM // 4, 1))
        return a * cos + rot * sin

    def proj(off, width):
        return _dot(h, w_ref[:, off:off + width])

    scale = HEAD_DIM ** -0.5
    qa = proj(OFF_Q, ATTN_WIDTH)
    for j in range(ATTN_WIDTH // LANES):
        sl = slice(j * LANES, (j + 1) * LANES)
        q_ref[0, :, sl] = (rope(qa[:, sl]) * scale).astype(BF16)
    ka = proj(OFF_K, KV_WIDTH)
    for j in range(KV_WIDTH // LANES):
        sl = slice(j * LANES, (j + 1) * LANES)
        k_ref[0, :, sl] = rope(ka[:, sl]).astype(BF16)
    v_ref[0] = proj(OFF_V, KV_WIDTH).astype(BF16)
    z_ref[0] = (proj(OFF_C, CONV_WIDTH) * proj(OFF_U, CONV_WIDTH)).astype(BF16)
    bg_ref[0] = proj(OFF_B, CONV_WIDTH).astype(BF16)
    ga_ref[0] = proj(OFF_GA, D_MODEL).astype(BF16)
    gc_ref[0] = proj(OFF_GC, D_MODEL).astype(BF16)


def _proj(x, g1, sh1, sc1, w_in, cos, sin):
    nt = SEQ // TOK_TILE
    tile = lambda w: pl.BlockSpec((1, TOK_TILE, w), lambda b, t: (b, t, 0))
    per_b = pl.BlockSpec((1, 1, D_MODEL), lambda b, t: (b, 0, 0))
    tab = pl.BlockSpec((TOK_TILE, LANES), lambda b, t: (t, 0))
    shp = lambda w: jax.ShapeDtypeStruct((BATCH, SEQ, w), BF16)
    widths = (ATTN_WIDTH, KV_WIDTH, KV_WIDTH, CONV_WIDTH, CONV_WIDTH, D_MODEL, D_MODEL)
    return pl.pallas_call(
        _proj_kernel,
        grid=(BATCH, nt),
        in_specs=[
            tile(D_MODEL),
            pl.BlockSpec((1, D_MODEL), lambda b, t: (0, 0)),
            per_b, per_b,
            pl.BlockSpec((D_MODEL, IN_WIDTH), lambda b, t: (0, 0), pipeline_mode=pl.Buffered(1)),
            tab, tab,
        ],
        out_specs=[tile(w) for w in widths],
        out_shape=[shp(w) for w in widths],
        compiler_params=_params("arbitrary", "arbitrary"),
        name="proj",
    )(x, g1, sh1, sc1, w_in, cos, sin)


def _attn_kernel(sink_ref, q_ref, kp_ref, kc_ref, kn_ref, vp_ref, vc_ref, vn_ref,
                 kx_ref, vx_ref, o_ref):
    i = pl.program_id(1)
    span = Q_BLOCK + 2 * WINDOW
    rows = GROUP * Q_BLOCK
    r = lax.broadcasted_iota(jnp.int32, (rows, span), 0) & (Q_BLOCK - 1)
    c = lax.broadcasted_iota(jnp.int32, (rows, span), 1)
    d = c - r
    c_lo = jnp.where(i > 0, 0, WINDOW)
    c_hi = jnp.where(i < N_QBLK - 1, span, WINDOW + Q_BLOCK)
    valid = (d >= 0) & (d <= 2 * WINDOW) & (c >= c_lo) & (c < c_hi)
    grp = lax.broadcasted_iota(jnp.int32, (rows, 1), 0) >> 7

    for g in range(N_KV_HEADS):
        ks = slice(g * HEAD_DIM, (g + 1) * HEAD_DIM)
        k_all = jnp.concatenate(
            [kp_ref[0, :, ks], kc_ref[0, :, ks], kn_ref[0, :, ks], kx_ref[0, :, ks]], axis=0)
        v_all = jnp.concatenate(
            [vp_ref[0, :, ks], vc_ref[0, :, ks], vn_ref[0, :, ks], vx_ref[0, :, ks]], axis=0)
        q4 = jnp.concatenate(
            [q_ref[0, :, (g * GROUP + j) * HEAD_DIM:(g * GROUP + j + 1) * HEAD_DIM]
             for j in range(GROUP)], axis=0)
        s = _dot_nt(q4, k_all)
        s_lat = jnp.where(valid, s[:, :span], NEG_INF)
        s_ctx = s[:, span:]
        sink = jnp.zeros((rows, 1), F32)
        for j in range(GROUP):
            sink = jnp.where(grp == j, sink_ref[g * GROUP + j], sink)
        m = jnp.maximum(jnp.maximum(jnp.max(s_lat, axis=1, keepdims=True),
                                    jnp.max(s_ctx, axis=1, keepdims=True)), sink)
        p_lat = jnp.exp(s_lat - m)
        p_ctx = jnp.exp(s_ctx - m)
        denom = (jnp.sum(p_lat, axis=1, keepdims=True) + jnp.sum(p_ctx, axis=1, keepdims=True)
                 + jnp.exp(sink - m))
        o = (_dot(p_lat.astype(BF16), v_all[:span]) + _dot(p_ctx.astype(BF16), v_all[span:]))
        o = o * (1.0 / denom)
        for jj in range(GROUP // 2):
            pair = jnp.concatenate(
                [o[(2 * jj) * Q_BLOCK:(2 * jj + 1) * Q_BLOCK],
                 o[(2 * jj + 1) * Q_BLOCK:(2 * jj + 2) * Q_BLOCK]], axis=1)
            col = (g * GROUP + 2 * jj) * HEAD_DIM
            o_ref[0, :, col:col + 2 * HEAD_DIM] = pair.astype(BF16)


def _attn(sink, q, k, v, k_ctx, v_ctx):
    qspec = pl.BlockSpec((1, Q_BLOCK, ATTN_WIDTH), lambda b, i: (b, i, 0))
    prev = pl.BlockSpec((1, Q_BLOCK, KV_WIDTH), lambda b, i: (b, jnp.maximum(i - 1, 0), 0))
    cur = pl.BlockSpec((1, Q_BLOCK, KV_WIDTH), lambda b, i: (b, i, 0))
    nxt = pl.BlockSpec((1, Q_BLOCK, KV_WIDTH), lambda b, i: (b, jnp.minimum(i + 1, N_QBLK - 1), 0))
    cx = pl.BlockSpec((1, CTX_LEN, KV_WIDTH), lambda b, i: (b, 0, 0))
    return pl.pallas_call(
        _attn_kernel,
        grid=(BATCH, N_QBLK),
        in_specs=[pl.BlockSpec(memory_space=pltpu.SMEM),
                  qspec, prev, cur, nxt, prev, cur, nxt, cx, cx],
        out_specs=qspec,
        out_shape=jax.ShapeDtypeStruct((BATCH, SEQ, ATTN_WIDTH), BF16),
        compiler_params=_params("arbitrary", "arbitrary"),
        name="attn",
    )(sink, q, k, k, k, v, v, v, k_ctx, v_ctx)


def _merge_kernel(x_ref, oa_ref, z_ref, zp_ref, zn_ref, bg_ref, ga_ref, gc_ref, cw_ref,
                  wpa_ref, wpc_ref, wo_ref, g1_ref, sh2_ref, sc2_ref, n2_ref, wr_ref,
                  x1_ref, h2_ref, lg_ref):
    t = pl.program_id(1)
    nt = pl.num_programs(1)
    z = z_ref[0].astype(F32)
    z_before = zp_ref[0, BF16_ROWS - 1:BF16_ROWS, :].astype(F32) * jnp.where(t > 0, 1.0, 0.0)
    z_after = zn_ref[0, 0:1, :].astype(F32) * jnp.where(t < nt - 1, 1.0, 0.0)
    row = lax.broadcasted_iota(jnp.int32, (TOK_TILE, 1), 0)
    z_prev = jnp.where(row == 0, z_before, pltpu.roll(z, 1, 0))
    z_next = jnp.where(row == TOK_TILE - 1, z_after, pltpu.roll(z, TOK_TILE - 1, 0))
    cw = cw_ref[...]
    y = bg_ref[0].astype(F32) * (cw[0:1] * z_prev + cw[1:2] * z + cw[2:3] * z_next)
    a = _dot(oa_ref[0], wpa_ref[...])
    s = _dot(y.astype(BF16), wpc_ref[...])
    merged = _sigmoid(ga_ref[0].astype(F32)) * a + _sigmoid(gc_ref[0].astype(F32)) * s
    x1 = x_ref[0] + g1_ref[0] * _dot(merged.astype(BF16), wo_ref[...])
    x1_ref[0] = x1
    h2 = _norm_mod(x1, n2_ref[...], sh2_ref[0], sc2_ref[0])
    h_hi, h_lo = _split_bf16(h2)
    h2_ref[0] = h_hi
    both = _dot_nt(wr_ref[...], h_hi)
    lg_ref[0] = both[:N_EXPERTS] + both[N_EXPERTS:] + _dot_nt(wr_ref[:N_EXPERTS], h_lo)


def _merge(x, oa, z, bg, ga, gc, conv_w, wpa, wpc, wo, g1, sh2, sc2, n2, wr):
    nt = SEQ // TOK_TILE
    hb = TOK_TILE // BF16_ROWS
    n_hb = SEQ // BF16_ROWS
    tile = pl.BlockSpec((1, TOK_TILE, D_MODEL), lambda b, t: (b, t, 0))
    halo_p = pl.BlockSpec((1, BF16_ROWS, D_MODEL), lambda b, t: (b, jnp.maximum(t * hb - 1, 0), 0))
    halo_n = pl.BlockSpec((1, BF16_ROWS, D_MODEL),
                          lambda b, t: (b, jnp.minimum((t + 1) * hb, n_hb - 1), 0))
    per_b = pl.BlockSpec((1, 1, D_MODEL), lambda b, t: (b, 0, 0))
    full = lambda r, c: pl.BlockSpec((r, c), lambda b, t: (0, 0))
    return pl.pallas_call(
        _merge_kernel,
        grid=(BATCH, nt),
        in_specs=[tile, tile, tile, halo_p, halo_n, tile, tile, tile,
                  full(3, D_MODEL),
                  full(D_MODEL, D_MODEL), full(D_MODEL, D_MODEL), full(D_MODEL, D_MODEL),
                  per_b, per_b, per_b, full(1, D_MODEL), full(2 * N_EXPERTS, D_MODEL)],
        out_specs=[tile, tile, pl.BlockSpec((1, N_EXPERTS, TOK_TILE), lambda b, t: (b, 0, t))],
        out_shape=[jax.ShapeDtypeStruct((BATCH, SEQ, D_MODEL), F32),
                   jax.ShapeDtypeStruct((BATCH, SEQ, D_MODEL), BF16),
                   jax.ShapeDtypeStruct((BATCH, N_EXPERTS, SEQ), F32)],
        compiler_params=_params("arbitrary", "arbitrary"),
        name="merge",
    )(x, oa, z, z, z, bg, ga, gc, conv_w, wpa, wpc, wo, g1, sh2, sc2, n2, wr)


def _cumsum_excl(mf, upper):
    nblk = SEQ // LANES
    stacked = jnp.concatenate([mf[:, j * LANES:(j + 1) * LANES] for j in range(nblk)], axis=0)
    within = _dot(stacked.astype(BF16), upper)
    tot = jnp.sum(stacked, axis=1, keepdims=True)
    off = jnp.zeros((N_EXPERTS, 1), F32)
    out = []
    for j in range(nblk):
        rs = slice(j * N_EXPERTS, (j + 1) * N_EXPERTS)
        out.append(within[rs] + off)
        off = off + tot[rs]
    return jnp.concatenate(out, axis=1)


def _route_kernel(lg_ref, pos_ref, gate_ref):
    lg = lg_ref[0]
    ex = jnp.exp(lg - jnp.max(lg, axis=0, keepdims=True))
    aff = ex / jnp.sum(ex, axis=0, keepdims=True)

    def body(_, carry):
        lo, hi = carry
        mid = lo + ((hi - lo) >> 1)
        cnt = jnp.sum(jnp.where(aff >= pltpu.bitcast(mid, F32), 1.0, 0.0), axis=1, keepdims=True)
        ge = cnt >= CAP
        return jnp.where(ge, mid, lo), jnp.where(ge, hi, mid)

    one_bits = 0x3F800000
    lo0 = jnp.zeros((N_EXPERTS, 1), jnp.int32)
    hi0 = jnp.full((N_EXPERTS, 1), one_bits + 1, jnp.int32)
    lo, _ = lax.fori_loop(0, 31, body, (lo0, hi0))
    th = pltpu.bitcast(lo, F32)

    ku = lax.broadcasted_iota(jnp.int32, (LANES, LANES), 0)
    nu = lax.broadcasted_iota(jnp.int32, (LANES, LANES), 1)
    upper = jnp.where(ku < nu, 1.0, 0.0).astype(BF16)
    gtf = jnp.where(aff > th, 1.0, 0.0)
    eqf = jnp.where(aff == th, 1.0, 0.0)
    need = CAP - jnp.sum(gtf, axis=1, keepdims=True)
    sel = gtf + eqf * jnp.where(_cumsum_excl(eqf, upper) < need, 1.0, 0.0)
    pos = _cumsum_excl(sel, upper)
    pos_ref[0] = jnp.where(sel > 0.0, pos, -1.0)
    gate_ref[0] = jnp.where(sel > 0.0, aff, 0.0)


def _route(logits):
    spec = pl.BlockSpec((1, N_EXPERTS, SEQ), lambda b: (b, 0, 0))
    shape = jax.ShapeDtypeStruct((BATCH, N_EXPERTS, SEQ), F32)
    return pl.pallas_call(
        _route_kernel,
        grid=(BATCH,),
        in_specs=[spec],
        out_specs=[spec, spec],
        out_shape=[shape, shape],
        compiler_params=_params("arbitrary"),
        name="route",
    )(logits)


def _moe_kernel(pos_ref, gate_ref, h_ref, wg_ref, wu_ref, wd_ref, o_ref):
    e = pl.program_id(1)
    pos = pos_ref[0, 0]
    gate = gate_ref[0, 0]
    slot = lax.broadcasted_iota(jnp.int32, (CAP, SEQ), 0).astype(F32)
    hit = slot == pos
    onehot = jnp.where(hit, 1.0, 0.0).astype(BF16)
    xe = _dot(onehot, h_ref[0]).astype(BF16)
    act = _dot(xe, wg_ref[0])
    up = _dot(xe, wu_ref[0])
    hid = (act * _sigmoid(act) * up).astype(BF16)
    ye_t = _dot(hid, wd_ref[0]).T.astype(BF16)
    scat = jnp.where(hit, gate, 0.0).astype(BF16)
    contrib = _dot(ye_t, scat)

    @pl.when(e == 0)
    def _():
        o_ref[0] = contrib

    @pl.when(e > 0)
    def _():
        o_ref[0] += contrib


def _moe(pos, gate, h2, wg, wu, wd):
    sel = pl.BlockSpec((1, 1, 1, SEQ), lambda b, e: (b, e, 0, 0))
    wspec = pl.BlockSpec((1, D_MODEL, EXPERT_FF), lambda b, e: (e, 0, 0))
    return pl.pallas_call(
        _moe_kernel,
        grid=(BATCH, N_EXPERTS),
        in_specs=[sel, sel,
                  pl.BlockSpec((1, SEQ, D_MODEL), lambda b, e: (b, 0, 0)),
                  wspec, wspec, wspec],
        out_specs=pl.BlockSpec((1, D_MODEL, SEQ), lambda b, e: (b, 0, 0)),
        out_shape=jax.ShapeDtypeStruct((BATCH, D_MODEL, SEQ), F32),
        compiler_params=_params("arbitrary", "arbitrary"),
        name="moe",
    )(pos, gate, h2, wg, wu, wd)


def _final_kernel(x1_ref, mt_ref, g2_ref, fg_ref, o_ref):
    x2 = x1_ref[0] + g2_ref[0] * mt_ref[0].T
    o_ref[0] = x2 * lax.rsqrt(jnp.mean(x2 * x2, axis=-1, keepdims=True) + EPS) * fg_ref[...]


def _final(x1, moe_t, g2, fg):
    nt = SEQ // TOK_TILE
    tile = pl.BlockSpec((1, TOK_TILE, D_MODEL), lambda b, t: (b, t, 0))
    return pl.pallas_call(
        _final_kernel,
        grid=(BATCH, nt),
        in_specs=[tile,
                  pl.BlockSpec((1, D_MODEL, TOK_TILE), lambda b, t: (b, 0, t)),
                  pl.BlockSpec((1, 1, D_MODEL), lambda b, t: (b, 0, 0)),
                  pl.BlockSpec((1, D_MODEL), lambda b, t: (0, 0))],
        out_specs=tile,
        out_shape=jax.ShapeDtypeStruct((BATCH, SEQ, D_MODEL), F32),
        compiler_params=_params("arbitrary", "arbitrary"),
        name="final",
    )(x1, moe_t, g2, fg)


def _rope_tables():
    rows = SEQ // GRID_W
    row = jnp.repeat(jnp.arange(rows, dtype=F32), GRID_W)
    col = jnp.tile(jnp.arange(GRID_W, dtype=F32), rows)
    n_freq = HEAD_DIM // 4
    inv_freq = ROPE_BASE ** (-jnp.arange(n_freq, dtype=F32) / n_freq)
    ang_r = row[:, None] * inv_freq[None, :]
    ang_c = col[:, None] * inv_freq[None, :]
    ang = jnp.concatenate([ang_r, ang_r, ang_c, ang_c], axis=-1)
    sign = jnp.tile(jnp.concatenate([-jnp.ones(n_freq, F32), jnp.ones(n_freq, F32)]), 2)
    reps = LANES // HEAD_DIM
    return jnp.tile(jnp.cos(ang), (1, reps)), jnp.tile(jnp.sin(ang) * sign, (1, reps))


def kernel(x, c, ctx, c_ctx, w_mod, b_mod, norm1_g, w_in, attn_sink, conv_w, w_proj_attn,
           w_proj_conv, w_out, norm2_g, w_router, w_exp_gate, w_exp_up, w_exp_down, final_norm_g):
    assert x.shape == (BATCH, SEQ, D_MODEL) and ctx.shape == (BATCH, CTX_LEN, D_MODEL)
    assert w_mod.shape[0] == 1, "single-layer problem"
    layer = 0

    cvec = jnp.concatenate(
        [c, c_ctx[None, :], jnp.zeros((MOD_ROWS - BATCH - 1, D_MODEL), F32)], axis=0)
    mod = _mod(cvec, w_mod[layer], b_mod[layer][None, :])
    chunk = lambda rows, k: rows[:, k * D_MODEL:(k + 1) * D_MODEL]
    mod_x = mod[:BATCH][:, None, :]
    sh1, sc1, g1, sh2, sc2, g2 = (mod_x[..., k * D_MODEL:(k + 1) * D_MODEL] for k in range(N_MOD))
    mod_c = mod[BATCH:BATCH + 1]
    csh1, csc1 = chunk(mod_c, 0), chunk(mod_c, 1)

    n1 = norm1_g[layer][None, :]
    w_in_b = w_in[layer].astype(BF16)
    k_ctx, v_ctx = _ctx_kv(ctx, n1, csh1, csc1, w_in_b[:, OFF_K:OFF_U])

    cos, sin = _rope_tables()
    q, k, v, z, bg, ga, gc = _proj(x, n1, sh1, sc1, w_in_b, cos, sin)
    o_attn = _attn(attn_sink[layer].reshape(-1), q, k, v, k_ctx, v_ctx)

    wr_hi, wr_lo = _split_bf16(w_router[layer].T)
    x1, h2, logits = _merge(
        x, o_attn, z, bg, ga, gc, conv_w[layer],
        w_proj_attn[layer].astype(BF16), w_proj_conv[layer].astype(BF16), w_out[layer].astype(BF16),
        g1, sh2, sc2, norm2_g[layer][None, :], jnp.concatenate([wr_hi, wr_lo], axis=0))

    pos, gate = _route(logits)
    moe_t = _moe(pos[:, :, None, :], gate[:, :, None, :], h2,
                 w_exp_gate[layer].astype(BF16), w_exp_up[layer].astype(BF16),
                 w_exp_down[layer].astype(BF16))
    return _final(x1, moe_t, g2, final_norm_g[None, :])
```

```python
import jax
import jax.numpy as jnp
from jax import lax
from jax.experimental import pallas as pl
from jax.experimental.pallas import tpu as pltpu

D_MODEL = 1024
BATCH = 16
SEQ = 2048
CTX_LEN = 256
GRID_W = 64
N_HEADS = 16
N_KV_HEADS = 4
GROUP = N_HEADS // N_KV_HEADS
HEAD_DIM = D_MODEL // N_HEADS
ATTN_WIDTH = N_HEADS * HEAD_DIM
KV_WIDTH = N_KV_HEADS * HEAD_DIM
WINDOW = 128
Q_BLOCK = 128
ROPE_BASE = 10000.0
CONV_WIDTH = D_MODEL
N_EXPERTS = 16
EXPERT_FF = D_MODEL
CAPACITY_FACTOR = 2
N_MOD = 6
EPS = 1e-6
NEG_INF = -1e30
LOG2E = 1.4426950408889634

CAP = CAPACITY_FACTOR * SEQ // N_EXPERTS
N_QBLK = SEQ // Q_BLOCK
LANES = 128
BF16_ROWS = 16
TOK_TILE = 256
N_SCAT = 4
SCAT_TOK = SEQ // N_SCAT
MOD_ROWS = 32
VMEM_LIMIT = 56 * 1024 * 1024

OFF_Q = 0
OFF_K = OFF_Q + ATTN_WIDTH
OFF_V = OFF_K + KV_WIDTH
OFF_U = OFF_V + KV_WIDTH
OFF_B = OFF_U + CONV_WIDTH
OFF_C = OFF_B + CONV_WIDTH
OFF_GA = OFF_C + CONV_WIDTH
OFF_GC = OFF_GA + D_MODEL
IN_WIDTH = OFF_GC + D_MODEL

F32 = jnp.float32
BF16 = jnp.bfloat16


def _params(*sem):
    return pltpu.CompilerParams(dimension_semantics=sem, vmem_limit_bytes=VMEM_LIMIT)


def _dot(a, b):
    return jnp.dot(a, b, preferred_element_type=F32)


def _dot_nt(a, b):
    return lax.dot_general(a, b, (((1,), (1,)), ((), ())), preferred_element_type=F32)


def _sigmoid(x):
    return 1.0 / (1.0 + jnp.exp(-x))


def _split_bf16(x):
    hi = x.astype(BF16)
    lo = (x - hi.astype(F32)).astype(BF16)
    return hi, lo


def _norm_mod(x, g, shift, scale):
    y = x * lax.rsqrt(jnp.mean(x * x, axis=-1, keepdims=True) + EPS) * g
    return y * (1.0 + scale) + shift


def _mod_kernel(c_ref, w_ref, b_ref, o_ref):
    cv = c_ref[...]
    s_hi, s_lo = _split_bf16(cv * _sigmoid(cv))
    w_hi, w_lo = _split_bf16(w_ref[...])
    o_ref[...] = _dot(s_hi, w_hi) + _dot(s_lo, w_hi) + _dot(s_hi, w_lo) + b_ref[...]


def _mod(cvec, w_mod, b_mod):
    n_out = N_MOD * D_MODEL
    blk = D_MODEL
    return pl.pallas_call(
        _mod_kernel,
        grid=(n_out // blk,),
        in_specs=[
            pl.BlockSpec((MOD_ROWS, D_MODEL), lambda j: (0, 0)),
            pl.BlockSpec((D_MODEL, blk), lambda j: (0, j)),
            pl.BlockSpec((1, blk), lambda j: (0, j)),
        ],
        out_specs=pl.BlockSpec((MOD_ROWS, blk), lambda j: (0, j)),
        out_shape=jax.ShapeDtypeStruct((MOD_ROWS, n_out), F32),
        compiler_params=_params("arbitrary"),
        name="mod",
    )(cvec, w_mod, b_mod)


def _ctx_kernel(ctx_ref, g_ref, sh_ref, sc_ref, w_ref, k_ref, vt_ref):
    h = _norm_mod(ctx_ref[0], g_ref[...], sh_ref[...], sc_ref[...]).astype(BF16)
    kv = _dot(h, w_ref[...])
    k_ref[0] = kv[:, :KV_WIDTH].astype(BF16)
    vt_ref[0] = kv[:, KV_WIDTH:].T.astype(BF16)


def _ctx_kv(ctx, g1, csh1, csc1, w_kv):
    row = pl.BlockSpec((1, D_MODEL), lambda b: (0, 0))
    return pl.pallas_call(
        _ctx_kernel,
        grid=(BATCH,),
        in_specs=[
            pl.BlockSpec((1, CTX_LEN, D_MODEL), lambda b: (b, 0, 0)),
            row, row, row,
            pl.BlockSpec((D_MODEL, 2 * KV_WIDTH), lambda b: (0, 0)),
        ],
        out_specs=[pl.BlockSpec((1, CTX_LEN, KV_WIDTH), lambda b: (b, 0, 0)),
                   pl.BlockSpec((1, KV_WIDTH, CTX_LEN), lambda b: (b, 0, 0))],
        out_shape=[jax.ShapeDtypeStruct((BATCH, CTX_LEN, KV_WIDTH), BF16),
                   jax.ShapeDtypeStruct((BATCH, KV_WIDTH, CTX_LEN), BF16)],
        compiler_params=_params("arbitrary"),
        name="ctx_kv",
    )(ctx, g1, csh1, csc1, w_kv)


def _proj_kernel(x_ref, g_ref, sh_ref, sc_ref, w_ref, cos_ref, sin_ref,
                 q_ref, k_ref, vt_ref, z_ref, bg_ref, ga_ref, gc_ref):
    h = _norm_mod(x_ref[0], g_ref[...], sh_ref[0], sc_ref[0]).astype(BF16)
    cos = cos_ref[...]
    sin = sin_ref[...]
    lane = lax.broadcasted_iota(jnp.int32, (TOK_TILE, LANES), 1)
    first_half = (lane & (HEAD_DIM // 4)) == 0

    def rope(a):
        rot = jnp.where(first_half,
                        pltpu.roll(a, LANES - HEAD_DIM // 4, 1),
                        pltpu.roll(a, HEAD_DIM // 4, 1))
        return a * cos + rot * sin

    def proj(off, width):
        return _dot(h, w_ref[:, off:off + width])

    scale = HEAD_DIM ** -0.5 * LOG2E
    qa = proj(OFF_Q, ATTN_WIDTH)
    for j in range(ATTN_WIDTH // LANES):
        sl = slice(j * LANES, (j + 1) * LANES)
        q_ref[0, :, sl] = (rope(qa[:, sl]) * scale).astype(BF16)
    ka = proj(OFF_K, KV_WIDTH)
    for j in range(KV_WIDTH // LANES):
        sl = slice(j * LANES, (j + 1) * LANES)
        k_ref[0, :, sl] = rope(ka[:, sl]).astype(BF16)
    vt_ref[0] = proj(OFF_V, KV_WIDTH).T.astype(BF16)
    z_ref[0] = (proj(OFF_C, CONV_WIDTH) * proj(OFF_U, CONV_WIDTH)).astype(BF16)
    bg_ref[0] = proj(OFF_B, CONV_WIDTH).astype(BF16)
    ga_ref[0] = proj(OFF_GA, D_MODEL).astype(BF16)
    gc_ref[0] = proj(OFF_GC, D_MODEL).astype(BF16)


def _proj(x, g1, sh1, sc1, w_in, cos, sin):
    nt = SEQ // TOK_TILE
    tile = lambda w: pl.BlockSpec((1, TOK_TILE, w), lambda b, t: (b, t, 0))
    per_b = pl.BlockSpec((1, 1, D_MODEL), lambda b, t: (b, 0, 0))
    tab = pl.BlockSpec((TOK_TILE, LANES), lambda b, t: (t, 0))
    shp = lambda w: jax.ShapeDtypeStruct((BATCH, SEQ, w), BF16)
    vt_spec = pl.BlockSpec((1, KV_WIDTH, TOK_TILE), lambda b, t: (b, 0, t))
    vt_shape = jax.ShapeDtypeStruct((BATCH, KV_WIDTH, SEQ), BF16)
    return pl.pallas_call(
        _proj_kernel,
        grid=(BATCH, nt),
        in_specs=[
            tile(D_MODEL),
            pl.BlockSpec((1, D_MODEL), lambda b, t: (0, 0)),
            per_b, per_b,
            pl.BlockSpec((D_MODEL, IN_WIDTH), lambda b, t: (0, 0), pipeline_mode=pl.Buffered(1)),
            tab, tab,
        ],
        out_specs=[tile(ATTN_WIDTH), tile(KV_WIDTH), vt_spec, tile(CONV_WIDTH), tile(CONV_WIDTH),
                   tile(D_MODEL), tile(D_MODEL)],
        out_shape=[shp(ATTN_WIDTH), shp(KV_WIDTH), vt_shape, shp(CONV_WIDTH), shp(CONV_WIDTH),
                   shp(D_MODEL), shp(D_MODEL)],
        compiler_params=_params("arbitrary", "arbitrary"),
        name="proj",
    )(x, g1, sh1, sc1, w_in, cos, sin)


def _attn_kernel(sink_ref, q_ref, kp_ref, kc_ref, kn_ref, vp_ref, vc_ref, vn_ref,
                 kx_ref, vx_ref, o_ref):
    i = pl.program_id(1)
    cols = GROUP * Q_BLOCK
    key = lax.broadcasted_iota(jnp.int32, (Q_BLOCK, cols), 0)
    qry = lax.broadcasted_iota(jnp.int32, (Q_BLOCK, cols), 1) & (Q_BLOCK - 1)
    ok_prev = key >= qry + jnp.where(i > 0, 0, Q_BLOCK)
    ok_next = key <= qry - jnp.where(i < N_QBLK - 1, 0, Q_BLOCK)
    lane_head = lax.broadcasted_iota(jnp.int32, (1, cols), 1) >> 7
    n_keys = Q_BLOCK + 2 * WINDOW + CTX_LEN
    ones = jnp.ones((BF16_ROWS, n_keys), BF16)

    for g in range(N_KV_HEADS):
        ks = slice(g * HEAD_DIM, (g + 1) * HEAD_DIM)
        q4 = jnp.concatenate(
            [q_ref[0, :, (g * GROUP + j) * HEAD_DIM:(g * GROUP + j + 1) * HEAD_DIM]
             for j in range(GROUP)], axis=0)
        k_all = jnp.concatenate(
            [kp_ref[0, :, ks], kc_ref[0, :, ks], kn_ref[0, :, ks], kx_ref[0, :, ks]], axis=0)
        st = _dot_nt(k_all, q4)
        pieces = [jnp.where(ok_prev, st[:WINDOW], NEG_INF),
                  st[WINDOW:WINDOW + Q_BLOCK],
                  jnp.where(ok_next, st[WINDOW + Q_BLOCK:2 * WINDOW + Q_BLOCK], NEG_INF),
                  st[2 * WINDOW + Q_BLOCK:]]
        sink = jnp.zeros((1, cols), F32)
        for j in range(GROUP):
            sink = jnp.where(lane_head == j, sink_ref[g * GROUP + j], sink)
        m = sink
        for piece in pieces:
            m = jnp.maximum(m, jnp.max(piece, axis=0, keepdims=True))
        pt = jnp.concatenate([jnp.exp2((piece - m).astype(BF16)) for piece in pieces], axis=0)
        vt = jnp.concatenate([vp_ref[0, ks, :], vc_ref[0, ks, :], vn_ref[0, ks, :], vx_ref[0, ks, :]],
                             axis=1)
        ot = _dot(jnp.concatenate([vt, ones], axis=0), pt)
        denom = ot[HEAD_DIM:HEAD_DIM + 1] + jnp.exp2(sink - m)
        on = ot[:HEAD_DIM] * (1.0 / denom)
        for jj in range(GROUP // 2):
            pair_t = jnp.concatenate(
                [on[:, (2 * jj) * Q_BLOCK:(2 * jj + 1) * Q_BLOCK],
                 on[:, (2 * jj + 1) * Q_BLOCK:(2 * jj + 2) * Q_BLOCK]], axis=0)
            col = (g * GROUP + 2 * jj) * HEAD_DIM
            o_ref[0, :, col:col + 2 * HEAD_DIM] = pair_t.T.astype(BF16)


def _attn(sink, q, k, vt, k_ctx, vt_ctx):
    qspec = pl.BlockSpec((1, Q_BLOCK, ATTN_WIDTH), lambda b, i: (b, i, 0))
    prev_i = lambda i: jnp.maximum(i - 1, 0)
    next_i = lambda i: jnp.minimum(i + 1, N_QBLK - 1)
    same_i = lambda i: i
    kspec = lambda f: pl.BlockSpec((1, Q_BLOCK, KV_WIDTH), lambda b, i: (b, f(i), 0))
    vspec = lambda f: pl.BlockSpec((1, KV_WIDTH, Q_BLOCK), lambda b, i: (b, 0, f(i)))
    return pl.pallas_call(
        _attn_kernel,
        grid=(BATCH, N_QBLK),
        in_specs=[pl.BlockSpec(memory_space=pltpu.SMEM), qspec,
                  kspec(prev_i), kspec(same_i), kspec(next_i),
                  vspec(prev_i), vspec(same_i), vspec(next_i),
                  pl.BlockSpec((1, CTX_LEN, KV_WIDTH), lambda b, i: (b, 0, 0)),
                  pl.BlockSpec((1, KV_WIDTH, CTX_LEN), lambda b, i: (b, 0, 0))],
        out_specs=qspec,
        out_shape=jax.ShapeDtypeStruct((BATCH, SEQ, ATTN_WIDTH), BF16),
        compiler_params=_params("arbitrary", "arbitrary"),
        name="attn",
    )(sink, q, k, k, k, vt, vt, vt, k_ctx, vt_ctx)


def _merge_kernel(x_ref, oa_ref, z_ref, zp_ref, zn_ref, bg_ref, ga_ref, gc_ref, cw_ref,
                  wpa_ref, wpc_ref, wo_ref, g1_ref, sh2_ref, sc2_ref, n2_ref, wr_ref,
                  x1_ref, h2_ref, lg_ref):
    t = pl.program_id(1)
    nt = pl.num_programs(1)
    z = z_ref[0].astype(F32)
    z_before = zp_ref[0, BF16_ROWS - 1:BF16_ROWS, :].astype(F32) * jnp.where(t > 0, 1.0, 0.0)
    z_after = zn_ref[0, 0:1, :].astype(F32) * jnp.where(t < nt - 1, 1.0, 0.0)
    row = lax.broadcasted_iota(jnp.int32, (TOK_TILE, 1), 0)
    z_prev = jnp.where(row == 0, z_before, pltpu.roll(z, 1, 0))
    z_next = jnp.where(row == TOK_TILE - 1, z_after, pltpu.roll(z, TOK_TILE - 1, 0))
    cw = cw_ref[...]
    y = bg_ref[0].astype(F32) * (cw[0:1] * z_prev + cw[1:2] * z + cw[2:3] * z_next)
    a = _dot(oa_ref[0], wpa_ref[...])
    s = _dot(y.astype(BF16), wpc_ref[...])
    merged = _sigmoid(ga_ref[0].astype(F32)) * a + _sigmoid(gc_ref[0].astype(F32)) * s
    x1 = x_ref[0] + g1_ref[0] * _dot(merged.astype(BF16), wo_ref[...])
    x1_ref[0] = x1
    h2 = _norm_mod(x1, n2_ref[...], sh2_ref[0], sc2_ref[0])
    h_hi, h_lo = _split_bf16(h2)
    h2_ref[0] = h_hi
    both = _dot_nt(wr_ref[...], h_hi)
    lg_ref[0] = both[:N_EXPERTS] + both[N_EXPERTS:] + _dot_nt(wr_ref[:N_EXPERTS], h_lo)


def _merge(x, oa, z, bg, ga, gc, conv_w, wpa, wpc, wo, g1, sh2, sc2, n2, wr):
    nt = SEQ // TOK_TILE
    hb = TOK_TILE // BF16_ROWS
    n_hb = SEQ // BF16_ROWS
    tile = pl.BlockSpec((1, TOK_TILE, D_MODEL), lambda b, t: (b, t, 0))
    halo_p = pl.BlockSpec((1, BF16_ROWS, D_MODEL), lambda b, t: (b, jnp.maximum(t * hb - 1, 0), 0))
    halo_n = pl.BlockSpec((1, BF16_ROWS, D_MODEL),
                          lambda b, t: (b, jnp.minimum((t + 1) * hb, n_hb - 1), 0))
    per_b = pl.BlockSpec((1, 1, D_MODEL), lambda b, t: (b, 0, 0))
    full = lambda r, c: pl.BlockSpec((r, c), lambda b, t: (0, 0))
    return pl.pallas_call(
        _merge_kernel,
        grid=(BATCH, nt),
        in_specs=[tile, tile, tile, halo_p, halo_n, tile, tile, tile,
                  full(3, D_MODEL),
                  full(D_MODEL, D_MODEL), full(D_MODEL, D_MODEL), full(D_MODEL, D_MODEL),
                  per_b, per_b, per_b, full(1, D_MODEL), full(2 * N_EXPERTS, D_MODEL)],
        out_specs=[tile, tile, pl.BlockSpec((1, N_EXPERTS, TOK_TILE), lambda b, t: (b, 0, t))],
        out_shape=[jax.ShapeDtypeStruct((BATCH, SEQ, D_MODEL), F32),
                   jax.ShapeDtypeStruct((BATCH, SEQ, D_MODEL), BF16),
                   jax.ShapeDtypeStruct((BATCH, N_EXPERTS, SEQ), F32)],
        compiler_params=_params("arbitrary", "arbitrary"),
        name="merge",
    )(x, oa, z, z, z, bg, ga, gc, conv_w, wpa, wpc, wo, g1, sh2, sc2, n2, wr)


def _cumsum_excl(mf, upper):
    nblk = SEQ // LANES
    stacked = jnp.concatenate([mf[:, j * LANES:(j + 1) * LANES] for j in range(nblk)], axis=0)
    within = _dot(stacked.astype(BF16), upper)
    tot = jnp.sum(stacked, axis=1, keepdims=True)
    off = jnp.zeros((N_EXPERTS, 1), F32)
    out = []
    for j in range(nblk):
        rs = slice(j * N_EXPERTS, (j + 1) * N_EXPERTS)
        out.append(within[rs] + off)
        off = off + tot[rs]
    return jnp.concatenate(out, axis=1)


def _route_kernel(lg_ref, pos_ref, gate_ref):
    lg = lg_ref[0]
    ex = jnp.exp(lg - jnp.max(lg, axis=0, keepdims=True))
    aff = ex / jnp.sum(ex, axis=0, keepdims=True)

    def body(_, carry):
        lo, hi = carry
        mid = lo + ((hi - lo) >> 1)
        cnt = jnp.sum(jnp.where(aff >= pltpu.bitcast(mid, F32), 1.0, 0.0), axis=1, keepdims=True)
        ge = cnt >= CAP
        return jnp.where(ge, mid, lo), jnp.where(ge, hi, mid)

    one_bits = 0x3F800000
    lo0 = jnp.zeros((N_EXPERTS, 1), jnp.int32)
    hi0 = jnp.full((N_EXPERTS, 1), one_bits + 1, jnp.int32)
    lo, _ = lax.fori_loop(0, 31, body, (lo0, hi0))
    th = pltpu.bitcast(lo, F32)

    ku = lax.broadcasted_iota(jnp.int32, (LANES, LANES), 0)
    nu = lax.broadcasted_iota(jnp.int32, (LANES, LANES), 1)
    upper = jnp.where(ku < nu, 1.0, 0.0).astype(BF16)
    gtf = jnp.where(aff > th, 1.0, 0.0)
    eqf = jnp.where(aff == th, 1.0, 0.0)
    need = CAP - jnp.sum(gtf, axis=1, keepdims=True)
    sel = gtf + eqf * jnp.where(_cumsum_excl(eqf, upper) < need, 1.0, 0.0)
    pos = _cumsum_excl(sel, upper)
    pos_ref[0] = jnp.where(sel > 0.0, pos, -1.0)
    gate_ref[0] = jnp.where(sel > 0.0, aff, 0.0)


def _route(logits):
    spec = pl.BlockSpec((1, N_EXPERTS, SEQ), lambda b: (b, 0, 0))
    shape = jax.ShapeDtypeStruct((BATCH, N_EXPERTS, SEQ), F32)
    return pl.pallas_call(
        _route_kernel,
        grid=(BATCH,),
        in_specs=[spec],
        out_specs=[spec, spec],
        out_shape=[shape, shape],
        compiler_params=_params("arbitrary"),
        name="route",
    )(logits)


def _moe_kernel(pos_e_ref, pos_c_ref, gate_c_ref, h_ref, wg_ref, wu_ref, wdt_ref,
                x1_ref, g2_ref, fg_ref, o_ref, yet_ref):
    e = pl.program_id(1)

    @pl.when(e < N_EXPERTS)
    def _():
        slot = lax.broadcasted_iota(jnp.int32, (CAP, SEQ), 0).astype(F32)
        onehot = jnp.where(slot == pos_e_ref[0, 0], 1.0, 0.0).astype(BF16)
        xe = _dot(onehot, h_ref[0]).astype(BF16)
        act = _dot(xe, wg_ref[0])
        up = _dot(xe, wu_ref[0])
        hid = (act * _sigmoid(act) * up).astype(BF16)
        off = pl.multiple_of(e * CAP, CAP)
        yet_ref[:, pl.ds(off, CAP)] = _dot_nt(wdt_ref[0], hid).astype(BF16)

    @pl.when(e >= N_EXPERTS)
    def _():
        slot = lax.broadcasted_iota(jnp.int32, (CAP, SCAT_TOK), 0).astype(F32)
        scat = jnp.concatenate(
            [jnp.where(slot == pos_c_ref[0, ee:ee + 1, :], gate_c_ref[0, ee:ee + 1, :], 0.0)
             .astype(BF16) for ee in range(N_EXPERTS)], axis=0)
        moe = _dot(yet_ref[...], scat).T
        x2 = x1_ref[0] + g2_ref[0] * moe
        o_ref[0] = x2 * lax.rsqrt(jnp.mean(x2 * x2, axis=-1, keepdims=True) + EPS) * fg_ref[...]


def _moe(pos, gate, h2, wg, wu, wdt, x1, g2, fg):
    last = N_EXPERTS - 1
    expert = lambda e: jnp.minimum(e, last)
    chunk = lambda e: jnp.maximum(e - N_EXPERTS, 0)
    wspec = pl.BlockSpec((1, D_MODEL, EXPERT_FF), lambda b, e: (expert(e), 0, 0))
    sel_c = pl.BlockSpec((1, N_EXPERTS, SCAT_TOK), lambda b, e: (b, 0, chunk(e)))
    tok_c = pl.BlockSpec((1, SCAT_TOK, D_MODEL), lambda b, e: (b, chunk(e), 0))
    return pl.pallas_call(
        _moe_kernel,
        grid=(BATCH, N_EXPERTS + N_SCAT),
        in_specs=[pl.BlockSpec((1, 1, 1, SEQ), lambda b, e: (b, expert(e), 0, 0)),
                  sel_c, sel_c,
                  pl.BlockSpec((1, SEQ, D_MODEL), lambda b, e: (b, 0, 0)),
                  wspec, wspec, wspec,
                  tok_c,
                  pl.BlockSpec((1, 1, D_MODEL), lambda b, e: (b, 0, 0)),
                  pl.BlockSpec((1, D_MODEL), lambda b, e: (0, 0))],
        out_specs=tok_c,
        out_shape=jax.ShapeDtypeStruct((BATCH, SEQ, D_MODEL), F32),
        scratch_shapes=[pltpu.VMEM((D_MODEL, N_EXPERTS * CAP), BF16)],
        compiler_params=_params("arbitrary", "arbitrary"),
        name="moe",
    )(pos[:, :, None, :], pos, gate, h2, wg, wu, wdt, x1, g2, fg)


def _rope_tables():
    rows = SEQ // GRID_W
    row = jnp.repeat(jnp.arange(rows, dtype=F32), GRID_W)
    col = jnp.tile(jnp.arange(GRID_W, dtype=F32), rows)
    n_freq = HEAD_DIM // 4
    inv_freq = ROPE_BASE ** (-jnp.arange(n_freq, dtype=F32) / n_freq)
    ang_r = row[:, None] * inv_freq[None, :]
    ang_c = col[:, None] * inv_freq[None, :]
    ang = jnp.concatenate([ang_r, ang_r, ang_c, ang_c], axis=-1)
    sign = jnp.tile(jnp.concatenate([-jnp.ones(n_freq, F32), jnp.ones(n_freq, F32)]), 2)
    reps = LANES // HEAD_DIM
    return jnp.tile(jnp.cos(ang), (1, reps)), jnp.tile(jnp.sin(ang) * sign, (1, reps))


def kernel(x, c, ctx, c_ctx, w_mod, b_mod, norm1_g, w_in, attn_sink, conv_w, w_proj_attn,
           w_proj_conv, w_out, norm2_g, w_router, w_exp_gate, w_exp_up, w_exp_down, final_norm_g):
    assert x.shape == (BATCH, SEQ, D_MODEL) and ctx.shape == (BATCH, CTX_LEN, D_MODEL)
    assert w_mod.shape[0] == 1, "single-layer problem"
    layer = 0

    cvec = jnp.concatenate(
        [c, c_ctx[None, :], jnp.zeros((MOD_ROWS - BATCH - 1, D_MODEL), F32)], axis=0)
    mod = _mod(cvec, w_mod[layer], b_mod[layer][None, :])
    chunk = lambda rows, k: rows[:, k * D_MODEL:(k + 1) * D_MODEL]
    mod_x = mod[:BATCH][:, None, :]
    sh1, sc1, g1, sh2, sc2, g2 = (mod_x[..., k * D_MODEL:(k + 1) * D_MODEL] for k in range(N_MOD))
    mod_c = mod[BATCH:BATCH + 1]
    csh1, csc1 = chunk(mod_c, 0), chunk(mod_c, 1)

    n1 = norm1_g[layer][None, :]
    w_in_b = w_in[layer].astype(BF16)
    k_ctx, vt_ctx = _ctx_kv(ctx, n1, csh1, csc1, w_in_b[:, OFF_K:OFF_U])

    cos, sin = _rope_tables()
    q, k, vt, z, bg, ga, gc = _proj(x, n1, sh1, sc1, w_in_b, cos, sin)
    sink_log2 = attn_sink[layer].reshape(-1) * LOG2E
    o_attn = _attn(sink_log2, q, k, vt, k_ctx, vt_ctx)

    wr_hi, wr_lo = _split_bf16(w_router[layer].T)
    x1, h2, logits = _merge(
        x, o_attn, z, bg, ga, gc, conv_w[layer],
        w_proj_attn[layer].astype(BF16), w_proj_conv[layer].astype(BF16), w_out[layer].astype(BF16),
        g1, sh2, sc2, norm2_g[layer][None, :], jnp.concatenate([wr_hi, wr_lo], axis=0))

    pos, gate = _route(logits)
    wdt = jnp.swapaxes(w_exp_down[layer], 1, 2).astype(BF16)
    return _moe(pos, gate, h2, w_exp_gate[layer].astype(BF16), w_exp_up[layer].astype(BF16), wdt,
                x1, g2, final_norm_g[None, :])
```

```python
import jax
import jax.numpy as jnp
from jax import lax
from jax.experimental import pallas as pl
from jax.experimental.pallas import tpu as pltpu

D_MODEL = 1024
BATCH = 16
SEQ = 2048
CTX_LEN = 256
GRID_W = 64
N_HEADS = 16
N_KV_HEADS = 4
GROUP = N_HEADS // N_KV_HEADS
HEAD_DIM = D_MODEL // N_HEADS
ATTN_WIDTH = N_HEADS * HEAD_DIM
KV_WIDTH = N_KV_HEADS * HEAD_DIM
WINDOW = 128
Q_BLOCK = 128
ROPE_BASE = 10000.0
CONV_WIDTH = D_MODEL
N_EXPERTS = 16
EXPERT_FF = D_MODEL
CAPACITY_FACTOR = 2
N_MOD = 6
EPS = 1e-6
NEG_INF = -1e30
LOG2E = 1.4426950408889634

CAP = CAPACITY_FACTOR * SEQ // N_EXPERTS
N_QBLK = SEQ // Q_BLOCK
LANES = 128
F32_ROWS = 8
BF16_ROWS = 16
W3_WIDTH = 3 * D_MODEL + LANES
TOK_TILE = 256
ATTN_TILE = 512
MERGE_TILE = 512
SUB_TILE = 256
N_SCAT = 4
SCAT_TOK = SEQ // N_SCAT
MOD_ROWS = 32
VMEM_LIMIT = 56 * 1024 * 1024

OFF_Q = 0
OFF_K = OFF_Q + ATTN_WIDTH
OFF_V = OFF_K + KV_WIDTH
OFF_U = OFF_V + KV_WIDTH
OFF_B = OFF_U + CONV_WIDTH
OFF_C = OFF_B + CONV_WIDTH
OFF_GA = OFF_C + CONV_WIDTH
OFF_GC = OFF_GA + D_MODEL
IN_WIDTH = OFF_GC + D_MODEL

F32 = jnp.float32
BF16 = jnp.bfloat16


def _params(*sem):
    return pltpu.CompilerParams(dimension_semantics=sem, vmem_limit_bytes=VMEM_LIMIT)


def _dot(a, b):
    return jnp.dot(a, b, preferred_element_type=F32)


def _dot_nt(a, b):
    return lax.dot_general(a, b, (((1,), (1,)), ((), ())), preferred_element_type=F32)


def _sigmoid(x):
    return 1.0 / (1.0 + jnp.exp(-x))


def _split_bf16(x):
    hi = x.astype(BF16)
    lo = (x - hi.astype(F32)).astype(BF16)
    return hi, lo


def _norm_mod(x, g, shift, scale):
    y = x * lax.rsqrt(jnp.mean(x * x, axis=-1, keepdims=True) + EPS) * g
    return y * (1.0 + scale) + shift


def _mod_kernel(c_ref, w_ref, b_ref, o_ref):
    cv = c_ref[...]
    s_hi, s_lo = _split_bf16(cv * _sigmoid(cv))
    w_hi, w_lo = _split_bf16(w_ref[...])
    o_ref[...] = _dot(s_hi, w_hi) + _dot(s_lo, w_hi) + _dot(s_hi, w_lo) + b_ref[...]


def _mod(cvec, w_mod, b_mod):
    n_out = N_MOD * D_MODEL
    blk = D_MODEL
    return pl.pallas_call(
        _mod_kernel,
        grid=(n_out // blk,),
        in_specs=[
            pl.BlockSpec((MOD_ROWS, D_MODEL), lambda j: (0, 0)),
            pl.BlockSpec((D_MODEL, blk), lambda j: (0, j)),
            pl.BlockSpec((1, blk), lambda j: (0, j)),
        ],
        out_specs=pl.BlockSpec((MOD_ROWS, blk), lambda j: (0, j)),
        out_shape=jax.ShapeDtypeStruct((MOD_ROWS, n_out), F32),
        compiler_params=_params("arbitrary"),
        name="mod",
    )(cvec, w_mod, b_mod)


def _ctx_kernel(ctx_ref, g_ref, sh_ref, sc_ref, w_ref, k_ref, vt_ref):
    h = _norm_mod(ctx_ref[0], g_ref[...], sh_ref[...], sc_ref[...]).astype(BF16)
    kv = _dot(h, w_ref[...])
    k_ref[0] = kv[:, :KV_WIDTH].astype(BF16)
    vt_ref[0] = kv[:, KV_WIDTH:].T.astype(BF16)


def _ctx_kv(ctx, g1, csh1, csc1, w_kv):
    row = pl.BlockSpec((1, D_MODEL), lambda b: (0, 0))
    return pl.pallas_call(
        _ctx_kernel,
        grid=(BATCH,),
        in_specs=[
            pl.BlockSpec((1, CTX_LEN, D_MODEL), lambda b: (b, 0, 0)),
            row, row, row,
            pl.BlockSpec((D_MODEL, 2 * KV_WIDTH), lambda b: (0, 0)),
        ],
        out_specs=[pl.BlockSpec((1, CTX_LEN, KV_WIDTH), lambda b: (b, 0, 0)),
                   pl.BlockSpec((1, KV_WIDTH, CTX_LEN), lambda b: (b, 0, 0))],
        out_shape=[jax.ShapeDtypeStruct((BATCH, CTX_LEN, KV_WIDTH), BF16),
                   jax.ShapeDtypeStruct((BATCH, KV_WIDTH, CTX_LEN), BF16)],
        compiler_params=_params("arbitrary"),
        name="ctx_kv",
    )(ctx, g1, csh1, csc1, w_kv)


def _proj_kernel(x_ref, g_ref, sh_ref, sc_ref, w_ref, cos_ref, sin_ref,
                 q_ref, k_ref, vt_ref, z_ref, bg_ref, ga_ref, gc_ref):
    h = _norm_mod(x_ref[0], g_ref[...], sh_ref[0], sc_ref[0]).astype(BF16)
    cos = cos_ref[...]
    sin = sin_ref[...]
    lane = lax.broadcasted_iota(jnp.int32, (TOK_TILE, LANES), 1)
    first_half = (lane & (HEAD_DIM // 4)) == 0

    def rope(a):
        rot = jnp.where(first_half,
                        pltpu.roll(a, LANES - HEAD_DIM // 4, 1),
                        pltpu.roll(a, HEAD_DIM // 4, 1))
        return a * cos + rot * sin

    def proj(off, width):
        return _dot(h, w_ref[:, off:off + width])

    scale = HEAD_DIM ** -0.5 * LOG2E
    qa = proj(OFF_Q, ATTN_WIDTH)
    for j in range(ATTN_WIDTH // LANES):
        sl = slice(j * LANES, (j + 1) * LANES)
        q_ref[0, :, sl] = (rope(qa[:, sl]) * scale).astype(BF16)
    ka = proj(OFF_K, KV_WIDTH)
    for j in range(KV_WIDTH // LANES):
        sl = slice(j * LANES, (j + 1) * LANES)
        k_ref[0, :, sl] = rope(ka[:, sl]).astype(BF16)
    vt_ref[0] = proj(OFF_V, KV_WIDTH).T.astype(BF16)
    z_ref[0] = (proj(OFF_C, CONV_WIDTH) * proj(OFF_U, CONV_WIDTH)).astype(BF16)
    bg_ref[0] = proj(OFF_B, CONV_WIDTH).astype(BF16)
    ga_ref[0] = _sigmoid(proj(OFF_GA, D_MODEL)).astype(BF16)
    gc_ref[0] = _sigmoid(proj(OFF_GC, D_MODEL)).astype(BF16)


def _proj(x, g1, sh1, sc1, w_in, cos, sin):
    nt = SEQ // TOK_TILE
    tile = lambda w: pl.BlockSpec((1, TOK_TILE, w), lambda b, t: (b, t, 0))
    per_b = pl.BlockSpec((1, 1, D_MODEL), lambda b, t: (b, 0, 0))
    tab = pl.BlockSpec((TOK_TILE, LANES), lambda b, t: (t, 0))
    shp = lambda w: jax.ShapeDtypeStruct((BATCH, SEQ, w), BF16)
    vt_spec = pl.BlockSpec((1, KV_WIDTH, TOK_TILE), lambda b, t: (b, 0, t))
    vt_shape = jax.ShapeDtypeStruct((BATCH, KV_WIDTH, SEQ), BF16)
    return pl.pallas_call(
        _proj_kernel,
        grid=(BATCH, nt),
        in_specs=[
            tile(D_MODEL),
            pl.BlockSpec((1, D_MODEL), lambda b, t: (0, 0)),
            per_b, per_b,
            pl.BlockSpec((D_MODEL, IN_WIDTH), lambda b, t: (0, 0), pipeline_mode=pl.Buffered(1)),
            tab, tab,
        ],
        out_specs=[tile(ATTN_WIDTH), tile(KV_WIDTH), vt_spec, tile(CONV_WIDTH), tile(CONV_WIDTH),
                   tile(D_MODEL), tile(D_MODEL)],
        out_shape=[shp(ATTN_WIDTH), shp(KV_WIDTH), vt_shape, shp(CONV_WIDTH), shp(CONV_WIDTH),
                   shp(D_MODEL), shp(D_MODEL)],
        compiler_params=_params("arbitrary", "arbitrary"),
        name="proj",
    )(x, g1, sh1, sc1, w_in, cos, sin)


def _attn_kernel(sink_ref, q_ref, k_ref, vt_ref, kx_ref, vx_ref, o_ref):
    cols = GROUP * Q_BLOCK
    key = lax.broadcasted_iota(jnp.int32, (Q_BLOCK, cols), 0)
    qry = lax.broadcasted_iota(jnp.int32, (Q_BLOCK, cols), 1) & (Q_BLOCK - 1)
    lane_head = lax.broadcasted_iota(jnp.int32, (1, cols), 1) >> 7
    n_keys = Q_BLOCK + 2 * WINDOW + CTX_LEN
    ones = jnp.ones((BF16_ROWS, n_keys), BF16)

    blocks = []
    for sb in range(ATTN_TILE // Q_BLOCK):
        i = pl.program_id(1) * (ATTN_TILE // Q_BLOCK) + sb
        blocks.append(dict(
            rows=slice(sb * Q_BLOCK, (sb + 1) * Q_BLOCK),
            p0=pl.multiple_of(jnp.maximum(i - 1, 0) * Q_BLOCK, Q_BLOCK),
            c0=pl.multiple_of(i * Q_BLOCK, Q_BLOCK),
            n0=pl.multiple_of(jnp.minimum(i + 1, N_QBLK - 1) * Q_BLOCK, Q_BLOCK),
            ok_prev=key >= qry + jnp.where(i > 0, 0, Q_BLOCK),
            ok_next=key <= qry - jnp.where(i < N_QBLK - 1, 0, Q_BLOCK)))

    def scores(blk, g):
        ks = slice(g * HEAD_DIM, (g + 1) * HEAD_DIM)
        q4 = jnp.concatenate(
            [q_ref[0, blk["rows"], (g * GROUP + j) * HEAD_DIM:(g * GROUP + j + 1) * HEAD_DIM]
             for j in range(GROUP)], axis=0)
        k_all = jnp.concatenate(
            [k_ref[0, pl.ds(blk["p0"], Q_BLOCK), ks], k_ref[0, pl.ds(blk["c0"], Q_BLOCK), ks],
             k_ref[0, pl.ds(blk["n0"], Q_BLOCK), ks], kx_ref[0, :, ks]], axis=0)
        return _dot_nt(k_all, q4)

    def finish(blk, g, st):
        ks = slice(g * HEAD_DIM, (g + 1) * HEAD_DIM)
        pieces = [jnp.where(blk["ok_prev"], st[:WINDOW], NEG_INF),
                  st[WINDOW:WINDOW + Q_BLOCK],
                  jnp.where(blk["ok_next"], st[WINDOW + Q_BLOCK:2 * WINDOW + Q_BLOCK], NEG_INF),
                  st[2 * WINDOW + Q_BLOCK:]]
        sink = jnp.zeros((1, cols), F32)
        for j in range(GROUP):
            sink = jnp.where(lane_head == j, sink_ref[g * GROUP + j], sink)
        m = sink
        for piece in pieces:
            m = jnp.maximum(m, jnp.max(piece, axis=0, keepdims=True))
        pt = jnp.concatenate([jnp.exp2((piece - m).astype(BF16)) for piece in pieces], axis=0)
        vt = jnp.concatenate(
            [vt_ref[0, ks, pl.ds(blk["p0"], Q_BLOCK)], vt_ref[0, ks, pl.ds(blk["c0"], Q_BLOCK)],
             vt_ref[0, ks, pl.ds(blk["n0"], Q_BLOCK)], vx_ref[0, ks, :]], axis=1)
        ot = _dot(jnp.concatenate([vt, ones], axis=0), pt)
        denom = ot[HEAD_DIM:HEAD_DIM + 1] + jnp.exp2(sink - m)
        on = ot[:HEAD_DIM] * (1.0 / denom)
        for jj in range(GROUP // 2):
            pair_t = jnp.concatenate(
                [on[:, (2 * jj) * Q_BLOCK:(2 * jj + 1) * Q_BLOCK],
                 on[:, (2 * jj + 1) * Q_BLOCK:(2 * jj + 2) * Q_BLOCK]], axis=0)
            col = (g * GROUP + 2 * jj) * HEAD_DIM
            o_ref[0, blk["rows"], col:col + 2 * HEAD_DIM] = pair_t.T.astype(BF16)

    chains = [(blk, g) for blk in blocks for g in range(N_KV_HEADS)]
    st_next = scores(*chains[0])
    for n, chain in enumerate(chains):
        st = st_next
        if n + 1 < len(chains):
            st_next = scores(*chains[n + 1])
        finish(*chain, st)


def _attn(sink, q, k, vt, k_ctx, vt_ctx):
    qspec = pl.BlockSpec((1, ATTN_TILE, ATTN_WIDTH), lambda b, i: (b, i, 0))
    return pl.pallas_call(
        _attn_kernel,
        grid=(BATCH, SEQ // ATTN_TILE),
        in_specs=[pl.BlockSpec(memory_space=pltpu.SMEM), qspec,
                  pl.BlockSpec((1, SEQ, KV_WIDTH), lambda b, i: (b, 0, 0)),
                  pl.BlockSpec((1, KV_WIDTH, SEQ), lambda b, i: (b, 0, 0)),
                  pl.BlockSpec((1, CTX_LEN, KV_WIDTH), lambda b, i: (b, 0, 0)),
                  pl.BlockSpec((1, KV_WIDTH, CTX_LEN), lambda b, i: (b, 0, 0))],
        out_specs=qspec,
        out_shape=jax.ShapeDtypeStruct((BATCH, SEQ, ATTN_WIDTH), BF16),
        compiler_params=_params("arbitrary", "arbitrary"),
        name="attn",
    )(sink, q, k, vt, k_ctx, vt_ctx)


def _merge_kernel(x_ref, oa_ref, z_ref, zp_ref, zn_ref, bg_ref, ga_ref, gc_ref, cw_ref,
                  w3_ref, g1_ref, sh2_ref, sc2_ref, n2_ref, wr_ref,
                  x1_ref, h2_ref, lg_ref):
    t = pl.program_id(1)
    nt = pl.num_programs(1)
    z_before = zp_ref[0, BF16_ROWS - 1:BF16_ROWS, :].astype(F32) * jnp.where(t > 0, 1.0, 0.0)
    z_after = zn_ref[0, 0:1, :].astype(F32) * jnp.where(t < nt - 1, 1.0, 0.0)
    row = lax.broadcasted_iota(jnp.int32, (F32_ROWS, 1), 0)
    n_sub = MERGE_TILE // SUB_TILE
    subs = [slice(r * SUB_TILE, (r + 1) * SUB_TILE) for r in range(n_sub)]

    def conv(r):
        lo, hi = subs[r].start, subs[r].stop
        z = z_ref[0, subs[r], :].astype(F32)
        before = z_before if r == 0 else z_ref[0, lo - 1:lo, :].astype(F32)
        after = z_after if r == n_sub - 1 else z_ref[0, hi:hi + 1, :].astype(F32)
        up = pltpu.roll(z, 1, 0)
        z_prev = jnp.concatenate(
            [jnp.where(row == 0, before, up[:F32_ROWS]), up[F32_ROWS:]], axis=0)
        dn = pltpu.roll(z, SUB_TILE - 1, 0)
        z_next = jnp.concatenate(
            [dn[:-F32_ROWS], jnp.where(row == F32_ROWS - 1, after, dn[-F32_ROWS:])], axis=0)
        y = bg_ref[0, subs[r], :].astype(F32) * (
            cw_ref[0:1, :] * z_prev + cw_ref[1:2, :] * z + cw_ref[2:3, :] * z_next)
        return y.astype(BF16)

    a, s = [], []
    for r in range(n_sub):
        a.append(_dot(oa_ref[0, subs[r], :], w3_ref[:, 0:D_MODEL]))
        s.append(_dot(conv(r), w3_ref[:, D_MODEL:2 * D_MODEL]))
    branch = []
    for r, rs in enumerate(subs):
        merged = ga_ref[0, rs, :].astype(F32) * a[r] + gc_ref[0, rs, :].astype(F32) * s[r]
        branch.append(_dot(merged.astype(BF16), w3_ref[:, 2 * D_MODEL:3 * D_MODEL]))
    for r, rs in enumerate(subs):
        x1 = x_ref[0, rs, :] + g1_ref[0] * branch[r]
        x1_ref[0, rs, :] = x1
        h2 = _norm_mod(x1, n2_ref[...], sh2_ref[0], sc2_ref[0])
        h_hi, h_lo = _split_bf16(h2)
        h2_ref[0, rs, :] = h_hi
        both = _dot_nt(wr_ref[...], h_hi)
        lg_ref[0, :, rs] = (both[:N_EXPERTS] + both[N_EXPERTS:]
                            + _dot_nt(wr_ref[:N_EXPERTS], h_lo))


def _merge(x, oa, z, bg, ga, gc, conv_w, w3, g1, sh2, sc2, n2, wr):
    nt = SEQ // MERGE_TILE
    hb = MERGE_TILE // BF16_ROWS
    n_hb = SEQ // BF16_ROWS
    tile = pl.BlockSpec((1, MERGE_TILE, D_MODEL), lambda b, t: (b, t, 0))
    halo_p = pl.BlockSpec((1, BF16_ROWS, D_MODEL), lambda b, t: (b, jnp.maximum(t * hb - 1, 0), 0))
    halo_n = pl.BlockSpec((1, BF16_ROWS, D_MODEL),
                          lambda b, t: (b, jnp.minimum((t + 1) * hb, n_hb - 1), 0))
    per_b = pl.BlockSpec((1, 1, D_MODEL), lambda b, t: (b, 0, 0))
    full = lambda r, c: pl.BlockSpec((r, c), lambda b, t: (0, 0))
    return pl.pallas_call(
        _merge_kernel,
        grid=(BATCH, nt),
        in_specs=[tile, tile, tile, halo_p, halo_n, tile, tile, tile,
                  full(3, D_MODEL), full(D_MODEL, W3_WIDTH),
                  per_b, per_b, per_b, full(1, D_MODEL), full(2 * N_EXPERTS, D_MODEL)],
        out_specs=[tile, tile, pl.BlockSpec((1, N_EXPERTS, MERGE_TILE), lambda b, t: (b, 0, t))],
        out_shape=[jax.ShapeDtypeStruct((BATCH, SEQ, D_MODEL), F32),
                   jax.ShapeDtypeStruct((BATCH, SEQ, D_MODEL), BF16),
                   jax.ShapeDtypeStruct((BATCH, N_EXPERTS, SEQ), F32)],
        compiler_params=_params("arbitrary", "arbitrary"),
        name="merge",
    )(x, oa, z, z, z, bg, ga, gc, conv_w, w3, g1, sh2, sc2, n2, wr)


def _cumsum_excl(mf, upper):
    nblk = SEQ // LANES
    stacked = jnp.concatenate([mf[:, j * LANES:(j + 1) * LANES] for j in range(nblk)], axis=0)
    within = _dot(stacked.astype(BF16), upper)
    tot = jnp.sum(stacked, axis=1, keepdims=True)
    off = jnp.zeros((N_EXPERTS, 1), F32)
    out = []
    for j in range(nblk):
        rs = slice(j * N_EXPERTS, (j + 1) * N_EXPERTS)
        out.append(within[rs] + off)
        off = off + tot[rs]
    return jnp.concatenate(out, axis=1)


def _route_kernel(lg_ref, pos_ref, gate_ref):
    lg = lg_ref[0]
    ex = jnp.exp(lg - jnp.max(lg, axis=0, keepdims=True))
    aff = ex / jnp.sum(ex, axis=0, keepdims=True)

    def body(_, carry):
        lo, hi = carry
        mid = lo + ((hi - lo) >> 1)
        cnt = jnp.sum(jnp.where(aff >= pltpu.bitcast(mid, F32), 1.0, 0.0), axis=1, keepdims=True)
        ge = cnt >= CAP
        return jnp.where(ge, mid, lo), jnp.where(ge, hi, mid)

    one_bits = 0x3F800000
    lo0 = jnp.zeros((N_EXPERTS, 1), jnp.int32)
    hi0 = jnp.full((N_EXPERTS, 1), one_bits + 1, jnp.int32)
    lo, _ = lax.fori_loop(0, 31, body, (lo0, hi0))
    th = pltpu.bitcast(lo, F32)

    ku = lax.broadcasted_iota(jnp.int32, (LANES, LANES), 0)
    nu = lax.broadcasted_iota(jnp.int32, (LANES, LANES), 1)
    upper = jnp.where(ku < nu, 1.0, 0.0).astype(BF16)
    gtf = jnp.where(aff > th, 1.0, 0.0)
    eqf = jnp.where(aff == th, 1.0, 0.0)
    need = CAP - jnp.sum(gtf, axis=1, keepdims=True)
    sel = gtf + eqf * jnp.where(_cumsum_excl(eqf, upper) < need, 1.0, 0.0)
    pos = _cumsum_excl(sel, upper)
    pos_ref[0] = jnp.where(sel > 0.0, pos, -1.0)
    gate_ref[0] = jnp.where(sel > 0.0, aff, 0.0)


def _route(logits):
    spec = pl.BlockSpec((1, N_EXPERTS, SEQ), lambda b: (b, 0, 0))
    shape = jax.ShapeDtypeStruct((BATCH, N_EXPERTS, SEQ), F32)
    return pl.pallas_call(
        _route_kernel,
        grid=(BATCH,),
        in_specs=[spec],
        out_specs=[spec, spec],
        out_shape=[shape, shape],
        compiler_params=_params("arbitrary"),
        name="route",
    )(logits)


def _moe_kernel(pos_e_ref, pos_c_ref, gate_c_ref, h_ref, wg_ref, wu_ref, wdt_ref,
                x1_ref, g2_ref, fg_ref, o_ref, yet_ref):
    e = pl.program_id(1)

    @pl.when(e < N_EXPERTS)
    def _():
        slot = lax.broadcasted_iota(jnp.int32, (CAP, SEQ), 0).astype(F32)
        onehot = jnp.where(slot == pos_e_ref[0, 0], 1.0, 0.0).astype(BF16)
        xe = _dot(onehot, h_ref[0]).astype(BF16)
        act = _dot(xe, wg_ref[0])
        up = _dot(xe, wu_ref[0])
        hid = (act * _sigmoid(act) * up).astype(BF16)
        off = pl.multiple_of(e * CAP, CAP)
        yet_ref[:, pl.ds(off, CAP)] = _dot_nt(wdt_ref[0], hid).astype(BF16)

    @pl.when(e >= N_EXPERTS)
    def _():
        slot = lax.broadcasted_iota(jnp.int32, (CAP, SCAT_TOK), 0).astype(F32)
        scat = jnp.concatenate(
            [jnp.where(slot == pos_c_ref[0, ee:ee + 1, :], gate_c_ref[0, ee:ee + 1, :], 0.0)
             .astype(BF16) for ee in range(N_EXPERTS)], axis=0)
        moe = _dot(yet_ref[...], scat).T
        x2 = x1_ref[0] + g2_ref[0] * moe
        o_ref[0] = x2 * lax.rsqrt(jnp.mean(x2 * x2, axis=-1, keepdims=True) + EPS) * fg_ref[...]


def _moe(pos, gate, h2, wg, wu, wdt, x1, g2, fg):
    last = N_EXPERTS - 1
    expert = lambda e: jnp.minimum(e, last)
    chunk = lambda e: jnp.maximum(e - N_EXPERTS, 0)
    wspec = pl.BlockSpec((1, D_MODEL, EXPERT_FF), lambda b, e: (expert(e), 0, 0))
    sel_c = pl.BlockSpec((1, N_EXPERTS, SCAT_TOK), lambda b, e: (b, 0, chunk(e)))
    tok_c = pl.BlockSpec((1, SCAT_TOK, D_MODEL), lambda b, e: (b, chunk(e), 0))
    return pl.pallas_call(
        _moe_kernel,
        grid=(BATCH, N_EXPERTS + N_SCAT),
        in_specs=[pl.BlockSpec((1, 1, 1, SEQ), lambda b, e: (b, expert(e), 0, 0)),
                  sel_c, sel_c,
                  pl.BlockSpec((1, SEQ, D_MODEL), lambda b, e: (b, 0, 0)),
                  wspec, wspec, wspec,
                  tok_c,
                  pl.BlockSpec((1, 1, D_MODEL), lambda b, e: (b, 0, 0)),
                  pl.BlockSpec((1, D_MODEL), lambda b, e: (0, 0))],
        out_specs=tok_c,
        out_shape=jax.ShapeDtypeStruct((BATCH, SEQ, D_MODEL), F32),
        scratch_shapes=[pltpu.VMEM((D_MODEL, N_EXPERTS * CAP), BF16)],
        compiler_params=_params("arbitrary", "arbitrary"),
        name="moe",
    )(pos[:, :, None, :], pos, gate, h2, wg, wu, wdt, x1, g2, fg)


def _rope_tables():
    rows = SEQ // GRID_W
    row = jnp.repeat(jnp.arange(rows, dtype=F32), GRID_W)
    col = jnp.tile(jnp.arange(GRID_W, dtype=F32), rows)
    n_freq = HEAD_DIM // 4
    inv_freq = ROPE_BASE ** (-jnp.arange(n_freq, dtype=F32) / n_freq)
    ang_r = row[:, None] * inv_freq[None, :]
    ang_c = col[:, None] * inv_freq[None, :]
    ang = jnp.concatenate([ang_r, ang_r, ang_c, ang_c], axis=-1)
    sign = jnp.tile(jnp.concatenate([-jnp.ones(n_freq, F32), jnp.ones(n_freq, F32)]), 2)
    reps = LANES // HEAD_DIM
    return jnp.tile(jnp.cos(ang), (1, reps)), jnp.tile(jnp.sin(ang) * sign, (1, reps))


def kernel(x, c, ctx, c_ctx, w_mod, b_mod, norm1_g, w_in, attn_sink, conv_w, w_proj_attn,
           w_proj_conv, w_out, norm2_g, w_router, w_exp_gate, w_exp_up, w_exp_down, final_norm_g):
    assert x.shape == (BATCH, SEQ, D_MODEL) and ctx.shape == (BATCH, CTX_LEN, D_MODEL)
    assert w_mod.shape[0] == 1, "single-layer problem"
    layer = 0

    cvec = jnp.concatenate(
        [c, c_ctx[None, :], jnp.zeros((MOD_ROWS - BATCH - 1, D_MODEL), F32)], axis=0)
    mod = _mod(cvec, w_mod[layer], b_mod[layer][None, :])
    chunk = lambda rows, k: rows[:, k * D_MODEL:(k + 1) * D_MODEL]
    mod_x = mod[:BATCH][:, None, :]
    sh1, sc1, g1, sh2, sc2, g2 = (mod_x[..., k * D_MODEL:(k + 1) * D_MODEL] for k in range(N_MOD))
    mod_c = mod[BATCH:BATCH + 1]
    csh1, csc1 = chunk(mod_c, 0), chunk(mod_c, 1)

    n1 = norm1_g[layer][None, :]
    w_in_b = w_in[layer].astype(BF16)
    k_ctx, vt_ctx = _ctx_kv(ctx, n1, csh1, csc1, w_in_b[:, OFF_K:OFF_U])

    cos, sin = _rope_tables()
    q, k, vt, z, bg, ga, gc = _proj(x, n1, sh1, sc1, w_in_b, cos, sin)
    sink_log2 = attn_sink[layer].reshape(-1) * LOG2E
    o_attn = _attn(sink_log2, q, k, vt, k_ctx, vt_ctx)

    wr_hi, wr_lo = _split_bf16(w_router[layer].T)
    w3 = jnp.concatenate(
        [w_proj_attn[layer], w_proj_conv[layer], w_out[layer],
         jnp.zeros((D_MODEL, W3_WIDTH - 3 * D_MODEL), F32)], axis=1).astype(BF16)
    x1, h2, logits = _merge(
        x, o_attn, z, bg, ga, gc, conv_w[layer], w3,
        g1, sh2, sc2, norm2_g[layer][None, :], jnp.concatenate([wr_hi, wr_lo], axis=0))

    pos, gate = _route(logits)
    wdt = jnp.swapaxes(w_exp_down[layer], 1, 2).astype(BF16)
    return _moe(pos, gate, h2, w_exp_gate[layer].astype(BF16), w_exp_up[layer].astype(BF16), wdt,
                x1, g2, final_norm_g[None, :])
```

```python
import jax
import jax.numpy as jnp
from jax import lax
from jax.experimental import pallas as pl
from jax.experimental.pallas import tpu as pltpu

D_MODEL = 1024
BATCH = 16
SEQ = 2048
CTX_LEN = 256
GRID_W = 64
N_HEADS = 16
N_KV_HEADS = 4
GROUP = N_HEADS // N_KV_HEADS
HEAD_DIM = D_MODEL // N_HEADS
ATTN_WIDTH = N_HEADS * HEAD_DIM
KV_WIDTH = N_KV_HEADS * HEAD_DIM
WINDOW = 128
Q_BLOCK = 128
ROPE_BASE = 10000.0
CONV_WIDTH = D_MODEL
N_EXPERTS = 16
EXPERT_FF = D_MODEL
CAPACITY_FACTOR = 2
N_MOD = 6
EPS = 1e-6
NEG_INF = -1e30
LOG2E = 1.4426950408889634

CAP = CAPACITY_FACTOR * SEQ // N_EXPERTS
N_QBLK = SEQ // Q_BLOCK
LANES = 128
F32_ROWS = 8
BF16_ROWS = 16
W3_WIDTH = 3 * D_MODEL + LANES
TOK_TILE = 512
PROJ_SUB = 256
ATTN_TILE = 512
MERGE_TILE = 512
SUB_TILE = 256
N_SCAT = 4
SCAT_TOK = SEQ // N_SCAT
MOD_ROWS = 32
VMEM_LIMIT = 56 * 1024 * 1024

OFF_Q = 0
OFF_K = OFF_Q + ATTN_WIDTH
OFF_V = OFF_K + KV_WIDTH
OFF_U = OFF_V + KV_WIDTH
OFF_B = OFF_U + CONV_WIDTH
OFF_C = OFF_B + CONV_WIDTH
OFF_GA = OFF_C + CONV_WIDTH
OFF_GC = OFF_GA + D_MODEL
IN_WIDTH = OFF_GC + D_MODEL

F32 = jnp.float32
BF16 = jnp.bfloat16


def _params(*sem):
    return pltpu.CompilerParams(dimension_semantics=sem, vmem_limit_bytes=VMEM_LIMIT)


def _dot(a, b):
    return jnp.dot(a, b, preferred_element_type=F32)


def _dot_nt(a, b):
    return lax.dot_general(a, b, (((1,), (1,)), ((), ())), preferred_element_type=F32)


def _sigmoid(x):
    return 1.0 / (1.0 + jnp.exp(-x))


def _split_bf16(x):
    hi = x.astype(BF16)
    lo = (x - hi.astype(F32)).astype(BF16)
    return hi, lo


def _norm_mod(x, g, shift, scale):
    y = x * lax.rsqrt(jnp.mean(x * x, axis=-1, keepdims=True) + EPS) * g
    return y * (1.0 + scale) + shift


def _mod_kernel(c_ref, w_ref, b_ref, o_ref):
    cv = c_ref[...]
    s_hi, s_lo = _split_bf16(cv * _sigmoid(cv))
    w_hi, w_lo = _split_bf16(w_ref[...])
    o_ref[...] = _dot(s_hi, w_hi) + _dot(s_lo, w_hi) + _dot(s_hi, w_lo) + b_ref[...]


def _mod(cvec, w_mod, b_mod):
    n_out = N_MOD * D_MODEL
    blk = D_MODEL
    return pl.pallas_call(
        _mod_kernel,
        grid=(n_out // blk,),
        in_specs=[
            pl.BlockSpec((MOD_ROWS, D_MODEL), lambda j: (0, 0)),
            pl.BlockSpec((D_MODEL, blk), lambda j: (0, j)),
            pl.BlockSpec((1, blk), lambda j: (0, j)),
        ],
        out_specs=pl.BlockSpec((MOD_ROWS, blk), lambda j: (0, j)),
        out_shape=jax.ShapeDtypeStruct((MOD_ROWS, n_out), F32),
        compiler_params=_params("arbitrary"),
        name="mod",
    )(cvec, w_mod, b_mod)


def _ctx_kernel(ctx_ref, g_ref, sh_ref, sc_ref, w_ref, k_ref, vt_ref):
    h = _norm_mod(ctx_ref[0], g_ref[...], sh_ref[...], sc_ref[...]).astype(BF16)
    kv = _dot(h, w_ref[...])
    k_ref[0] = kv[:, :KV_WIDTH].astype(BF16)
    vt_ref[0] = kv[:, KV_WIDTH:].T.astype(BF16)


def _ctx_kv(ctx, g1, csh1, csc1, w_kv):
    row = pl.BlockSpec((1, D_MODEL), lambda b: (0, 0))
    return pl.pallas_call(
        _ctx_kernel,
        grid=(BATCH,),
        in_specs=[
            pl.BlockSpec((1, CTX_LEN, D_MODEL), lambda b: (b, 0, 0)),
            row, row, row,
            pl.BlockSpec((D_MODEL, 2 * KV_WIDTH), lambda b: (0, 0)),
        ],
        out_specs=[pl.BlockSpec((1, CTX_LEN, KV_WIDTH), lambda b: (b, 0, 0)),
                   pl.BlockSpec((1, KV_WIDTH, CTX_LEN), lambda b: (b, 0, 0))],
        out_shape=[jax.ShapeDtypeStruct((BATCH, CTX_LEN, KV_WIDTH), BF16),
                   jax.ShapeDtypeStruct((BATCH, KV_WIDTH, CTX_LEN), BF16)],
        compiler_params=_params("arbitrary"),
        name="ctx_kv",
    )(ctx, g1, csh1, csc1, w_kv)


def _proj_kernel(x_ref, g_ref, sh_ref, sc_ref, w_ref, cos_ref, sin_ref,
                 q_ref, k_ref, vt_ref, z_ref, bg_ref, ga_ref, gc_ref):
    lane = lax.broadcasted_iota(jnp.int32, (PROJ_SUB, LANES), 1)
    first_half = (lane & (HEAD_DIM // 4)) == 0
    scale = HEAD_DIM ** -0.5 * LOG2E

    for r in range(TOK_TILE // PROJ_SUB):
        rs = slice(r * PROJ_SUB, (r + 1) * PROJ_SUB)
        h = _norm_mod(x_ref[0, rs, :], g_ref[...], sh_ref[0], sc_ref[0]).astype(BF16)
        cos = cos_ref[rs, :]
        sin = sin_ref[rs, :]

        def rope(a):
            rot = jnp.where(first_half,
                            pltpu.roll(a, LANES - HEAD_DIM // 4, 1),
                            pltpu.roll(a, HEAD_DIM // 4, 1))
            return a * cos + rot * sin

        def proj(off, width):
            return _dot(h, w_ref[:, off:off + width])

        qa = proj(OFF_Q, ATTN_WIDTH)
        for j in range(ATTN_WIDTH // LANES):
            sl = slice(j * LANES, (j + 1) * LANES)
            q_ref[0, rs, sl] = (rope(qa[:, sl]) * scale).astype(BF16)
        ka = proj(OFF_K, KV_WIDTH)
        for j in range(KV_WIDTH // LANES):
            sl = slice(j * LANES, (j + 1) * LANES)
            k_ref[0, rs, sl] = rope(ka[:, sl]).astype(BF16)
        vt_ref[0, :, rs] = proj(OFF_V, KV_WIDTH).T.astype(BF16)
        z_ref[0, rs, :] = (proj(OFF_C, CONV_WIDTH) * proj(OFF_U, CONV_WIDTH)).astype(BF16)
        bg_ref[0, rs, :] = proj(OFF_B, CONV_WIDTH).astype(BF16)
        ga_ref[0, rs, :] = _sigmoid(proj(OFF_GA, D_MODEL)).astype(BF16)
        gc_ref[0, rs, :] = _sigmoid(proj(OFF_GC, D_MODEL)).astype(BF16)


def _proj(x, g1, sh1, sc1, w_in, cos, sin):
    nt = SEQ // TOK_TILE
    tile = lambda w: pl.BlockSpec((1, TOK_TILE, w), lambda b, t: (b, t, 0))
    per_b = pl.BlockSpec((1, 1, D_MODEL), lambda b, t: (b, 0, 0))
    tab = pl.BlockSpec((TOK_TILE, LANES), lambda b, t: (t, 0))
    shp = lambda w: jax.ShapeDtypeStruct((BATCH, SEQ, w), BF16)
    vt_spec = pl.BlockSpec((1, KV_WIDTH, TOK_TILE), lambda b, t: (b, 0, t))
    vt_shape = jax.ShapeDtypeStruct((BATCH, KV_WIDTH, SEQ), BF16)
    return pl.pallas_call(
        _proj_kernel,
        grid=(BATCH, nt),
        in_specs=[
            tile(D_MODEL),
            pl.BlockSpec((1, D_MODEL), lambda b, t: (0, 0)),
            per_b, per_b,
            pl.BlockSpec((D_MODEL, IN_WIDTH), lambda b, t: (0, 0), pipeline_mode=pl.Buffered(1)),
            tab, tab,
        ],
        out_specs=[tile(ATTN_WIDTH), tile(KV_WIDTH), vt_spec, tile(CONV_WIDTH), tile(CONV_WIDTH),
                   tile(D_MODEL), tile(D_MODEL)],
        out_shape=[shp(ATTN_WIDTH), shp(KV_WIDTH), vt_shape, shp(CONV_WIDTH), shp(CONV_WIDTH),
                   shp(D_MODEL), shp(D_MODEL)],
        compiler_params=_params("arbitrary", "arbitrary"),
        name="proj",
    )(x, g1, sh1, sc1, w_in, cos, sin)


def _attn_kernel(sink_ref, q_ref, k_ref, vt_ref, kx_ref, vx_ref, o_ref):
    cols = GROUP * Q_BLOCK
    key = lax.broadcasted_iota(jnp.int32, (Q_BLOCK, cols), 0)
    qry = lax.broadcasted_iota(jnp.int32, (Q_BLOCK, cols), 1) & (Q_BLOCK - 1)
    lane_head = lax.broadcasted_iota(jnp.int32, (1, cols), 1) >> 7
    n_keys = Q_BLOCK + 2 * WINDOW + CTX_LEN
    ones = jnp.ones((BF16_ROWS, n_keys), BF16)

    blocks = []
    for sb in range(ATTN_TILE // Q_BLOCK):
        i = pl.program_id(1) * (ATTN_TILE // Q_BLOCK) + sb
        blocks.append(dict(
            rows=slice(sb * Q_BLOCK, (sb + 1) * Q_BLOCK),
            p0=pl.multiple_of(jnp.maximum(i - 1, 0) * Q_BLOCK, Q_BLOCK),
            c0=pl.multiple_of(i * Q_BLOCK, Q_BLOCK),
            n0=pl.multiple_of(jnp.minimum(i + 1, N_QBLK - 1) * Q_BLOCK, Q_BLOCK),
            ok_prev=key >= qry + jnp.where(i > 0, 0, Q_BLOCK),
            ok_next=key <= qry - jnp.where(i < N_QBLK - 1, 0, Q_BLOCK)))

    def scores(blk, g):
        ks = slice(g * HEAD_DIM, (g + 1) * HEAD_DIM)
        q4 = jnp.concatenate(
            [q_ref[0, blk["rows"], (g * GROUP + j) * HEAD_DIM:(g * GROUP + j + 1) * HEAD_DIM]
             for j in range(GROUP)], axis=0)
        k_all = jnp.concatenate(
            [k_ref[0, pl.ds(blk["p0"], Q_BLOCK), ks], k_ref[0, pl.ds(blk["c0"], Q_BLOCK), ks],
             k_ref[0, pl.ds(blk["n0"], Q_BLOCK), ks], kx_ref[0, :, ks]], axis=0)
        return _dot_nt(k_all, q4)

    def softmax(blk, g, st):
        pieces = [jnp.where(blk["ok_prev"], st[:WINDOW], NEG_INF),
                  st[WINDOW:WINDOW + Q_BLOCK],
                  jnp.where(blk["ok_next"], st[WINDOW + Q_BLOCK:2 * WINDOW + Q_BLOCK], NEG_INF),
                  st[2 * WINDOW + Q_BLOCK:]]
        sink = jnp.zeros((1, cols), F32)
        for j in range(GROUP):
            sink = jnp.where(lane_head == j, sink_ref[g * GROUP + j], sink)
        m = sink
        for piece in pieces:
            m = jnp.maximum(m, jnp.max(piece, axis=0, keepdims=True))
        pt = jnp.concatenate([jnp.exp2((piece - m).astype(BF16)) for piece in pieces], axis=0)
        return pt, jnp.exp2(sink - m)

    def values(blk, g, pt, p_sink):
        ks = slice(g * HEAD_DIM, (g + 1) * HEAD_DIM)
        vt = jnp.concatenate(
            [vt_ref[0, ks, pl.ds(blk["p0"], Q_BLOCK)], vt_ref[0, ks, pl.ds(blk["c0"], Q_BLOCK)],
             vt_ref[0, ks, pl.ds(blk["n0"], Q_BLOCK)], vx_ref[0, ks, :]], axis=1)
        ot = _dot(jnp.concatenate([vt, ones], axis=0), pt)
        denom = ot[HEAD_DIM:HEAD_DIM + 1] + p_sink
        on = ot[:HEAD_DIM] * (1.0 / denom)
        for jj in range(GROUP // 2):
            pair_t = jnp.concatenate(
                [on[:, (2 * jj) * Q_BLOCK:(2 * jj + 1) * Q_BLOCK],
                 on[:, (2 * jj + 1) * Q_BLOCK:(2 * jj + 2) * Q_BLOCK]], axis=0)
            col = (g * GROUP + 2 * jj) * HEAD_DIM
            o_ref[0, blk["rows"], col:col + 2 * HEAD_DIM] = pair_t.T.astype(BF16)

    chains = [(blk, g) for blk in blocks for g in range(N_KV_HEADS)]
    st_next = scores(*chains[0])
    for n, chain in enumerate(chains):
        st = st_next
        if n + 1 < len(chains):
            st_next = scores(*chains[n + 1])
        values(*chain, *softmax(*chain, st))


def _attn(sink, q, k, vt, k_ctx, vt_ctx):
    qspec = pl.BlockSpec((1, ATTN_TILE, ATTN_WIDTH), lambda b, i: (b, i, 0))
    return pl.pallas_call(
        _attn_kernel,
        grid=(BATCH, SEQ // ATTN_TILE),
        in_specs=[pl.BlockSpec(memory_space=pltpu.SMEM), qspec,
                  pl.BlockSpec((1, SEQ, KV_WIDTH), lambda b, i: (b, 0, 0)),
                  pl.BlockSpec((1, KV_WIDTH, SEQ), lambda b, i: (b, 0, 0)),
                  pl.BlockSpec((1, CTX_LEN, KV_WIDTH), lambda b, i: (b, 0, 0)),
                  pl.BlockSpec((1, KV_WIDTH, CTX_LEN), lambda b, i: (b, 0, 0))],
        out_specs=qspec,
        out_shape=jax.ShapeDtypeStruct((BATCH, SEQ, ATTN_WIDTH), BF16),
        compiler_params=_params("arbitrary", "arbitrary"),
        name="attn",
    )(sink, q, k, vt, k_ctx, vt_ctx)


def _merge_kernel(x_ref, oa_ref, z_ref, zp_ref, zn_ref, bg_ref, ga_ref, gc_ref, cw_ref,
                  w3_ref, g1_ref, sh2_ref, sc2_ref, n2_ref, wr_ref,
                  x1_ref, h2_ref, lg_ref):
    t = pl.program_id(1)
    nt = pl.num_programs(1)
    z_before = zp_ref[0, BF16_ROWS - 1:BF16_ROWS, :].astype(F32) * jnp.where(t > 0, 1.0, 0.0)
    z_after = zn_ref[0, 0:1, :].astype(F32) * jnp.where(t < nt - 1, 1.0, 0.0)
    row = lax.broadcasted_iota(jnp.int32, (F32_ROWS, 1), 0)
    n_sub = MERGE_TILE // SUB_TILE
    subs = [slice(r * SUB_TILE, (r + 1) * SUB_TILE) for r in range(n_sub)]

    def conv(r):
        lo, hi = subs[r].start, subs[r].stop
        z = z_ref[0, subs[r], :].astype(F32)
        before = z_before if r == 0 else z_ref[0, lo - 1:lo, :].astype(F32)
        after = z_after if r == n_sub - 1 else z_ref[0, hi:hi + 1, :].astype(F32)
        up = pltpu.roll(z, 1, 0)
        z_prev = jnp.concatenate(
            [jnp.where(row == 0, before, up[:F32_ROWS]), up[F32_ROWS:]], axis=0)
        dn = pltpu.roll(z, SUB_TILE - 1, 0)
        z_next = jnp.concatenate(
            [dn[:-F32_ROWS], jnp.where(row == F32_ROWS - 1, after, dn[-F32_ROWS:])], axis=0)
        y = bg_ref[0, subs[r], :].astype(F32) * (
            cw_ref[0:1, :] * z_prev + cw_ref[1:2, :] * z + cw_ref[2:3, :] * z_next)
        return y.astype(BF16)

    a, s = [], []
    for r in range(n_sub):
        a.append(_dot(oa_ref[0, subs[r], :], w3_ref[:, 0:D_MODEL]))
        s.append(_dot(conv(r), w3_ref[:, D_MODEL:2 * D_MODEL]))
    branch = []
    for r, rs in enumerate(subs):
        merged = ga_ref[0, rs, :].astype(F32) * a[r] + gc_ref[0, rs, :].astype(F32) * s[r]
        branch.append(_dot(merged.astype(BF16), w3_ref[:, 2 * D_MODEL:3 * D_MODEL]))
    for r, rs in enumerate(subs):
        x1 = x_ref[0, rs, :] + g1_ref[0] * branch[r]
        x1_ref[0, rs, :] = x1
        h2 = _norm_mod(x1, n2_ref[...], sh2_ref[0], sc2_ref[0])
        h_hi, h_lo = _split_bf16(h2)
        h2_ref[0, rs, :] = h_hi
        both = _dot_nt(wr_ref[...], h_hi)
        lg_ref[0, :, rs] = (both[:N_EXPERTS] + both[N_EXPERTS:]
                            + _dot_nt(wr_ref[:N_EXPERTS], h_lo))


def _merge(x, oa, z, bg, ga, gc, conv_w, w3, g1, sh2, sc2, n2, wr):
    nt = SEQ // MERGE_TILE
    hb = MERGE_TILE // BF16_ROWS
    n_hb = SEQ // BF16_ROWS
    tile = pl.BlockSpec((1, MERGE_TILE, D_MODEL), lambda b, t: (b, t, 0))
    halo_p = pl.BlockSpec((1, BF16_ROWS, D_MODEL), lambda b, t: (b, jnp.maximum(t * hb - 1, 0), 0))
    halo_n = pl.BlockSpec((1, BF16_ROWS, D_MODEL),
                          lambda b, t: (b, jnp.minimum((t + 1) * hb, n_hb - 1), 0))
    per_b = pl.BlockSpec((1, 1, D_MODEL), lambda b, t: (b, 0, 0))
    full = lambda r, c: pl.BlockSpec((r, c), lambda b, t: (0, 0))
    return pl.pallas_call(
        _merge_kernel,
        grid=(BATCH, nt),
        in_specs=[tile, tile, tile, halo_p, halo_n, tile, tile, tile,
                  full(3, D_MODEL), full(D_MODEL, W3_WIDTH),
                  per_b, per_b, per_b, full(1, D_MODEL), full(2 * N_EXPERTS, D_MODEL)],
        out_specs=[tile, tile, pl.BlockSpec((1, N_EXPERTS, MERGE_TILE), lambda b, t: (b, 0, t))],
        out_shape=[jax.ShapeDtypeStruct((BATCH, SEQ, D_MODEL), F32),
                   jax.ShapeDtypeStruct((BATCH, SEQ, D_MODEL), BF16),
                   jax.ShapeDtypeStruct((BATCH, N_EXPERTS, SEQ), F32)],
        compiler_params=_params("arbitrary", "arbitrary"),
        name="merge",
    )(x, oa, z, z, z, bg, ga, gc, conv_w, w3, g1, sh2, sc2, n2, wr)


def _cumsum_excl(mf, upper):
    nblk = SEQ // LANES
    stacked = jnp.concatenate([mf[:, j * LANES:(j + 1) * LANES] for j in range(nblk)], axis=0)
    within = _dot(stacked.astype(BF16), upper)
    tot = jnp.sum(stacked, axis=1, keepdims=True)
    off = jnp.zeros((N_EXPERTS, 1), F32)
    out = []
    for j in range(nblk):
        rs = slice(j * N_EXPERTS, (j + 1) * N_EXPERTS)
        out.append(within[rs] + off)
        off = off + tot[rs]
    return jnp.concatenate(out, axis=1)


def _route_kernel(lg_ref, pos_ref, gate_ref):
    lg = lg_ref[0]
    ex = jnp.exp(lg - jnp.max(lg, axis=0, keepdims=True))
    aff = ex / jnp.sum(ex, axis=0, keepdims=True)

    def body(_, carry):
        lo, hi = carry
        mid = lo + ((hi - lo) >> 1)
        cnt = jnp.sum(jnp.where(aff >= pltpu.bitcast(mid, F32), 1.0, 0.0), axis=1, keepdims=True)
        ge = cnt >= CAP
        return jnp.where(ge, mid, lo), jnp.where(ge, hi, mid)

    one_bits = 0x3F800000
    lo0 = jnp.zeros((N_EXPERTS, 1), jnp.int32)
    hi0 = jnp.full((N_EXPERTS, 1), one_bits + 1, jnp.int32)
    lo, _ = lax.fori_loop(0, 31, body, (lo0, hi0))
    th = pltpu.bitcast(lo, F32)

    ku = lax.broadcasted_iota(jnp.int32, (LANES, LANES), 0)
    nu = lax.broadcasted_iota(jnp.int32, (LANES, LANES), 1)
    upper = jnp.where(ku < nu, 1.0, 0.0).astype(BF16)
    gtf = jnp.where(aff > th, 1.0, 0.0)
    eqf = jnp.where(aff == th, 1.0, 0.0)
    need = CAP - jnp.sum(gtf, axis=1, keepdims=True)
    sel = gtf + eqf * jnp.where(_cumsum_excl(eqf, upper) < need, 1.0, 0.0)
    pos = _cumsum_excl(sel, upper)
    pos_ref[0] = jnp.where(sel > 0.0, pos, -1.0)
    gate_ref[0] = jnp.where(sel > 0.0, aff, 0.0)


def _route(logits):
    spec = pl.BlockSpec((1, N_EXPERTS, SEQ), lambda b: (b, 0, 0))
    shape = jax.ShapeDtypeStruct((BATCH, N_EXPERTS, SEQ), F32)
    return pl.pallas_call(
        _route_kernel,
        grid=(BATCH,),
        in_specs=[spec],
        out_specs=[spec, spec],
        out_shape=[shape, shape],
        compiler_params=_params("arbitrary"),
        name="route",
    )(logits)


def _ffn_kernel(pos_ref, h_ref, wg_ref, wu_ref, wd_ref, yet_ref, wg_b, wu_b, wdt_b):
    @pl.when(pl.program_id(1) == 0)
    def _():
        wg_b[...] = wg_ref[0].astype(BF16)
        wu_b[...] = wu_ref[0].astype(BF16)
        wdt_b[...] = wd_ref[0].T.astype(BF16)

    slot = lax.broadcasted_iota(jnp.int32, (CAP, SEQ), 0).astype(F32)
    onehot = jnp.where(slot == pos_ref[0, 0], 1.0, 0.0).astype(BF16)
    xe = _dot(onehot, h_ref[0]).astype(BF16)
    act = _dot(xe, wg_b[...])
    up = _dot(xe, wu_b[...])
    hid = (act * _sigmoid(act) * up).astype(BF16)
    yet_ref[0] = _dot_nt(wdt_b[...], hid).astype(BF16)


def _ffn(pos, h2, wg, wu, wd):
    wspec = pl.BlockSpec((1, D_MODEL, EXPERT_FF), lambda e, b: (e, 0, 0))
    wscr = pltpu.VMEM((D_MODEL, EXPERT_FF), BF16)
    return pl.pallas_call(
        _ffn_kernel,
        grid=(N_EXPERTS, BATCH),
        in_specs=[pl.BlockSpec((1, 1, 1, SEQ), lambda e, b: (b, e, 0, 0)),
                  pl.BlockSpec((1, SEQ, D_MODEL), lambda e, b: (b, 0, 0)),
                  wspec, wspec, wspec],
        out_specs=pl.BlockSpec((1, D_MODEL, CAP), lambda e, b: (b, 0, e)),
        out_shape=jax.ShapeDtypeStruct((BATCH, D_MODEL, N_EXPERTS * CAP), BF16),
        scratch_shapes=[wscr, wscr, wscr],
        compiler_params=_params("arbitrary", "arbitrary"),
        name="ffn",
    )(pos[:, :, None, :], h2, wg, wu, wd)


def _combine_kernel(pos_ref, gate_ref, yet_ref, x1_ref, g2_ref, fg_ref, o_ref):
    slot = lax.broadcasted_iota(jnp.int32, (CAP, SCAT_TOK), 0).astype(F32)
    scat = jnp.concatenate(
        [jnp.where(slot == pos_ref[0, ee:ee + 1, :], gate_ref[0, ee:ee + 1, :], 0.0)
         .astype(BF16) for ee in range(N_EXPERTS)], axis=0)
    moe = _dot(yet_ref[0], scat).T
    x2 = x1_ref[0] + g2_ref[0] * moe
    o_ref[0] = x2 * lax.rsqrt(jnp.mean(x2 * x2, axis=-1, keepdims=True) + EPS) * fg_ref[...]


def _combine(pos, gate, yet, x1, g2, fg):
    sel_c = pl.BlockSpec((1, N_EXPERTS, SCAT_TOK), lambda b, c: (b, 0, c))
    tok_c = pl.BlockSpec((1, SCAT_TOK, D_MODEL), lambda b, c: (b, c, 0))
    return pl.pallas_call(
        _combine_kernel,
        grid=(BATCH, N_SCAT),
        in_specs=[sel_c, sel_c,
                  pl.BlockSpec((1, D_MODEL, N_EXPERTS * CAP), lambda b, c: (b, 0, 0)),
                  tok_c,
                  pl.BlockSpec((1, 1, D_MODEL), lambda b, c: (b, 0, 0)),
                  pl.BlockSpec((1, D_MODEL), lambda b, c: (0, 0))],
        out_specs=tok_c,
        out_shape=jax.ShapeDtypeStruct((BATCH, SEQ, D_MODEL), F32),
        compiler_params=_params("arbitrary", "arbitrary"),
        name="combine",
    )(pos, gate, yet, x1, g2, fg)


def _rope_tables():
    rows = SEQ // GRID_W
    row = jnp.repeat(jnp.arange(rows, dtype=F32), GRID_W)
    col = jnp.tile(jnp.arange(GRID_W, dtype=F32), rows)
    n_freq = HEAD_DIM // 4
    inv_freq = ROPE_BASE ** (-jnp.arange(n_freq, dtype=F32) / n_freq)
    ang_r = row[:, None] * inv_freq[None, :]
    ang_c = col[:, None] * inv_freq[None, :]
    ang = jnp.concatenate([ang_r, ang_r, ang_c, ang_c], axis=-1)
    sign = jnp.tile(jnp.concatenate([-jnp.ones(n_freq, F32), jnp.ones(n_freq, F32)]), 2)
    reps = LANES // HEAD_DIM
    return jnp.tile(jnp.cos(ang), (1, reps)), jnp.tile(jnp.sin(ang) * sign, (1, reps))


def kernel(x, c, ctx, c_ctx, w_mod, b_mod, norm1_g, w_in, attn_sink, conv_w, w_proj_attn,
           w_proj_conv, w_out, norm2_g, w_router, w_exp_gate, w_exp_up, w_exp_down, final_norm_g):
    assert x.shape == (BATCH, SEQ, D_MODEL) and ctx.shape == (BATCH, CTX_LEN, D_MODEL)
    assert w_mod.shape[0] == 1, "single-layer problem"
    layer = 0

    cvec = jnp.concatenate(
        [c, c_ctx[None, :], jnp.zeros((MOD_ROWS - BATCH - 1, D_MODEL), F32)], axis=0)
    mod = _mod(cvec, w_mod[layer], b_mod[layer][None, :])
    chunk = lambda rows, k: rows[:, k * D_MODEL:(k + 1) * D_MODEL]
    mod_x = mod[:BATCH][:, None, :]
    sh1, sc1, g1, sh2, sc2, g2 = (mod_x[..., k * D_MODEL:(k + 1) * D_MODEL] for k in range(N_MOD))
    mod_c = mod[BATCH:BATCH + 1]
    csh1, csc1 = chunk(mod_c, 0), chunk(mod_c, 1)

    n1 = norm1_g[layer][None, :]
    w_in_b = w_in[layer].astype(BF16)
    k_ctx, vt_ctx = _ctx_kv(ctx, n1, csh1, csc1, w_in_b[:, OFF_K:OFF_U])

    cos, sin = _rope_tables()
    q, k, vt, z, bg, ga, gc = _proj(x, n1, sh1, sc1, w_in_b, cos, sin)
    sink_log2 = attn_sink[layer].reshape(-1) * LOG2E
    o_attn = _attn(sink_log2, q, k, vt, k_ctx, vt_ctx)

    wr_hi, wr_lo = _split_bf16(w_router[layer].T)
    w3 = jnp.concatenate(
        [w_proj_attn[layer], w_proj_conv[layer], w_out[layer],
         jnp.zeros((D_MODEL, W3_WIDTH - 3 * D_MODEL), F32)], axis=1).astype(BF16)
    x1, h2, logits = _merge(
        x, o_attn, z, bg, ga, gc, conv_w[layer], w3,
        g1, sh2, sc2, norm2_g[layer][None, :], jnp.concatenate([wr_hi, wr_lo], axis=0))

    pos, gate = _route(logits)
    yet = _ffn(pos, h2, w_exp_gate[layer], w_exp_up[layer], w_exp_down[layer])
    return _combine(pos, gate, yet, x1, g2, final_norm_g[None, :])
```

```python
import jax
import jax.numpy as jnp
from jax import lax
from jax.experimental import pallas as pl
from jax.experimental.pallas import tpu as pltpu

D_MODEL = 1024
BATCH = 16
SEQ = 2048
CTX_LEN = 256
GRID_W = 64
N_HEADS = 16
N_KV_HEADS = 4
GROUP = N_HEADS // N_KV_HEADS
HEAD_DIM = D_MODEL // N_HEADS
ATTN_WIDTH = N_HEADS * HEAD_DIM
KV_WIDTH = N_KV_HEADS * HEAD_DIM
WINDOW = 128
Q_BLOCK = 128
ROPE_BASE = 10000.0
CONV_WIDTH = D_MODEL
N_EXPERTS = 16
EXPERT_FF = D_MODEL
CAPACITY_FACTOR = 2
N_MOD = 6
EPS = 1e-6
NEG_INF = -1e30
LOG2E = 1.4426950408889634

CAP = CAPACITY_FACTOR * SEQ // N_EXPERTS
N_QBLK = SEQ // Q_BLOCK
LANES = 128
F32_ROWS = 8
BF16_ROWS = 16
W3_WIDTH = 3 * D_MODEL + LANES
TOK_TILE = 512
PROJ_SUB = 256
ATTN_TILE = 512
MERGE_TILE = 512
SUB_TILE = 256
ROUTE_B = 4
N_SCAT = 4
SCAT_TOK = SEQ // N_SCAT
MOD_ROWS = 32
VMEM_LIMIT = 56 * 1024 * 1024

OFF_Q = 0
OFF_K = OFF_Q + ATTN_WIDTH
OFF_V = OFF_K + KV_WIDTH
OFF_U = OFF_V + KV_WIDTH
OFF_B = OFF_U + CONV_WIDTH
OFF_C = OFF_B + CONV_WIDTH
OFF_GA = OFF_C + CONV_WIDTH
OFF_GC = OFF_GA + D_MODEL
IN_WIDTH = OFF_GC + D_MODEL

F32 = jnp.float32
BF16 = jnp.bfloat16


def _params(*sem):
    return pltpu.CompilerParams(dimension_semantics=sem, vmem_limit_bytes=VMEM_LIMIT)


def _dot(a, b):
    return jnp.dot(a, b, preferred_element_type=F32)


def _dot_nt(a, b):
    return lax.dot_general(a, b, (((1,), (1,)), ((), ())), preferred_element_type=F32)


def _sigmoid(x):
    return 1.0 / (1.0 + jnp.exp(-x))


def _split_bf16(x):
    hi = x.astype(BF16)
    lo = (x - hi.astype(F32)).astype(BF16)
    return hi, lo


def _norm_mod(x, g, shift, scale):
    y = x * lax.rsqrt(jnp.mean(x * x, axis=-1, keepdims=True) + EPS) * g
    return y * (1.0 + scale) + shift


def _mod_kernel(c_ref, w_ref, b_ref, o_ref):
    cv = c_ref[...]
    s_hi, s_lo = _split_bf16(cv * _sigmoid(cv))
    w_hi, w_lo = _split_bf16(w_ref[...])
    o_ref[...] = _dot(s_hi, w_hi) + _dot(s_lo, w_hi) + _dot(s_hi, w_lo) + b_ref[...]


def _mod(cvec, w_mod, b_mod):
    n_out = N_MOD * D_MODEL
    blk = D_MODEL
    return pl.pallas_call(
        _mod_kernel,
        grid=(n_out // blk,),
        in_specs=[
            pl.BlockSpec((MOD_ROWS, D_MODEL), lambda j: (0, 0)),
            pl.BlockSpec((D_MODEL, blk), lambda j: (0, j)),
            pl.BlockSpec((1, blk), lambda j: (0, j)),
        ],
        out_specs=pl.BlockSpec((MOD_ROWS, blk), lambda j: (0, j)),
        out_shape=jax.ShapeDtypeStruct((MOD_ROWS, n_out), F32),
        compiler_params=_params("arbitrary"),
        name="mod",
    )(cvec, w_mod, b_mod)


def _ctx_kernel(ctx_ref, g_ref, sh_ref, sc_ref, w_ref, k_ref, vt_ref):
    h = _norm_mod(ctx_ref[0], g_ref[...], sh_ref[...], sc_ref[...]).astype(BF16)
    kv = _dot(h, w_ref[...])
    k_ref[0] = kv[:, :KV_WIDTH].astype(BF16)
    vt_ref[0] = kv[:, KV_WIDTH:].T.astype(BF16)


def _ctx_kv(ctx, g1, csh1, csc1, w_kv):
    row = pl.BlockSpec((1, D_MODEL), lambda b: (0, 0))
    return pl.pallas_call(
        _ctx_kernel,
        grid=(BATCH,),
        in_specs=[
            pl.BlockSpec((1, CTX_LEN, D_MODEL), lambda b: (b, 0, 0)),
            row, row, row,
            pl.BlockSpec((D_MODEL, 2 * KV_WIDTH), lambda b: (0, 0)),
        ],
        out_specs=[pl.BlockSpec((1, CTX_LEN, KV_WIDTH), lambda b: (b, 0, 0)),
                   pl.BlockSpec((1, KV_WIDTH, CTX_LEN), lambda b: (b, 0, 0))],
        out_shape=[jax.ShapeDtypeStruct((BATCH, CTX_LEN, KV_WIDTH), BF16),
                   jax.ShapeDtypeStruct((BATCH, KV_WIDTH, CTX_LEN), BF16)],
        compiler_params=_params("arbitrary"),
        name="ctx_kv",
    )(ctx, g1, csh1, csc1, w_kv)


def _proj_kernel(x_ref, g_ref, sh_ref, sc_ref, w_ref, cos_ref, sin_ref,
                 q_ref, k_ref, vt_ref, z_ref, bg_ref, ga_ref, gc_ref):
    lane = lax.broadcasted_iota(jnp.int32, (PROJ_SUB, LANES), 1)
    first_half = (lane & (HEAD_DIM // 4)) == 0
    scale = HEAD_DIM ** -0.5 * LOG2E

    for r in range(TOK_TILE // PROJ_SUB):
        rs = slice(r * PROJ_SUB, (r + 1) * PROJ_SUB)
        h = _norm_mod(x_ref[0, rs, :], g_ref[...], sh_ref[0], sc_ref[0]).astype(BF16)
        cos = cos_ref[rs, :]
        sin = sin_ref[rs, :]

        def rope(a):
            rot = jnp.where(first_half,
                            pltpu.roll(a, LANES - HEAD_DIM // 4, 1),
                            pltpu.roll(a, HEAD_DIM // 4, 1))
            return a * cos + rot * sin

        def proj(off, width):
            return _dot(h, w_ref[:, off:off + width])

        qa = proj(OFF_Q, ATTN_WIDTH)
        for j in range(ATTN_WIDTH // LANES):
            sl = slice(j * LANES, (j + 1) * LANES)
            q_ref[0, rs, sl] = (rope(qa[:, sl]) * scale).astype(BF16)
        ka = proj(OFF_K, KV_WIDTH)
        for j in range(KV_WIDTH // LANES):
            sl = slice(j * LANES, (j + 1) * LANES)
            k_ref[0, rs, sl] = rope(ka[:, sl]).astype(BF16)
        vt_ref[0, :, rs] = proj(OFF_V, KV_WIDTH).T.astype(BF16)
        z_ref[0, rs, :] = (proj(OFF_C, CONV_WIDTH) * proj(OFF_U, CONV_WIDTH)).astype(BF16)
        bg_ref[0, rs, :] = proj(OFF_B, CONV_WIDTH).astype(BF16)
        ga_ref[0, rs, :] = _sigmoid(proj(OFF_GA, D_MODEL)).astype(BF16)
        gc_ref[0, rs, :] = _sigmoid(proj(OFF_GC, D_MODEL)).astype(BF16)


def _proj(x, g1, sh1, sc1, w_in, cos, sin):
    nt = SEQ // TOK_TILE
    tile = lambda w: pl.BlockSpec((1, TOK_TILE, w), lambda b, t: (b, t, 0))
    per_b = pl.BlockSpec((1, 1, D_MODEL), lambda b, t: (b, 0, 0))
    tab = pl.BlockSpec((TOK_TILE, LANES), lambda b, t: (t, 0))
    shp = lambda w: jax.ShapeDtypeStruct((BATCH, SEQ, w), BF16)
    vt_spec = pl.BlockSpec((1, KV_WIDTH, TOK_TILE), lambda b, t: (b, 0, t))
    vt_shape = jax.ShapeDtypeStruct((BATCH, KV_WIDTH, SEQ), BF16)
    return pl.pallas_call(
        _proj_kernel,
        grid=(BATCH, nt),
        in_specs=[
            tile(D_MODEL),
            pl.BlockSpec((1, D_MODEL), lambda b, t: (0, 0)),
            per_b, per_b,
            pl.BlockSpec((D_MODEL, IN_WIDTH), lambda b, t: (0, 0), pipeline_mode=pl.Buffered(1)),
            tab, tab,
        ],
        out_specs=[tile(ATTN_WIDTH), tile(KV_WIDTH), vt_spec, tile(CONV_WIDTH), tile(CONV_WIDTH),
                   tile(D_MODEL), tile(D_MODEL)],
        out_shape=[shp(ATTN_WIDTH), shp(KV_WIDTH), vt_shape, shp(CONV_WIDTH), shp(CONV_WIDTH),
                   shp(D_MODEL), shp(D_MODEL)],
        compiler_params=_params("arbitrary", "arbitrary"),
        name="proj",
    )(x, g1, sh1, sc1, w_in, cos, sin)


def _attn_kernel(sink_ref, q_ref, k_ref, vt_ref, kx_ref, vx_ref, o_ref):
    cols = GROUP * Q_BLOCK
    key = lax.broadcasted_iota(jnp.int32, (Q_BLOCK, cols), 0)
    qry = lax.broadcasted_iota(jnp.int32, (Q_BLOCK, cols), 1) & (Q_BLOCK - 1)
    lane_head = lax.broadcasted_iota(jnp.int32, (1, cols), 1) >> 7
    n_keys = Q_BLOCK + 2 * WINDOW + CTX_LEN
    ones = jnp.ones((BF16_ROWS, n_keys), BF16)

    blocks = []
    for sb in range(ATTN_TILE // Q_BLOCK):
        i = pl.program_id(1) * (ATTN_TILE // Q_BLOCK) + sb
        blocks.append(dict(
            rows=slice(sb * Q_BLOCK, (sb + 1) * Q_BLOCK),
            p0=pl.multiple_of(jnp.maximum(i - 1, 0) * Q_BLOCK, Q_BLOCK),
            c0=pl.multiple_of(i * Q_BLOCK, Q_BLOCK),
            n0=pl.multiple_of(jnp.minimum(i + 1, N_QBLK - 1) * Q_BLOCK, Q_BLOCK),
            ok_prev=key >= qry + jnp.where(i > 0, 0, Q_BLOCK),
            ok_next=key <= qry - jnp.where(i < N_QBLK - 1, 0, Q_BLOCK)))

    def scores(blk, g):
        ks = slice(g * HEAD_DIM, (g + 1) * HEAD_DIM)
        q4 = jnp.concatenate(
            [q_ref[0, blk["rows"], (g * GROUP + j) * HEAD_DIM:(g * GROUP + j + 1) * HEAD_DIM]
             for j in range(GROUP)], axis=0)
        k_all = jnp.concatenate(
            [k_ref[0, pl.ds(blk["p0"], Q_BLOCK), ks], k_ref[0, pl.ds(blk["c0"], Q_BLOCK), ks],
             k_ref[0, pl.ds(blk["n0"], Q_BLOCK), ks], kx_ref[0, :, ks]], axis=0)
        return _dot_nt(k_all, q4)

    def softmax(blk, g, st):
        pieces = [jnp.where(blk["ok_prev"], st[:WINDOW], NEG_INF),
                  st[WINDOW:WINDOW + Q_BLOCK],
                  jnp.where(blk["ok_next"], st[WINDOW + Q_BLOCK:2 * WINDOW + Q_BLOCK], NEG_INF),
                  st[2 * WINDOW + Q_BLOCK:]]
        sink = jnp.zeros((1, cols), F32)
        for j in range(GROUP):
            sink = jnp.where(lane_head == j, sink_ref[g * GROUP + j], sink)
        m = sink
        for piece in pieces:
            m = jnp.maximum(m, jnp.max(piece, axis=0, keepdims=True))
        pt = jnp.concatenate([jnp.exp2((piece - m).astype(BF16)) for piece in pieces], axis=0)
        return pt, jnp.exp2(sink - m)

    def values(blk, g, pt, p_sink):
        ks = slice(g * HEAD_DIM, (g + 1) * HEAD_DIM)
        vt = jnp.concatenate(
            [vt_ref[0, ks, pl.ds(blk["p0"], Q_BLOCK)], vt_ref[0, ks, pl.ds(blk["c0"], Q_BLOCK)],
             vt_ref[0, ks, pl.ds(blk["n0"], Q_BLOCK)], vx_ref[0, ks, :]], axis=1)
        ot = _dot(jnp.concatenate([vt, ones], axis=0), pt)
        denom = ot[HEAD_DIM:HEAD_DIM + 1] + p_sink
        on = ot[:HEAD_DIM] * (1.0 / denom)
        for jj in range(GROUP // 2):
            pair_t = jnp.concatenate(
                [on[:, (2 * jj) * Q_BLOCK:(2 * jj + 1) * Q_BLOCK],
                 on[:, (2 * jj + 1) * Q_BLOCK:(2 * jj + 2) * Q_BLOCK]], axis=0)
            col = (g * GROUP + 2 * jj) * HEAD_DIM
            o_ref[0, blk["rows"], col:col + 2 * HEAD_DIM] = pair_t.T.astype(BF16)

    chains = [(blk, g) for blk in blocks for g in range(N_KV_HEADS)]
    st_next = scores(*chains[0])
    for n, chain in enumerate(chains):
        st = st_next
        if n + 1 < len(chains):
            st_next = scores(*chains[n + 1])
        values(*chain, *softmax(*chain, st))


def _attn(sink, q, k, vt, k_ctx, vt_ctx):
    qspec = pl.BlockSpec((1, ATTN_TILE, ATTN_WIDTH), lambda b, i: (b, i, 0))
    return pl.pallas_call(
        _attn_kernel,
        grid=(BATCH, SEQ // ATTN_TILE),
        in_specs=[pl.BlockSpec(memory_space=pltpu.SMEM), qspec,
                  pl.BlockSpec((1, SEQ, KV_WIDTH), lambda b, i: (b, 0, 0)),
                  pl.BlockSpec((1, KV_WIDTH, SEQ), lambda b, i: (b, 0, 0)),
                  pl.BlockSpec((1, CTX_LEN, KV_WIDTH), lambda b, i: (b, 0, 0)),
                  pl.BlockSpec((1, KV_WIDTH, CTX_LEN), lambda b, i: (b, 0, 0))],
        out_specs=qspec,
        out_shape=jax.ShapeDtypeStruct((BATCH, SEQ, ATTN_WIDTH), BF16),
        compiler_params=_params("arbitrary", "arbitrary"),
        name="attn",
    )(sink, q, k, vt, k_ctx, vt_ctx)


def _merge_kernel(x_ref, oa_ref, z_ref, zp_ref, zn_ref, bg_ref, ga_ref, gc_ref, cw_ref,
                  w3_ref, g1_ref, sh2_ref, sc2_ref, n2_ref, wr_ref,
                  x1_ref, h2_ref, lg_ref):
    t = pl.program_id(1)
    nt = pl.num_programs(1)
    z_before = zp_ref[0, BF16_ROWS - 1:BF16_ROWS, :].astype(F32) * jnp.where(t > 0, 1.0, 0.0)
    z_after = zn_ref[0, 0:1, :].astype(F32) * jnp.where(t < nt - 1, 1.0, 0.0)
    row = lax.broadcasted_iota(jnp.int32, (F32_ROWS, 1), 0)
    n_sub = MERGE_TILE // SUB_TILE
    subs = [slice(r * SUB_TILE, (r + 1) * SUB_TILE) for r in range(n_sub)]

    def conv(r):
        lo, hi = subs[r].start, subs[r].stop
        z = z_ref[0, subs[r], :].astype(F32)
        before = z_before if r == 0 else z_ref[0, lo - 1:lo, :].astype(F32)
        after = z_after if r == n_sub - 1 else z_ref[0, hi:hi + 1, :].astype(F32)
        up = pltpu.roll(z, 1, 0)
        z_prev = jnp.concatenate(
            [jnp.where(row == 0, before, up[:F32_ROWS]), up[F32_ROWS:]], axis=0)
        dn = pltpu.roll(z, SUB_TILE - 1, 0)
        z_next = jnp.concatenate(
            [dn[:-F32_ROWS], jnp.where(row == F32_ROWS - 1, after, dn[-F32_ROWS:])], axis=0)
        y = bg_ref[0, subs[r], :].astype(F32) * (
            cw_ref[0:1, :] * z_prev + cw_ref[1:2, :] * z + cw_ref[2:3, :] * z_next)
        return y.astype(BF16)

    a, s = [], []
    for r in range(n_sub):
        a.append(_dot(oa_ref[0, subs[r], :], w3_ref[:, 0:D_MODEL]))
        s.append(_dot(conv(r), w3_ref[:, D_MODEL:2 * D_MODEL]))
    branch = []
    for r, rs in enumerate(subs):
        merged = ga_ref[0, rs, :].astype(F32) * a[r] + gc_ref[0, rs, :].astype(F32) * s[r]
        branch.append(_dot(merged.astype(BF16), w3_ref[:, 2 * D_MODEL:3 * D_MODEL]))
    for r, rs in enumerate(subs):
        x1 = x_ref[0, rs, :] + g1_ref[0] * branch[r]
        x1_ref[0, rs, :] = x1
        h2 = _norm_mod(x1, n2_ref[...], sh2_ref[0], sc2_ref[0])
        h_hi, h_lo = _split_bf16(h2)
        h2_ref[0, rs, :] = h_hi
        both = _dot_nt(wr_ref[...], h_hi)
        lg_ref[0, :, rs] = (both[:N_EXPERTS] + both[N_EXPERTS:]
                            + _dot_nt(wr_ref[:N_EXPERTS], h_lo))


def _merge(x, oa, z, bg, ga, gc, conv_w, w3, g1, sh2, sc2, n2, wr):
    nt = SEQ // MERGE_TILE
    hb = MERGE_TILE // BF16_ROWS
    n_hb = SEQ // BF16_ROWS
    tile = pl.BlockSpec((1, MERGE_TILE, D_MODEL), lambda b, t: (b, t, 0))
    halo_p = pl.BlockSpec((1, BF16_ROWS, D_MODEL), lambda b, t: (b, jnp.maximum(t * hb - 1, 0), 0))
    halo_n = pl.BlockSpec((1, BF16_ROWS, D_MODEL),
                          lambda b, t: (b, jnp.minimum((t + 1) * hb, n_hb - 1), 0))
    per_b = pl.BlockSpec((1, 1, D_MODEL), lambda b, t: (b, 0, 0))
    full = lambda r, c: pl.BlockSpec((r, c), lambda b, t: (0, 0))
    return pl.pallas_call(
        _merge_kernel,
        grid=(BATCH, nt),
        in_specs=[tile, tile, tile, halo_p, halo_n, tile, tile, tile,
                  full(3, D_MODEL), full(D_MODEL, W3_WIDTH),
                  per_b, per_b, per_b, full(1, D_MODEL), full(2 * N_EXPERTS, D_MODEL)],
        out_specs=[tile, tile, pl.BlockSpec((1, N_EXPERTS, MERGE_TILE), lambda b, t: (b, 0, t))],
        out_shape=[jax.ShapeDtypeStruct((BATCH, SEQ, D_MODEL), F32),
                   jax.ShapeDtypeStruct((BATCH, SEQ, D_MODEL), BF16),
                   jax.ShapeDtypeStruct((BATCH, N_EXPERTS, SEQ), F32)],
        compiler_params=_params("arbitrary", "arbitrary"),
        name="merge",
    )(x, oa, z, z, z, bg, ga, gc, conv_w, w3, g1, sh2, sc2, n2, wr)


def _cumsum_excl(mf, upper):
    nblk = SEQ // LANES
    rows = mf.shape[0]
    stacked = jnp.concatenate([mf[:, j * LANES:(j + 1) * LANES] for j in range(nblk)], axis=0)
    within = _dot(stacked.astype(BF16), upper)
    tot = jnp.sum(stacked, axis=1, keepdims=True)
    off = jnp.zeros((rows, 1), F32)
    out = []
    for j in range(nblk):
        rs = slice(j * rows, (j + 1) * rows)
        out.append(within[rs] + off)
        off = off + tot[rs]
    return jnp.concatenate(out, axis=1)


def _route_kernel(lg_ref, pos_ref, gate_ref):
    rows = ROUTE_B * N_EXPERTS
    lg = lg_ref[...]
    ex = jnp.exp(lg - jnp.max(lg, axis=1, keepdims=True))
    aff = (ex / jnp.sum(ex, axis=1, keepdims=True)).reshape(rows, SEQ)

    def body(_, carry):
        lo, hi = carry
        mid = lo + ((hi - lo) >> 1)
        cnt = jnp.sum(jnp.where(aff >= pltpu.bitcast(mid, F32), 1.0, 0.0), axis=1, keepdims=True)
        ge = cnt >= CAP
        return jnp.where(ge, mid, lo), jnp.where(ge, hi, mid)

    one_bits = 0x3F800000
    lo0 = jnp.zeros((rows, 1), jnp.int32)
    hi0 = jnp.full((rows, 1), one_bits + 1, jnp.int32)
    lo, _ = lax.fori_loop(0, 31, body, (lo0, hi0))
    th = pltpu.bitcast(lo, F32)

    ku = lax.broadcasted_iota(jnp.int32, (LANES, LANES), 0)
    nu = lax.broadcasted_iota(jnp.int32, (LANES, LANES), 1)
    upper = jnp.where(ku < nu, 1.0, 0.0).astype(BF16)
    gtf = jnp.where(aff > th, 1.0, 0.0)
    eqf = jnp.where(aff == th, 1.0, 0.0)
    need = CAP - jnp.sum(gtf, axis=1, keepdims=True)
    sel = gtf + eqf * jnp.where(_cumsum_excl(eqf, upper) < need, 1.0, 0.0)
    pos = _cumsum_excl(sel, upper)
    pos_ref[...] = jnp.where(sel > 0.0, pos, -1.0).reshape(ROUTE_B, N_EXPERTS, SEQ)
    gate_ref[...] = jnp.where(sel > 0.0, aff, 0.0).reshape(ROUTE_B, N_EXPERTS, SEQ)


def _route(logits):
    spec = pl.BlockSpec((ROUTE_B, N_EXPERTS, SEQ), lambda b: (b, 0, 0))
    shape = jax.ShapeDtypeStruct((BATCH, N_EXPERTS, SEQ), F32)
    return pl.pallas_call(
        _route_kernel,
        grid=(BATCH // ROUTE_B,),
        in_specs=[spec],
        out_specs=[spec, spec],
        out_shape=[shape, shape],
        compiler_params=_params("arbitrary"),
        name="route",
    )(logits)


def _ffn_kernel(pos_ref, h_ref, wg_ref, wu_ref, wd_ref, yet_ref, wg_b, wu_b, wdt_b):
    @pl.when(pl.program_id(1) == 0)
    def _():
        wg_b[...] = wg_ref[0].astype(BF16)
        wu_b[...] = wu_ref[0].astype(BF16)
        wdt_b[...] = wd_ref[0].T.astype(BF16)

    slot = lax.broadcasted_iota(jnp.int32, (CAP, SEQ), 0).astype(F32)
    onehot = jnp.where(slot == pos_ref[0, 0], 1.0, 0.0).astype(BF16)
    xe = _dot(onehot, h_ref[0]).astype(BF16)
    act = _dot(xe, wg_b[...])
    up = _dot(xe, wu_b[...])
    hid = (act * _sigmoid(act) * up).astype(BF16)
    yet_ref[0] = _dot_nt(wdt_b[...], hid).astype(BF16)


def _ffn(pos, h2, wg, wu, wd):
    def wspec(lead):
        return pl.BlockSpec(
            (1, D_MODEL, EXPERT_FF),
            lambda e, b: (jnp.minimum(e + jnp.where(b >= BATCH - lead, 1, 0), N_EXPERTS - 1),
                          0, 0))
    wscr = pltpu.VMEM((D_MODEL, EXPERT_FF), BF16)
    return pl.pallas_call(
        _ffn_kernel,
        grid=(N_EXPERTS, BATCH),
        in_specs=[pl.BlockSpec((1, 1, 1, SEQ), lambda e, b: (b, e, 0, 0)),
                  pl.BlockSpec((1, SEQ, D_MODEL), lambda e, b: (b, 0, 0)),
                  wspec(3), wspec(2), wspec(1)],
        out_specs=pl.BlockSpec((1, D_MODEL, CAP), lambda e, b: (b, 0, e)),
        out_shape=jax.ShapeDtypeStruct((BATCH, D_MODEL, N_EXPERTS * CAP), BF16),
        scratch_shapes=[wscr, wscr, wscr],
        compiler_params=_params("arbitrary", "arbitrary"),
        name="ffn",
    )(pos[:, :, None, :], h2, wg, wu, wd)


def _combine_kernel(pos_ref, gate_ref, yet_ref, x1_ref, g2_ref, fg_ref, o_ref):
    slot = lax.broadcasted_iota(jnp.int32, (CAP, SCAT_TOK), 0).astype(F32)
    scat = jnp.concatenate(
        [jnp.where(slot == pos_ref[0, ee:ee + 1, :], gate_ref[0, ee:ee + 1, :], 0.0)
         .astype(BF16) for ee in range(N_EXPERTS)], axis=0)
    moe = _dot(yet_ref[0], scat).T
    x2 = x1_ref[0] + g2_ref[0] * moe
    o_ref[0] = x2 * lax.rsqrt(jnp.mean(x2 * x2, axis=-1, keepdims=True) + EPS) * fg_ref[...]


def _combine(pos, gate, yet, x1, g2, fg):
    sel_c = pl.BlockSpec((1, N_EXPERTS, SCAT_TOK), lambda b, c: (b, 0, c))
    tok_c = pl.BlockSpec((1, SCAT_TOK, D_MODEL), lambda b, c: (b, c, 0))
    return pl.pallas_call(
        _combine_kernel,
        grid=(BATCH, N_SCAT),
        in_specs=[sel_c, sel_c,
                  pl.BlockSpec((1, D_MODEL, N_EXPERTS * CAP), lambda b, c: (b, 0, 0)),
                  tok_c,
                  pl.BlockSpec((1, 1, D_MODEL), lambda b, c: (b, 0, 0)),
                  pl.BlockSpec((1, D_MODEL), lambda b, c: (0, 0))],
        out_specs=tok_c,
        out_shape=jax.ShapeDtypeStruct((BATCH, SEQ, D_MODEL), F32),
        compiler_params=_params("arbitrary", "arbitrary"),
        name="combine",
    )(pos, gate, yet, x1, g2, fg)


def _rope_tables():
    rows = SEQ // GRID_W
    row = jnp.repeat(jnp.arange(rows, dtype=F32), GRID_W)
    col = jnp.tile(jnp.arange(GRID_W, dtype=F32), rows)
    n_freq = HEAD_DIM // 4
    inv_freq = ROPE_BASE ** (-jnp.arange(n_freq, dtype=F32) / n_freq)
    ang_r = row[:, None] * inv_freq[None, :]
    ang_c = col[:, None] * inv_freq[None, :]
    ang = jnp.concatenate([ang_r, ang_r, ang_c, ang_c], axis=-1)
    sign = jnp.tile(jnp.concatenate([-jnp.ones(n_freq, F32), jnp.ones(n_freq, F32)]), 2)
    reps = LANES // HEAD_DIM
    return jnp.tile(jnp.cos(ang), (1, reps)), jnp.tile(jnp.sin(ang) * sign, (1, reps))


def kernel(x, c, ctx, c_ctx, w_mod, b_mod, norm1_g, w_in, attn_sink, conv_w, w_proj_attn,
           w_proj_conv, w_out, norm2_g, w_router, w_exp_gate, w_exp_up, w_exp_down, final_norm_g):
    assert x.shape == (BATCH, SEQ, D_MODEL) and ctx.shape == (BATCH, CTX_LEN, D_MODEL)
    assert w_mod.shape[0] == 1, "single-layer problem"
    layer = 0

    cvec = jnp.concatenate(
        [c, c_ctx[None, :], jnp.zeros((MOD_ROWS - BATCH - 1, D_MODEL), F32)], axis=0)
    mod = _mod(cvec, w_mod[layer], b_mod[layer][None, :])
    chunk = lambda rows, k: rows[:, k * D_MODEL:(k + 1) * D_MODEL]
    mod_x = mod[:BATCH][:, None, :]
    sh1, sc1, g1, sh2, sc2, g2 = (mod_x[..., k * D_MODEL:(k + 1) * D_MODEL] for k in range(N_MOD))
    mod_c = mod[BATCH:BATCH + 1]
    csh1, csc1 = chunk(mod_c, 0), chunk(mod_c, 1)

    n1 = norm1_g[layer][None, :]
    w_in_b = w_in[layer].astype(BF16)
    k_ctx, vt_ctx = _ctx_kv(ctx, n1, csh1, csc1, w_in_b[:, OFF_K:OFF_U])

    cos, sin = _rope_tables()
    q, k, vt, z, bg, ga, gc = _proj(x, n1, sh1, sc1, w_in_b, cos, sin)
    sink_log2 = attn_sink[layer].reshape(-1) * LOG2E
    o_attn = _attn(sink_log2, q, k, vt, k_ctx, vt_ctx)

    wr_hi, wr_lo = _split_bf16(w_router[layer].T)
    w3 = jnp.concatenate(
        [w_proj_attn[layer], w_proj_conv[layer], w_out[layer],
         jnp.zeros((D_MODEL, W3_WIDTH - 3 * D_MODEL), F32)], axis=1).astype(BF16)
    x1, h2, logits = _merge(
        x, o_attn, z, bg, ga, gc, conv_w[layer], w3,
        g1, sh2, sc2, norm2_g[layer][None, :], jnp.concatenate([wr_hi, wr_lo], axis=0))

    pos, gate = _route(logits)
    yet = _ffn(pos, h2, w_exp_gate[layer], w_exp_up[layer], w_exp_down[layer])
    return _combine(pos, gate, yet, x1, g2, final_norm_g[None, :])
```

```python
import jax
import jax.numpy as jnp
from jax import lax
from jax.experimental import pallas as pl
from jax.experimental.pallas import tpu as pltpu

D_MODEL = 1024
BATCH = 16
SEQ = 2048
CTX_LEN = 256
GRID_W = 64
N_HEADS = 16
N_KV_HEADS = 4
GROUP = N_HEADS // N_KV_HEADS
HEAD_DIM = D_MODEL // N_HEADS
ATTN_WIDTH = N_HEADS * HEAD_DIM
KV_WIDTH = N_KV_HEADS * HEAD_DIM
WINDOW = 128
Q_BLOCK = 128
ROPE_BASE = 10000.0
CONV_WIDTH = D_MODEL
N_EXPERTS = 16
EXPERT_FF = D_MODEL
CAPACITY_FACTOR = 2
N_MOD = 6
EPS = 1e-6
NEG_INF = -1e30
LOG2E = 1.4426950408889634

CAP = CAPACITY_FACTOR * SEQ // N_EXPERTS
N_QBLK = SEQ // Q_BLOCK
LANES = 128
F32_ROWS = 8
BF16_ROWS = 16
W3_WIDTH = 3 * D_MODEL + LANES
TOK_TILE = 512
PROJ_SUB = 256
ATTN_TILE = 512
MERGE_TILE = 512
SUB_TILE = 256
ROUTE_B = 4
GATHER_LO_END = 5 * SEQ // 8
GATHER_HI_START = 3 * SEQ // 8
N_SCAT = 4
SCAT_TOK = SEQ // N_SCAT
MOD_ROWS = 32
VMEM_LIMIT = 56 * 1024 * 1024

OFF_Q = 0
OFF_K = OFF_Q + ATTN_WIDTH
OFF_V = OFF_K + KV_WIDTH
OFF_U = OFF_V + KV_WIDTH
OFF_B = OFF_U + CONV_WIDTH
OFF_C = OFF_B + CONV_WIDTH
OFF_GA = OFF_C + CONV_WIDTH
OFF_GC = OFF_GA + D_MODEL
IN_WIDTH = OFF_GC + D_MODEL

F32 = jnp.float32
BF16 = jnp.bfloat16


def _params(*sem):
    return pltpu.CompilerParams(dimension_semantics=sem, vmem_limit_bytes=VMEM_LIMIT)


def _dot(a, b):
    return jnp.dot(a, b, preferred_element_type=F32)


def _dot_nt(a, b):
    return lax.dot_general(a, b, (((1,), (1,)), ((), ())), preferred_element_type=F32)


def _sigmoid(x):
    return 1.0 / (1.0 + jnp.exp(-x))


def _split_bf16(x):
    hi = x.astype(BF16)
    lo = (x - hi.astype(F32)).astype(BF16)
    return hi, lo


def _norm_mod(x, g, shift, scale):
    y = x * lax.rsqrt(jnp.mean(x * x, axis=-1, keepdims=True) + EPS) * g
    return y * (1.0 + scale) + shift


def _mod_kernel(c_ref, w_ref, b_ref, o_ref):
    cv = c_ref[...]
    s_hi, s_lo = _split_bf16(cv * _sigmoid(cv))
    w_hi, w_lo = _split_bf16(w_ref[...])
    o_ref[...] = _dot(s_hi, w_hi) + _dot(s_lo, w_hi) + _dot(s_hi, w_lo) + b_ref[...]


def _mod(cvec, w_mod, b_mod):
    n_out = N_MOD * D_MODEL
    blk = D_MODEL
    return pl.pallas_call(
        _mod_kernel,
        grid=(n_out // blk,),
        in_specs=[
            pl.BlockSpec((MOD_ROWS, D_MODEL), lambda j: (0, 0)),
            pl.BlockSpec((D_MODEL, blk), lambda j: (0, j)),
            pl.BlockSpec((1, blk), lambda j: (0, j)),
        ],
        out_specs=pl.BlockSpec((MOD_ROWS, blk), lambda j: (0, j)),
        out_shape=jax.ShapeDtypeStruct((MOD_ROWS, n_out), F32),
        compiler_params=_params("arbitrary"),
        name="mod",
    )(cvec, w_mod, b_mod)


def _ctx_kernel(ctx_ref, g_ref, sh_ref, sc_ref, w_ref, k_ref, vt_ref):
    h = _norm_mod(ctx_ref[0], g_ref[...], sh_ref[...], sc_ref[...]).astype(BF16)
    kv = _dot(h, w_ref[...])
    k_ref[0] = kv[:, :KV_WIDTH].astype(BF16)
    vt_ref[0] = kv[:, KV_WIDTH:].T.astype(BF16)


def _ctx_kv(ctx, g1, csh1, csc1, w_kv):
    row = pl.BlockSpec((1, D_MODEL), lambda b: (0, 0))
    return pl.pallas_call(
        _ctx_kernel,
        grid=(BATCH,),
        in_specs=[
            pl.BlockSpec((1, CTX_LEN, D_MODEL), lambda b: (b, 0, 0)),
            row, row, row,
            pl.BlockSpec((D_MODEL, 2 * KV_WIDTH), lambda b: (0, 0)),
        ],
        out_specs=[pl.BlockSpec((1, CTX_LEN, KV_WIDTH), lambda b: (b, 0, 0)),
                   pl.BlockSpec((1, KV_WIDTH, CTX_LEN), lambda b: (b, 0, 0))],
        out_shape=[jax.ShapeDtypeStruct((BATCH, CTX_LEN, KV_WIDTH), BF16),
                   jax.ShapeDtypeStruct((BATCH, KV_WIDTH, CTX_LEN), BF16)],
        compiler_params=_params("arbitrary"),
        name="ctx_kv",
    )(ctx, g1, csh1, csc1, w_kv)


def _proj_kernel(x_ref, g_ref, sh_ref, sc_ref, w_ref, cos_ref, sin_ref,
                 q_ref, k_ref, vt_ref, z_ref, bg_ref, ga_ref, gc_ref):
    lane = lax.broadcasted_iota(jnp.int32, (PROJ_SUB, LANES), 1)
    first_half = (lane & (HEAD_DIM // 4)) == 0
    scale = HEAD_DIM ** -0.5 * LOG2E

    for r in range(TOK_TILE // PROJ_SUB):
        rs = slice(r * PROJ_SUB, (r + 1) * PROJ_SUB)
        h = _norm_mod(x_ref[0, rs, :], g_ref[...], sh_ref[0], sc_ref[0]).astype(BF16)
        cos = cos_ref[rs, :]
        sin = sin_ref[rs, :]

        def rope(a):
            rot = jnp.where(first_half,
                            pltpu.roll(a, LANES - HEAD_DIM // 4, 1),
                            pltpu.roll(a, HEAD_DIM // 4, 1))
            return a * cos + rot * sin

        def proj(off, width):
            return _dot(h, w_ref[:, off:off + width])

        qa = proj(OFF_Q, ATTN_WIDTH)
        for j in range(ATTN_WIDTH // LANES):
            sl = slice(j * LANES, (j + 1) * LANES)
            q_ref[0, rs, sl] = (rope(qa[:, sl]) * scale).astype(BF16)
        ka = proj(OFF_K, KV_WIDTH)
        for j in range(KV_WIDTH // LANES):
            sl = slice(j * LANES, (j + 1) * LANES)
            k_ref[0, rs, sl] = rope(ka[:, sl]).astype(BF16)
        vt_ref[0, :, rs] = proj(OFF_V, KV_WIDTH).T.astype(BF16)
        z_ref[0, rs, :] = (proj(OFF_C, CONV_WIDTH) * proj(OFF_U, CONV_WIDTH)).astype(BF16)
        bg_ref[0, rs, :] = proj(OFF_B, CONV_WIDTH).astype(BF16)
        ga_ref[0, rs, :] = _sigmoid(proj(OFF_GA, D_MODEL)).astype(BF16)
        gc_ref[0, rs, :] = _sigmoid(proj(OFF_GC, D_MODEL)).astype(BF16)


def _proj(x, g1, sh1, sc1, w_in, cos, sin):
    nt = SEQ // TOK_TILE
    tile = lambda w: pl.BlockSpec((1, TOK_TILE, w), lambda b, t: (b, t, 0))
    per_b = pl.BlockSpec((1, 1, D_MODEL), lambda b, t: (b, 0, 0))
    tab = pl.BlockSpec((TOK_TILE, LANES), lambda b, t: (t, 0))
    shp = lambda w: jax.ShapeDtypeStruct((BATCH, SEQ, w), BF16)
    vt_spec = pl.BlockSpec((1, KV_WIDTH, TOK_TILE), lambda b, t: (b, 0, t))
    vt_shape = jax.ShapeDtypeStruct((BATCH, KV_WIDTH, SEQ), BF16)
    return pl.pallas_call(
        _proj_kernel,
        grid=(BATCH, nt),
        in_specs=[
            tile(D_MODEL),
            pl.BlockSpec((1, D_MODEL), lambda b, t: (0, 0)),
            per_b, per_b,
            pl.BlockSpec((D_MODEL, IN_WIDTH), lambda b, t: (0, 0), pipeline_mode=pl.Buffered(1)),
            tab, tab,
        ],
        out_specs=[tile(ATTN_WIDTH), tile(KV_WIDTH), vt_spec, tile(CONV_WIDTH), tile(CONV_WIDTH),
                   tile(D_MODEL), tile(D_MODEL)],
        out_shape=[shp(ATTN_WIDTH), shp(KV_WIDTH), vt_shape, shp(CONV_WIDTH), shp(CONV_WIDTH),
                   shp(D_MODEL), shp(D_MODEL)],
        compiler_params=_params("arbitrary", "arbitrary"),
        name="proj",
    )(x, g1, sh1, sc1, w_in, cos, sin)


def _attn_kernel(sink_ref, q_ref, k_ref, vt_ref, kx_ref, vx_ref, o_ref):
    cols = GROUP * Q_BLOCK
    key = lax.broadcasted_iota(jnp.int32, (Q_BLOCK, cols), 0)
    qry = lax.broadcasted_iota(jnp.int32, (Q_BLOCK, cols), 1) & (Q_BLOCK - 1)
    lane_head = lax.broadcasted_iota(jnp.int32, (1, cols), 1) >> 7
    n_keys = Q_BLOCK + 2 * WINDOW + CTX_LEN
    ones = jnp.ones((BF16_ROWS, n_keys), BF16)

    blocks = []
    for sb in range(ATTN_TILE // Q_BLOCK):
        i = pl.program_id(1) * (ATTN_TILE // Q_BLOCK) + sb
        blocks.append(dict(
            rows=slice(sb * Q_BLOCK, (sb + 1) * Q_BLOCK),
            p0=pl.multiple_of(jnp.maximum(i - 1, 0) * Q_BLOCK, Q_BLOCK),
            c0=pl.multiple_of(i * Q_BLOCK, Q_BLOCK),
            n0=pl.multiple_of(jnp.minimum(i + 1, N_QBLK - 1) * Q_BLOCK, Q_BLOCK),
            ok_prev=key >= qry + jnp.where(i > 0, 0, Q_BLOCK),
            ok_next=key <= qry - jnp.where(i < N_QBLK - 1, 0, Q_BLOCK)))

    def scores(blk, g):
        ks = slice(g * HEAD_DIM, (g + 1) * HEAD_DIM)
        q4 = jnp.concatenate(
            [q_ref[0, blk["rows"], (g * GROUP + j) * HEAD_DIM:(g * GROUP + j + 1) * HEAD_DIM]
             for j in range(GROUP)], axis=0)
        k_all = jnp.concatenate(
            [k_ref[0, pl.ds(blk["p0"], Q_BLOCK), ks], k_ref[0, pl.ds(blk["c0"], Q_BLOCK), ks],
             k_ref[0, pl.ds(blk["n0"], Q_BLOCK), ks], kx_ref[0, :, ks]], axis=0)
        return _dot_nt(k_all, q4)

    def softmax(blk, g, st):
        pieces = [jnp.where(blk["ok_prev"], st[:WINDOW], NEG_INF),
                  st[WINDOW:WINDOW + Q_BLOCK],
                  jnp.where(blk["ok_next"], st[WINDOW + Q_BLOCK:2 * WINDOW + Q_BLOCK], NEG_INF),
                  st[2 * WINDOW + Q_BLOCK:]]
        sink = jnp.zeros((1, cols), F32)
        for j in range(GROUP):
            sink = jnp.where(lane_head == j, sink_ref[g * GROUP + j], sink)
        m = sink
        for piece in pieces:
            m = jnp.maximum(m, jnp.max(piece, axis=0, keepdims=True))
        pt = jnp.concatenate([jnp.exp2((piece - m).astype(BF16)) for piece in pieces], axis=0)
        return pt, jnp.exp2(sink - m)

    def values(blk, g, pt, p_sink):
        ks = slice(g * HEAD_DIM, (g + 1) * HEAD_DIM)
        vt = jnp.concatenate(
            [vt_ref[0, ks, pl.ds(blk["p0"], Q_BLOCK)], vt_ref[0, ks, pl.ds(blk["c0"], Q_BLOCK)],
             vt_ref[0, ks, pl.ds(blk["n0"], Q_BLOCK)], vx_ref[0, ks, :]], axis=1)
        ot = _dot(jnp.concatenate([vt, ones], axis=0), pt)
        denom = ot[HEAD_DIM:HEAD_DIM + 1] + p_sink
        on = ot[:HEAD_DIM] * (1.0 / denom)
        for jj in range(GROUP // 2):
            pair_t = jnp.concatenate(
                [on[:, (2 * jj) * Q_BLOCK:(2 * jj + 1) * Q_BLOCK],
                 on[:, (2 * jj + 1) * Q_BLOCK:(2 * jj + 2) * Q_BLOCK]], axis=0)
            col = (g * GROUP + 2 * jj) * HEAD_DIM
            o_ref[0, blk["rows"], col:col + 2 * HEAD_DIM] = pair_t.T.astype(BF16)

    chains = [(blk, g) for blk in blocks for g in range(N_KV_HEADS)]
    st_next = scores(*chains[0])
    for n, chain in enumerate(chains):
        st = st_next
        if n + 1 < len(chains):
            st_next = scores(*chains[n + 1])
        values(*chain, *softmax(*chain, st))


def _attn(sink, q, k, vt, k_ctx, vt_ctx):
    qspec = pl.BlockSpec((1, ATTN_TILE, ATTN_WIDTH), lambda b, i: (b, i, 0))
    return pl.pallas_call(
        _attn_kernel,
        grid=(BATCH, SEQ // ATTN_TILE),
        in_specs=[pl.BlockSpec(memory_space=pltpu.SMEM), qspec,
                  pl.BlockSpec((1, SEQ, KV_WIDTH), lambda b, i: (b, 0, 0)),
                  pl.BlockSpec((1, KV_WIDTH, SEQ), lambda b, i: (b, 0, 0)),
                  pl.BlockSpec((1, CTX_LEN, KV_WIDTH), lambda b, i: (b, 0, 0)),
                  pl.BlockSpec((1, KV_WIDTH, CTX_LEN), lambda b, i: (b, 0, 0))],
        out_specs=qspec,
        out_shape=jax.ShapeDtypeStruct((BATCH, SEQ, ATTN_WIDTH), BF16),
        compiler_params=_params("arbitrary", "arbitrary"),
        name="attn",
    )(sink, q, k, vt, k_ctx, vt_ctx)


def _merge_kernel(x_ref, oa_ref, z_ref, zp_ref, zn_ref, bg_ref, ga_ref, gc_ref, cw_ref,
                  w3_ref, g1_ref, sh2_ref, sc2_ref, n2_ref, wr_ref,
                  x1_ref, h2_ref, lg_ref):
    t = pl.program_id(1)
    nt = pl.num_programs(1)
    z_before = zp_ref[0, BF16_ROWS - 1:BF16_ROWS, :].astype(F32) * jnp.where(t > 0, 1.0, 0.0)
    z_after = zn_ref[0, 0:1, :].astype(F32) * jnp.where(t < nt - 1, 1.0, 0.0)
    row = lax.broadcasted_iota(jnp.int32, (F32_ROWS, 1), 0)
    n_sub = MERGE_TILE // SUB_TILE
    subs = [slice(r * SUB_TILE, (r + 1) * SUB_TILE) for r in range(n_sub)]

    def conv(r):
        lo, hi = subs[r].start, subs[r].stop
        z = z_ref[0, subs[r], :].astype(F32)
        before = z_before if r == 0 else z_ref[0, lo - 1:lo, :].astype(F32)
        after = z_after if r == n_sub - 1 else z_ref[0, hi:hi + 1, :].astype(F32)
        up = pltpu.roll(z, 1, 0)
        z_prev = jnp.concatenate(
            [jnp.where(row == 0, before, up[:F32_ROWS]), up[F32_ROWS:]], axis=0)
        dn = pltpu.roll(z, SUB_TILE - 1, 0)
        z_next = jnp.concatenate(
            [dn[:-F32_ROWS], jnp.where(row == F32_ROWS - 1, after, dn[-F32_ROWS:])], axis=0)
        y = bg_ref[0, subs[r], :].astype(F32) * (
            cw_ref[0:1, :] * z_prev + cw_ref[1:2, :] * z + cw_ref[2:3, :] * z_next)
        return y.astype(BF16)

    a, s = [], []
    for r in range(n_sub):
        a.append(_dot(oa_ref[0, subs[r], :], w3_ref[:, 0:D_MODEL]))
        s.append(_dot(conv(r), w3_ref[:, D_MODEL:2 * D_MODEL]))
    branch = []
    for r, rs in enumerate(subs):
        merged = ga_ref[0, rs, :].astype(F32) * a[r] + gc_ref[0, rs, :].astype(F32) * s[r]
        branch.append(_dot(merged.astype(BF16), w3_ref[:, 2 * D_MODEL:3 * D_MODEL]))
    for r, rs in enumerate(subs):
        x1 = x_ref[0, rs, :] + g1_ref[0] * branch[r]
        x1_ref[0, rs, :] = x1
        h2 = _norm_mod(x1, n2_ref[...], sh2_ref[0], sc2_ref[0])
        h_hi, h_lo = _split_bf16(h2)
        h2_ref[0, rs, :] = h_hi
        both = _dot_nt(wr_ref[...], h_hi)
        lg_ref[0, :, rs] = (both[:N_EXPERTS] + both[N_EXPERTS:]
                            + _dot_nt(wr_ref[:N_EXPERTS], h_lo))


def _merge(x, oa, z, bg, ga, gc, conv_w, w3, g1, sh2, sc2, n2, wr):
    nt = SEQ // MERGE_TILE
    hb = MERGE_TILE // BF16_ROWS
    n_hb = SEQ // BF16_ROWS
    tile = pl.BlockSpec((1, MERGE_TILE, D_MODEL), lambda b, t: (b, t, 0))
    halo_p = pl.BlockSpec((1, BF16_ROWS, D_MODEL), lambda b, t: (b, jnp.maximum(t * hb - 1, 0), 0))
    halo_n = pl.BlockSpec((1, BF16_ROWS, D_MODEL),
                          lambda b, t: (b, jnp.minimum((t + 1) * hb, n_hb - 1), 0))
    per_b = pl.BlockSpec((1, 1, D_MODEL), lambda b, t: (b, 0, 0))
    full = lambda r, c: pl.BlockSpec((r, c), lambda b, t: (0, 0))
    return pl.pallas_call(
        _merge_kernel,
        grid=(BATCH, nt),
        in_specs=[tile, tile, tile, halo_p, halo_n, tile, tile, tile,
                  full(3, D_MODEL), full(D_MODEL, W3_WIDTH),
                  per_b, per_b, per_b, full(1, D_MODEL), full(2 * N_EXPERTS, D_MODEL)],
        out_specs=[tile, tile, pl.BlockSpec((1, N_EXPERTS, MERGE_TILE), lambda b, t: (b, 0, t))],
        out_shape=[jax.ShapeDtypeStruct((BATCH, SEQ, D_MODEL), F32),
                   jax.ShapeDtypeStruct((BATCH, SEQ, D_MODEL), BF16),
                   jax.ShapeDtypeStruct((BATCH, N_EXPERTS, SEQ), F32)],
        compiler_params=_params("arbitrary", "arbitrary"),
        name="merge",
    )(x, oa, z, z, z, bg, ga, gc, conv_w, w3, g1, sh2, sc2, n2, wr)


def _cumsum_excl(mf, upper):
    nblk = SEQ // LANES
    rows = mf.shape[0]
    stacked = jnp.concatenate([mf[:, j * LANES:(j + 1) * LANES] for j in range(nblk)], axis=0)
    within = _dot(stacked.astype(BF16), upper)
    tot = jnp.sum(stacked, axis=1, keepdims=True)
    off = jnp.zeros((rows, 1), F32)
    out = []
    for j in range(nblk):
        rs = slice(j * rows, (j + 1) * rows)
        out.append(within[rs] + off)
        off = off + tot[rs]
    return jnp.concatenate(out, axis=1)


def _route_kernel(lg_ref, pos_ref, gate_ref, band_ref):
    rows = ROUTE_B * N_EXPERTS
    lg = lg_ref[...]
    ex = jnp.exp(lg - jnp.max(lg, axis=1, keepdims=True))
    aff = (ex / jnp.sum(ex, axis=1, keepdims=True)).reshape(rows, SEQ)

    def body(_, carry):
        lo, hi = carry
        mid = lo + ((hi - lo) >> 1)
        cnt = jnp.sum(jnp.where(aff >= pltpu.bitcast(mid, F32), 1.0, 0.0), axis=1, keepdims=True)
        ge = cnt >= CAP
        return jnp.where(ge, mid, lo), jnp.where(ge, hi, mid)

    one_bits = 0x3F800000
    lo0 = jnp.zeros((rows, 1), jnp.int32)
    hi0 = jnp.full((rows, 1), one_bits + 1, jnp.int32)
    lo, _ = lax.fori_loop(0, 31, body, (lo0, hi0))
    th = pltpu.bitcast(lo, F32)

    ku = lax.broadcasted_iota(jnp.int32, (LANES, LANES), 0)
    nu = lax.broadcasted_iota(jnp.int32, (LANES, LANES), 1)
    upper = jnp.where(ku < nu, 1.0, 0.0).astype(BF16)
    gtf = jnp.where(aff > th, 1.0, 0.0)
    eqf = jnp.where(aff == th, 1.0, 0.0)
    need = CAP - jnp.sum(gtf, axis=1, keepdims=True)
    sel = gtf + eqf * jnp.where(_cumsum_excl(eqf, upper) < need, 1.0, 0.0)
    pos = _cumsum_excl(sel, upper)
    pos_ref[...] = jnp.where(sel > 0.0, pos, -1.0).reshape(ROUTE_B, N_EXPERTS, SEQ)
    gate_ref[...] = jnp.where(sel > 0.0, aff, 0.0).reshape(ROUTE_B, N_EXPERTS, SEQ)
    count = pos + sel
    tok_last_lo = jnp.sum(jnp.where(count <= CAP // 2 - 1, 1.0, 0.0), axis=1, keepdims=True)
    tok_first_hi = jnp.sum(jnp.where(count <= CAP // 2, 1.0, 0.0), axis=1, keepdims=True)
    banded = jnp.logical_and(tok_last_lo < GATHER_LO_END, tok_first_hi >= GATHER_HI_START)
    band_ref[...] = jnp.where(banded, 1, 0).astype(jnp.int32).reshape(ROUTE_B, N_EXPERTS, 1)


def _route(logits):
    spec = pl.BlockSpec((ROUTE_B, N_EXPERTS, SEQ), lambda b: (b, 0, 0))
    shape = jax.ShapeDtypeStruct((BATCH, N_EXPERTS, SEQ), F32)
    return pl.pallas_call(
        _route_kernel,
        grid=(BATCH // ROUTE_B,),
        in_specs=[spec],
        out_specs=[spec, spec, pl.BlockSpec((ROUTE_B, N_EXPERTS, 1), lambda b: (b, 0, 0))],
        out_shape=[shape, shape, jax.ShapeDtypeStruct((BATCH, N_EXPERTS, 1), jnp.int32)],
        compiler_params=_params("arbitrary"),
        name="route",
    )(logits)


def _ffn_kernel(band_ref, pos_ref, h_ref, wg_ref, wu_ref, wd_ref, yet_ref,
                wg_b, wu_b, wdt_b, xe_ref):
    e, b = pl.program_id(0), pl.program_id(1)

    @pl.when(b == 0)
    def _():
        wg_b[...] = wg_ref[0].astype(BF16)
        wu_b[...] = wu_ref[0].astype(BF16)
        wdt_b[...] = wd_ref[0].T.astype(BF16)

    def gather(slots, toks):
        n_slots, n_toks = slots.stop - slots.start, toks.stop - toks.start
        slot = (lax.broadcasted_iota(jnp.int32, (n_slots, n_toks), 0) + slots.start).astype(F32)
        onehot = jnp.where(slot == pos_ref[0, 0, :, toks], 1.0, 0.0).astype(BF16)
        xe_ref[slots, :] = _dot(onehot, h_ref[0, toks, :]).astype(BF16)

    banded = band_ref[b * N_EXPERTS + e] == 1

    @pl.when(banded)
    def _():
        gather(slice(0, CAP // 2), slice(0, GATHER_LO_END))
        gather(slice(CAP // 2, CAP), slice(GATHER_HI_START, SEQ))

    @pl.when(jnp.logical_not(banded))
    def _():
        gather(slice(0, CAP), slice(0, SEQ))

    xe = xe_ref[...]
    act = _dot(xe, wg_b[...])
    up = _dot(xe, wu_b[...])
    hid = (act * _sigmoid(act) * up).astype(BF16)
    yet_ref[0] = _dot_nt(wdt_b[...], hid).astype(BF16)


def _ffn(band, pos, h2, wg, wu, wd):
    def wspec(lead):
        return pl.BlockSpec(
            (1, D_MODEL, EXPERT_FF),
            lambda e, b: (jnp.minimum(e + jnp.where(b >= BATCH - lead, 1, 0), N_EXPERTS - 1),
                          0, 0))
    wscr = pltpu.VMEM((D_MODEL, EXPERT_FF), BF16)
    return pl.pallas_call(
        _ffn_kernel,
        grid=(N_EXPERTS, BATCH),
        in_specs=[pl.BlockSpec(memory_space=pltpu.SMEM),
                  pl.BlockSpec((1, 1, 1, SEQ), lambda e, b: (b, e, 0, 0)),
                  pl.BlockSpec((1, SEQ, D_MODEL), lambda e, b: (b, 0, 0)),
                  wspec(3), wspec(2), wspec(1)],
        out_specs=pl.BlockSpec((1, D_MODEL, CAP), lambda e, b: (b, 0, e)),
        out_shape=jax.ShapeDtypeStruct((BATCH, D_MODEL, N_EXPERTS * CAP), BF16),
        scratch_shapes=[wscr, wscr, wscr, pltpu.VMEM((CAP, D_MODEL), BF16)],
        compiler_params=_params("arbitrary", "arbitrary"),
        name="ffn",
    )(band.reshape(-1), pos[:, :, None, :], h2, wg, wu, wd)


def _combine_kernel(pos_ref, gate_ref, yet_ref, x1_ref, g2_ref, fg_ref, o_ref):
    slot = lax.broadcasted_iota(jnp.int32, (CAP, SCAT_TOK), 0).astype(F32)
    scat = jnp.concatenate(
        [jnp.where(slot == pos_ref[0, ee:ee + 1, :], gate_ref[0, ee:ee + 1, :], 0.0)
         .astype(BF16) for ee in range(N_EXPERTS)], axis=0)
    moe = _dot(yet_ref[0], scat).T
    x2 = x1_ref[0] + g2_ref[0] * moe
    o_ref[0] = x2 * lax.rsqrt(jnp.mean(x2 * x2, axis=-1, keepdims=True) + EPS) * fg_ref[...]


def _combine(pos, gate, yet, x1, g2, fg):
    sel_c = pl.BlockSpec((1, N_EXPERTS, SCAT_TOK), lambda b, c: (b, 0, c))
    tok_c = pl.BlockSpec((1, SCAT_TOK, D_MODEL), lambda b, c: (b, c, 0))
    return pl.pallas_call(
        _combine_kernel,
        grid=(BATCH, N_SCAT),
        in_specs=[sel_c, sel_c,
                  pl.BlockSpec((1, D_MODEL, N_EXPERTS * CAP), lambda b, c: (b, 0, 0)),
                  tok_c,
                  pl.BlockSpec((1, 1, D_MODEL), lambda b, c: (b, 0, 0)),
                  pl.BlockSpec((1, D_MODEL), lambda b, c: (0, 0))],
        out_specs=tok_c,
        out_shape=jax.ShapeDtypeStruct((BATCH, SEQ, D_MODEL), F32),
        compiler_params=_params("arbitrary", "arbitrary"),
        name="combine",
    )(pos, gate, yet, x1, g2, fg)


def _rope_tables():
    rows = SEQ // GRID_W
    row = jnp.repeat(jnp.arange(rows, dtype=F32), GRID_W)
    col = jnp.tile(jnp.arange(GRID_W, dtype=F32), rows)
    n_freq = HEAD_DIM // 4
    inv_freq = ROPE_BASE ** (-jnp.arange(n_freq, dtype=F32) / n_freq)
    ang_r = row[:, None] * inv_freq[None, :]
    ang_c = col[:, None] * inv_freq[None, :]
    ang = jnp.concatenate([ang_r, ang_r, ang_c, ang_c], axis=-1)
    sign = jnp.tile(jnp.concatenate([-jnp.ones(n_freq, F32), jnp.ones(n_freq, F32)]), 2)
    reps = LANES // HEAD_DIM
    return jnp.tile(jnp.cos(ang), (1, reps)), jnp.tile(jnp.sin(ang) * sign, (1, reps))


def kernel(x, c, ctx, c_ctx, w_mod, b_mod, norm1_g, w_in, attn_sink, conv_w, w_proj_attn,
           w_proj_conv, w_out, norm2_g, w_router, w_exp_gate, w_exp_up, w_exp_down, final_norm_g):
    assert x.shape == (BATCH, SEQ, D_MODEL) and ctx.shape == (BATCH, CTX_LEN, D_MODEL)
    assert w_mod.shape[0] == 1, "single-layer problem"
    layer = 0

    cvec = jnp.concatenate(
        [c, c_ctx[None, :], jnp.zeros((MOD_ROWS - BATCH - 1, D_MODEL), F32)], axis=0)
    mod = _mod(cvec, w_mod[layer], b_mod[layer][None, :])
    chunk = lambda rows, k: rows[:, k * D_MODEL:(k + 1) * D_MODEL]
    mod_x = mod[:BATCH][:, None, :]
    sh1, sc1, g1, sh2, sc2, g2 = (mod_x[..., k * D_MODEL:(k + 1) * D_MODEL] for k in range(N_MOD))
    mod_c = mod[BATCH:BATCH + 1]
    csh1, csc1 = chunk(mod_c, 0), chunk(mod_c, 1)

    n1 = norm1_g[layer][None, :]
    w_in_b = w_in[layer].astype(BF16)
    k_ctx, vt_ctx = _ctx_kv(ctx, n1, csh1, csc1, w_in_b[:, OFF_K:OFF_U])

    cos, sin = _rope_tables()
    q, k, vt, z, bg, ga, gc = _proj(x, n1, sh1, sc1, w_in_b, cos, sin)
    sink_log2 = attn_sink[layer].reshape(-1) * LOG2E
    o_attn = _attn(sink_log2, q, k, vt, k_ctx, vt_ctx)

    wr_hi, wr_lo = _split_bf16(w_router[layer].T)
    w3 = jnp.concatenate(
        [w_proj_attn[layer], w_proj_conv[layer], w_out[layer],
         jnp.zeros((D_MODEL, W3_WIDTH - 3 * D_MODEL), F32)], axis=1).astype(BF16)
    x1, h2, logits = _merge(
        x, o_attn, z, bg, ga, gc, conv_w[layer], w3,
        g1, sh2, sc2, norm2_g[layer][None, :], jnp.concatenate([wr_hi, wr_lo], axis=0))

    pos, gate, band = _route(logits)
    yet = _ffn(band, pos, h2, w_exp_gate[layer], w_exp_up[layer], w_exp_down[layer])
    return _combine(pos, gate, yet, x1, g2, final_norm_g[None, :])
```

```python
import jax
import jax.numpy as jnp
from jax import lax
from jax.experimental import pallas as pl
from jax.experimental.pallas import tpu as pltpu

D_MODEL = 1024
BATCH = 16
SEQ = 2048
CTX_LEN = 256
GRID_W = 64
N_HEADS = 16
N_KV_HEADS = 4
GROUP = N_HEADS // N_KV_HEADS
HEAD_DIM = D_MODEL // N_HEADS
ATTN_WIDTH = N_HEADS * HEAD_DIM
KV_WIDTH = N_KV_HEADS * HEAD_DIM
WINDOW = 128
Q_BLOCK = 128
ROPE_BASE = 10000.0
CONV_WIDTH = D_MODEL
N_EXPERTS = 16
EXPERT_FF = D_MODEL
CAPACITY_FACTOR = 2
N_MOD = 6
EPS = 1e-6
NEG_INF = -1e30
LOG2E = 1.4426950408889634

CAP = CAPACITY_FACTOR * SEQ // N_EXPERTS
N_QBLK = SEQ // Q_BLOCK
LANES = 128
F32_ROWS = 8
BF16_ROWS = 16
W3_WIDTH = 3 * D_MODEL + LANES
TOK_TILE = 512
PROJ_SUB = 256
ATTN_TILE = 512
MERGE_TILE = 512
SUB_TILE = 256
ROUTE_B = 4
GATHER_LO_END = 5 * SEQ // 8
GATHER_HI_START = 3 * SEQ // 8
N_SCAT = 8
SCAT_TOK = SEQ // N_SCAT
MOD_ROWS = 32
VMEM_LIMIT = 56 * 1024 * 1024

OFF_Q = 0
OFF_K = OFF_Q + ATTN_WIDTH
OFF_V = OFF_K + KV_WIDTH
OFF_U = OFF_V + KV_WIDTH
OFF_B = OFF_U + CONV_WIDTH
OFF_C = OFF_B + CONV_WIDTH
OFF_GA = OFF_C + CONV_WIDTH
OFF_GC = OFF_GA + D_MODEL
IN_WIDTH = OFF_GC + D_MODEL

F32 = jnp.float32
BF16 = jnp.bfloat16


def _params(*sem):
    return pltpu.CompilerParams(dimension_semantics=sem, vmem_limit_bytes=VMEM_LIMIT)


def _dot(a, b):
    return jnp.dot(a, b, preferred_element_type=F32)


def _dot_nt(a, b):
    return lax.dot_general(a, b, (((1,), (1,)), ((), ())), preferred_element_type=F32)


def _sigmoid(x):
    return 1.0 / (1.0 + jnp.exp(-x))


def _split_bf16(x):
    hi = x.astype(BF16)
    lo = (x - hi.astype(F32)).astype(BF16)
    return hi, lo


def _norm_mod(x, g, shift, scale):
    y = x * lax.rsqrt(jnp.mean(x * x, axis=-1, keepdims=True) + EPS) * g
    return y * (1.0 + scale) + shift


def _mod_kernel(c_ref, w_ref, b_ref, o_ref):
    cv = c_ref[...]
    s_hi, s_lo = _split_bf16(cv * _sigmoid(cv))
    w_hi, w_lo = _split_bf16(w_ref[...])
    o_ref[...] = _dot(s_hi, w_hi) + _dot(s_lo, w_hi) + _dot(s_hi, w_lo) + b_ref[...]


def _mod(cvec, w_mod, b_mod):
    n_out = N_MOD * D_MODEL
    blk = D_MODEL
    return pl.pallas_call(
        _mod_kernel,
        grid=(n_out // blk,),
        in_specs=[
            pl.BlockSpec((MOD_ROWS, D_MODEL), lambda j: (0, 0)),
            pl.BlockSpec((D_MODEL, blk), lambda j: (0, j)),
            pl.BlockSpec((1, blk), lambda j: (0, j)),
        ],
        out_specs=pl.BlockSpec((MOD_ROWS, blk), lambda j: (0, j)),
        out_shape=jax.ShapeDtypeStruct((MOD_ROWS, n_out), F32),
        compiler_params=_params("arbitrary"),
        name="mod",
    )(cvec, w_mod, b_mod)


def _ctx_kernel(ctx_ref, g_ref, sh_ref, sc_ref, w_ref, k_ref, vt_ref):
    h = _norm_mod(ctx_ref[0], g_ref[...], sh_ref[...], sc_ref[...]).astype(BF16)
    kv = _dot(h, w_ref[...])
    k_ref[0] = kv[:, :KV_WIDTH].astype(BF16)
    vt_ref[0] = kv[:, KV_WIDTH:].T.astype(BF16)


def _ctx_kv(ctx, g1, csh1, csc1, w_kv):
    row = pl.BlockSpec((1, D_MODEL), lambda b: (0, 0))
    return pl.pallas_call(
        _ctx_kernel,
        grid=(BATCH,),
        in_specs=[
            pl.BlockSpec((1, CTX_LEN, D_MODEL), lambda b: (b, 0, 0)),
            row, row, row,
            pl.BlockSpec((D_MODEL, 2 * KV_WIDTH), lambda b: (0, 0)),
        ],
        out_specs=[pl.BlockSpec((1, CTX_LEN, KV_WIDTH), lambda b: (b, 0, 0)),
                   pl.BlockSpec((1, KV_WIDTH, CTX_LEN), lambda b: (b, 0, 0))],
        out_shape=[jax.ShapeDtypeStruct((BATCH, CTX_LEN, KV_WIDTH), BF16),
                   jax.ShapeDtypeStruct((BATCH, KV_WIDTH, CTX_LEN), BF16)],
        compiler_params=_params("arbitrary"),
        name="ctx_kv",
    )(ctx, g1, csh1, csc1, w_kv)


def _proj_kernel(x_ref, g_ref, sh_ref, sc_ref, w_ref, cos_ref, sin_ref,
                 q_ref, k_ref, vt_ref, z_ref, bg_ref, ga_ref, gc_ref):
    lane = lax.broadcasted_iota(jnp.int32, (PROJ_SUB, LANES), 1)
    first_half = (lane & (HEAD_DIM // 4)) == 0
    scale = HEAD_DIM ** -0.5 * LOG2E

    for r in range(TOK_TILE // PROJ_SUB):
        rs = slice(r * PROJ_SUB, (r + 1) * PROJ_SUB)
        h = _norm_mod(x_ref[0, rs, :], g_ref[...], sh_ref[0], sc_ref[0]).astype(BF16)
        cos = cos_ref[rs, :]
        sin = sin_ref[rs, :]

        def rope(a):
            rot = jnp.where(first_half,
                            pltpu.roll(a, LANES - HEAD_DIM // 4, 1),
                            pltpu.roll(a, HEAD_DIM // 4, 1))
            return a * cos + rot * sin

        def proj(off, width):
            return _dot(h, w_ref[:, off:off + width])

        qa = proj(OFF_Q, ATTN_WIDTH)
        for j in range(ATTN_WIDTH // LANES):
            sl = slice(j * LANES, (j + 1) * LANES)
            q_ref[0, rs, sl] = (rope(qa[:, sl]) * scale).astype(BF16)
        ka = proj(OFF_K, KV_WIDTH)
        for j in range(KV_WIDTH // LANES):
            sl = slice(j * LANES, (j + 1) * LANES)
            k_ref[0, rs, sl] = rope(ka[:, sl]).astype(BF16)
        vt_ref[0, :, rs] = proj(OFF_V, KV_WIDTH).T.astype(BF16)
        z_ref[0, rs, :] = (proj(OFF_C, CONV_WIDTH) * proj(OFF_U, CONV_WIDTH)).astype(BF16)
        bg_ref[0, rs, :] = proj(OFF_B, CONV_WIDTH).astype(BF16)
        ga_ref[0, rs, :] = _sigmoid(proj(OFF_GA, D_MODEL)).astype(BF16)
        gc_ref[0, rs, :] = _sigmoid(proj(OFF_GC, D_MODEL)).astype(BF16)


def _proj(x, g1, sh1, sc1, w_in, cos, sin):
    nt = SEQ // TOK_TILE
    tile = lambda w: pl.BlockSpec((1, TOK_TILE, w), lambda b, t: (b, t, 0))
    per_b = pl.BlockSpec((1, 1, D_MODEL), lambda b, t: (b, 0, 0))
    tab = pl.BlockSpec((TOK_TILE, LANES), lambda b, t: (t, 0))
    shp = lambda w: jax.ShapeDtypeStruct((BATCH, SEQ, w), BF16)
    vt_spec = pl.BlockSpec((1, KV_WIDTH, TOK_TILE), lambda b, t: (b, 0, t))
    vt_shape = jax.ShapeDtypeStruct((BATCH, KV_WIDTH, SEQ), BF16)
    return pl.pallas_call(
        _proj_kernel,
        grid=(BATCH, nt),
        in_specs=[
            tile(D_MODEL),
            pl.BlockSpec((1, D_MODEL), lambda b, t: (0, 0)),
            per_b, per_b,
            pl.BlockSpec((D_MODEL, IN_WIDTH), lambda b, t: (0, 0), pipeline_mode=pl.Buffered(1)),
            tab, tab,
        ],
        out_specs=[tile(ATTN_WIDTH), tile(KV_WIDTH), vt_spec, tile(CONV_WIDTH), tile(CONV_WIDTH),
                   tile(D_MODEL), tile(D_MODEL)],
        out_shape=[shp(ATTN_WIDTH), shp(KV_WIDTH), vt_shape, shp(CONV_WIDTH), shp(CONV_WIDTH),
                   shp(D_MODEL), shp(D_MODEL)],
        compiler_params=_params("arbitrary", "arbitrary"),
        name="proj",
    )(x, g1, sh1, sc1, w_in, cos, sin)


def _attn_kernel(sink_ref, q_ref, k_ref, vt_ref, kx_ref, vx_ref, o_ref):
    cols = GROUP * Q_BLOCK
    key = lax.broadcasted_iota(jnp.int32, (Q_BLOCK, cols), 0)
    qry = lax.broadcasted_iota(jnp.int32, (Q_BLOCK, cols), 1) & (Q_BLOCK - 1)
    lane_head = lax.broadcasted_iota(jnp.int32, (1, cols), 1) >> 7
    n_keys = Q_BLOCK + 2 * WINDOW + CTX_LEN
    ones = jnp.ones((BF16_ROWS, n_keys), BF16)

    blocks = []
    for sb in range(ATTN_TILE // Q_BLOCK):
        i = pl.program_id(1) * (ATTN_TILE // Q_BLOCK) + sb
        blocks.append(dict(
            rows=slice(sb * Q_BLOCK, (sb + 1) * Q_BLOCK),
            p0=pl.multiple_of(jnp.maximum(i - 1, 0) * Q_BLOCK, Q_BLOCK),
            c0=pl.multiple_of(i * Q_BLOCK, Q_BLOCK),
            n0=pl.multiple_of(jnp.minimum(i + 1, N_QBLK - 1) * Q_BLOCK, Q_BLOCK),
            ok_prev=key >= qry + jnp.where(i > 0, 0, Q_BLOCK),
            ok_next=key <= qry - jnp.where(i < N_QBLK - 1, 0, Q_BLOCK)))

    def scores(blk, g):
        ks = slice(g * HEAD_DIM, (g + 1) * HEAD_DIM)
        q4 = jnp.concatenate(
            [q_ref[0, blk["rows"], (g * GROUP + j) * HEAD_DIM:(g * GROUP + j + 1) * HEAD_DIM]
             for j in range(GROUP)], axis=0)
        k_all = jnp.concatenate(
            [k_ref[0, pl.ds(blk["p0"], Q_BLOCK), ks], k_ref[0, pl.ds(blk["c0"], Q_BLOCK), ks],
             k_ref[0, pl.ds(blk["n0"], Q_BLOCK), ks], kx_ref[0, :, ks]], axis=0)
        return _dot_nt(k_all, q4)

    def softmax(blk, g, st):
        pieces = [jnp.where(blk["ok_prev"], st[:WINDOW], NEG_INF),
                  st[WINDOW:WINDOW + Q_BLOCK],
                  jnp.where(blk["ok_next"], st[WINDOW + Q_BLOCK:2 * WINDOW + Q_BLOCK], NEG_INF),
                  st[2 * WINDOW + Q_BLOCK:]]
        sink = jnp.zeros((1, cols), F32)
        for j in range(GROUP):
            sink = jnp.where(lane_head == j, sink_ref[g * GROUP + j], sink)
        m = sink
        for piece in pieces:
            m = jnp.maximum(m, jnp.max(piece, axis=0, keepdims=True))
        pt = jnp.concatenate([jnp.exp2((piece - m).astype(BF16)) for piece in pieces], axis=0)
        return pt, jnp.exp2(sink - m)

    def values(blk, g, pt, p_sink):
        ks = slice(g * HEAD_DIM, (g + 1) * HEAD_DIM)
        vt = jnp.concatenate(
            [vt_ref[0, ks, pl.ds(blk["p0"], Q_BLOCK)], vt_ref[0, ks, pl.ds(blk["c0"], Q_BLOCK)],
             vt_ref[0, ks, pl.ds(blk["n0"], Q_BLOCK)], vx_ref[0, ks, :]], axis=1)
        ot = _dot(jnp.concatenate([vt, ones], axis=0), pt)
        denom = ot[HEAD_DIM:HEAD_DIM + 1] + p_sink
        on = ot[:HEAD_DIM] * (1.0 / denom)
        for jj in range(GROUP // 2):
            pair_t = jnp.concatenate(
                [on[:, (2 * jj) * Q_BLOCK:(2 * jj + 1) * Q_BLOCK],
                 on[:, (2 * jj + 1) * Q_BLOCK:(2 * jj + 2) * Q_BLOCK]], axis=0)
            col = (g * GROUP + 2 * jj) * HEAD_DIM
            o_ref[0, blk["rows"], col:col + 2 * HEAD_DIM] = pair_t.T.astype(BF16)

    chains = [(blk, g) for blk in blocks for g in range(N_KV_HEADS)]
    st_next = scores(*chains[0])
    for n, chain in enumerate(chains):
        st = st_next
        if n + 1 < len(chains):
            st_next = scores(*chains[n + 1])
        values(*chain, *softmax(*chain, st))


def _attn(sink, q, k, vt, k_ctx, vt_ctx):
    qspec = pl.BlockSpec((1, ATTN_TILE, ATTN_WIDTH), lambda b, i: (b, i, 0))
    return pl.pallas_call(
        _attn_kernel,
        grid=(BATCH, SEQ // ATTN_TILE),
        in_specs=[pl.BlockSpec(memory_space=pltpu.SMEM), qspec,
                  pl.BlockSpec((1, SEQ, KV_WIDTH), lambda b, i: (b, 0, 0)),
                  pl.BlockSpec((1, KV_WIDTH, SEQ), lambda b, i: (b, 0, 0)),
                  pl.BlockSpec((1, CTX_LEN, KV_WIDTH), lambda b, i: (b, 0, 0)),
                  pl.BlockSpec((1, KV_WIDTH, CTX_LEN), lambda b, i: (b, 0, 0))],
        out_specs=qspec,
        out_shape=jax.ShapeDtypeStruct((BATCH, SEQ, ATTN_WIDTH), BF16),
        compiler_params=_params("arbitrary", "arbitrary"),
        name="attn",
    )(sink, q, k, vt, k_ctx, vt_ctx)


def _merge_kernel(x_ref, oa_ref, z_ref, zp_ref, zn_ref, bg_ref, ga_ref, gc_ref, cw_ref,
                  w3_ref, g1_ref, sh2_ref, sc2_ref, n2_ref, wr_ref,
                  x1_ref, h2_ref, lg_ref):
    t = pl.program_id(1)
    nt = pl.num_programs(1)
    z_before = zp_ref[0, BF16_ROWS - 1:BF16_ROWS, :].astype(F32) * jnp.where(t > 0, 1.0, 0.0)
    z_after = zn_ref[0, 0:1, :].astype(F32) * jnp.where(t < nt - 1, 1.0, 0.0)
    row = lax.broadcasted_iota(jnp.int32, (F32_ROWS, 1), 0)
    n_sub = MERGE_TILE // SUB_TILE
    subs = [slice(r * SUB_TILE, (r + 1) * SUB_TILE) for r in range(n_sub)]

    def conv(r):
        lo, hi = subs[r].start, subs[r].stop
        z = z_ref[0, subs[r], :].astype(F32)
        before = z_before if r == 0 else z_ref[0, lo - 1:lo, :].astype(F32)
        after = z_after if r == n_sub - 1 else z_ref[0, hi:hi + 1, :].astype(F32)
        up = pltpu.roll(z, 1, 0)
        z_prev = jnp.concatenate(
            [jnp.where(row == 0, before, up[:F32_ROWS]), up[F32_ROWS:]], axis=0)
        dn = pltpu.roll(z, SUB_TILE - 1, 0)
        z_next = jnp.concatenate(
            [dn[:-F32_ROWS], jnp.where(row == F32_ROWS - 1, after, dn[-F32_ROWS:])], axis=0)
        y = bg_ref[0, subs[r], :].astype(F32) * (
            cw_ref[0:1, :] * z_prev + cw_ref[1:2, :] * z + cw_ref[2:3, :] * z_next)
        return y.astype(BF16)

    a, s = [], []
    for r in range(n_sub):
        a.append(_dot(oa_ref[0, subs[r], :], w3_ref[:, 0:D_MODEL]))
        s.append(_dot(conv(r), w3_ref[:, D_MODEL:2 * D_MODEL]))
    branch = []
    for r, rs in enumerate(subs):
        merged = ga_ref[0, rs, :].astype(F32) * a[r] + gc_ref[0, rs, :].astype(F32) * s[r]
        branch.append(_dot(merged.astype(BF16), w3_ref[:, 2 * D_MODEL:3 * D_MODEL]))
    for r, rs in enumerate(subs):
        x1 = x_ref[0, rs, :] + g1_ref[0] * branch[r]
        x1_ref[0, rs, :] = x1
        h2 = _norm_mod(x1, n2_ref[...], sh2_ref[0], sc2_ref[0])
        h_hi, h_lo = _split_bf16(h2)
        h2_ref[0, rs, :] = h_hi
        both = _dot_nt(wr_ref[...], h_hi)
        lg_ref[0, :, rs] = (both[:N_EXPERTS] + both[N_EXPERTS:]
                            + _dot_nt(wr_ref[:N_EXPERTS], h_lo))


def _merge(x, oa, z, bg, ga, gc, conv_w, w3, g1, sh2, sc2, n2, wr):
    nt = SEQ // MERGE_TILE
    hb = MERGE_TILE // BF16_ROWS
    n_hb = SEQ // BF16_ROWS
    tile = pl.BlockSpec((1, MERGE_TILE, D_MODEL), lambda b, t: (b, t, 0))
    halo_p = pl.BlockSpec((1, BF16_ROWS, D_MODEL), lambda b, t: (b, jnp.maximum(t * hb - 1, 0), 0))
    halo_n = pl.BlockSpec((1, BF16_ROWS, D_MODEL),
                          lambda b, t: (b, jnp.minimum((t + 1) * hb, n_hb - 1), 0))
    per_b = pl.BlockSpec((1, 1, D_MODEL), lambda b, t: (b, 0, 0))
    full = lambda r, c: pl.BlockSpec((r, c), lambda b, t: (0, 0))
    return pl.pallas_call(
        _merge_kernel,
        grid=(BATCH, nt),
        in_specs=[tile, tile, tile, halo_p, halo_n, tile, tile, tile,
                  full(3, D_MODEL), full(D_MODEL, W3_WIDTH),
                  per_b, per_b, per_b, full(1, D_MODEL), full(2 * N_EXPERTS, D_MODEL)],
        out_specs=[tile, tile, pl.BlockSpec((1, N_EXPERTS, MERGE_TILE), lambda b, t: (b, 0, t))],
        out_shape=[jax.ShapeDtypeStruct((BATCH, SEQ, D_MODEL), F32),
                   jax.ShapeDtypeStruct((BATCH, SEQ, D_MODEL), BF16),
                   jax.ShapeDtypeStruct((BATCH, N_EXPERTS, SEQ), F32)],
        compiler_params=_params("arbitrary", "arbitrary"),
        name="merge",
    )(x, oa, z, z, z, bg, ga, gc, conv_w, w3, g1, sh2, sc2, n2, wr)


def _cumsum_excl(mf, upper):
    nblk = SEQ // LANES
    rows = mf.shape[0]
    stacked = jnp.concatenate([mf[:, j * LANES:(j + 1) * LANES] for j in range(nblk)], axis=0)
    within = _dot(stacked.astype(BF16), upper)
    tot = jnp.sum(stacked, axis=1, keepdims=True)
    off = jnp.zeros((rows, 1), F32)
    out = []
    for j in range(nblk):
        rs = slice(j * rows, (j + 1) * rows)
        out.append(within[rs] + off)
        off = off + tot[rs]
    return jnp.concatenate(out, axis=1)


def _route_kernel(lg_ref, pos_ref, gate_ref, band_ref):
    rows = ROUTE_B * N_EXPERTS
    lg = lg_ref[...]
    ex = jnp.exp(lg - jnp.max(lg, axis=1, keepdims=True))
    aff = (ex / jnp.sum(ex, axis=1, keepdims=True)).reshape(rows, SEQ)

    def body(_, carry):
        lo, hi = carry
        mid = lo + ((hi - lo) >> 1)
        cnt = jnp.sum(jnp.where(aff >= pltpu.bitcast(mid, F32), 1.0, 0.0), axis=1, keepdims=True)
        ge = cnt >= CAP
        return jnp.where(ge, mid, lo), jnp.where(ge, hi, mid)

    one_bits = 0x3F800000
    lo0 = jnp.zeros((rows, 1), jnp.int32)
    hi0 = jnp.full((rows, 1), one_bits + 1, jnp.int32)
    lo, _ = lax.fori_loop(0, 31, body, (lo0, hi0))
    th = pltpu.bitcast(lo, F32)

    ku = lax.broadcasted_iota(jnp.int32, (LANES, LANES), 0)
    nu = lax.broadcasted_iota(jnp.int32, (LANES, LANES), 1)
    upper = jnp.where(ku < nu, 1.0, 0.0).astype(BF16)
    gtf = jnp.where(aff > th, 1.0, 0.0)
    eqf = jnp.where(aff == th, 1.0, 0.0)
    need = CAP - jnp.sum(gtf, axis=1, keepdims=True)
    sel = gtf + eqf * jnp.where(_cumsum_excl(eqf, upper) < need, 1.0, 0.0)
    pos = _cumsum_excl(sel, upper)
    pos_ref[...] = jnp.where(sel > 0.0, pos, -1.0).reshape(ROUTE_B, N_EXPERTS, SEQ)
    gate_ref[...] = jnp.where(sel > 0.0, aff, 0.0).reshape(ROUTE_B, N_EXPERTS, SEQ)
    count = pos + sel
    tok_last_lo = jnp.sum(jnp.where(count <= CAP // 2 - 1, 1.0, 0.0), axis=1, keepdims=True)
    tok_first_hi = jnp.sum(jnp.where(count <= CAP // 2, 1.0, 0.0), axis=1, keepdims=True)
    banded = jnp.logical_and(tok_last_lo < GATHER_LO_END, tok_first_hi >= GATHER_HI_START)
    band_ref[...] = jnp.where(banded, 1, 0).astype(jnp.int32).reshape(ROUTE_B, N_EXPERTS, 1)


def _route(logits):
    spec = pl.BlockSpec((ROUTE_B, N_EXPERTS, SEQ), lambda b: (b, 0, 0))
    shape = jax.ShapeDtypeStruct((BATCH, N_EXPERTS, SEQ), F32)
    return pl.pallas_call(
        _route_kernel,
        grid=(BATCH // ROUTE_B,),
        in_specs=[spec],
        out_specs=[spec, spec, pl.BlockSpec((ROUTE_B, N_EXPERTS, 1), lambda b: (b, 0, 0))],
        out_shape=[shape, shape, jax.ShapeDtypeStruct((BATCH, N_EXPERTS, 1), jnp.int32)],
        compiler_params=_params("arbitrary"),
        name="route",
    )(logits)


def _ffn_kernel(band_ref, pos_ref, h_ref, wg_ref, wu_ref, wd_ref, yet_lo_ref, yet_hi_ref,
                wg_b, wu_b, wdt_b, xe_ref):
    e, b = pl.program_id(0), pl.program_id(1)

    @pl.when(b == 0)
    def _():
        wg_b[...] = wg_ref[0].astype(BF16)
        wu_b[...] = wu_ref[0].astype(BF16)
        wdt_b[...] = wd_ref[0].T.astype(BF16)

    def gather(slots, toks):
        n_slots, n_toks = slots.stop - slots.start, toks.stop - toks.start
        slot = (lax.broadcasted_iota(jnp.int32, (n_slots, n_toks), 0) + slots.start).astype(F32)
        onehot = jnp.where(slot == pos_ref[0, 0, :, toks], 1.0, 0.0).astype(BF16)
        xe_ref[slots, :] = _dot(onehot, h_ref[0, toks, :]).astype(BF16)

    banded = band_ref[b * N_EXPERTS + e] == 1

    @pl.when(banded)
    def _():
        gather(slice(0, CAP // 2), slice(0, GATHER_LO_END))
        gather(slice(CAP // 2, CAP), slice(GATHER_HI_START, SEQ))

    @pl.when(jnp.logical_not(banded))
    def _():
        gather(slice(0, CAP), slice(0, SEQ))

    xe = xe_ref[...]
    act = _dot(xe, wg_b[...])
    up = _dot(xe, wu_b[...])
    hid = (act * _sigmoid(act) * up).astype(BF16)
    ye_t = _dot_nt(wdt_b[...], hid)
    yet_lo_ref[0] = ye_t[:, :CAP // 2].astype(BF16)
    yet_hi_ref[0] = ye_t[:, CAP // 2:].astype(BF16)


def _ffn(band, pos, h2, wg, wu, wd):
    def wspec(lead):
        return pl.BlockSpec(
            (1, D_MODEL, EXPERT_FF),
            lambda e, b: (jnp.minimum(e + jnp.where(b >= BATCH - lead, 1, 0), N_EXPERTS - 1),
                          0, 0))
    wscr = pltpu.VMEM((D_MODEL, EXPERT_FF), BF16)
    half = pl.BlockSpec((1, D_MODEL, CAP // 2), lambda e, b: (b, 0, e))
    half_shape = jax.ShapeDtypeStruct((BATCH, D_MODEL, N_EXPERTS * CAP // 2), BF16)
    return pl.pallas_call(
        _ffn_kernel,
        grid=(N_EXPERTS, BATCH),
        in_specs=[pl.BlockSpec(memory_space=pltpu.SMEM),
                  pl.BlockSpec((1, 1, 1, SEQ), lambda e, b: (b, e, 0, 0)),
                  pl.BlockSpec((1, SEQ, D_MODEL), lambda e, b: (b, 0, 0)),
                  wspec(3), wspec(2), wspec(1)],
        out_specs=[half, half],
        out_shape=[half_shape, half_shape],
        scratch_shapes=[wscr, wscr, wscr, pltpu.VMEM((CAP, D_MODEL), BF16)],
        compiler_params=_params("arbitrary", "arbitrary"),
        name="ffn",
    )(band.reshape(-1), pos[:, :, None, :], h2, wg, wu, wd)


def _combine_kernel(band_ref, pos_ref, gate_ref, yet_lo_ref, yet_hi_ref, x1_ref, g2_ref, fg_ref,
                    o_ref, moe_ref):
    b, c = pl.program_id(0), pl.program_id(1)
    n_banded = band_ref[b * N_EXPERTS]
    for ee in range(1, N_EXPERTS):
        n_banded = n_banded + band_ref[b * N_EXPERTS + ee]
    all_banded = n_banded == N_EXPERTS
    lo_only = jnp.logical_and(all_banded, (c + 1) * SCAT_TOK <= GATHER_HI_START)
    hi_only = jnp.logical_and(all_banded, c * SCAT_TOK >= GATHER_LO_END)

    def half(yet_ref, first_slot):
        slot = (lax.broadcasted_iota(jnp.int32, (CAP // 2, SCAT_TOK), 0) + first_slot).astype(F32)
        scat = jnp.concatenate(
            [jnp.where(slot == pos_ref[0, ee:ee + 1, :], gate_ref[0, ee:ee + 1, :], 0.0)
             .astype(BF16) for ee in range(N_EXPERTS)], axis=0)
        return _dot(yet_ref[0], scat)

    @pl.when(lo_only)
    def _():
        moe_ref[...] = half(yet_lo_ref, 0)

    @pl.when(hi_only)
    def _():
        moe_ref[...] = half(yet_hi_ref, CAP // 2)

    @pl.when(jnp.logical_not(jnp.logical_or(lo_only, hi_only)))
    def _():
        moe_ref[...] = half(yet_lo_ref, 0) + half(yet_hi_ref, CAP // 2)

    x2 = x1_ref[0] + g2_ref[0] * moe_ref[...].T
    o_ref[0] = x2 * lax.rsqrt(jnp.mean(x2 * x2, axis=-1, keepdims=True) + EPS) * fg_ref[...]


def _combine(band, pos, gate, yet_lo, yet_hi, x1, g2, fg):
    sel_c = pl.BlockSpec((1, N_EXPERTS, SCAT_TOK), lambda b, c: (b, 0, c))
    tok_c = pl.BlockSpec((1, SCAT_TOK, D_MODEL), lambda b, c: (b, c, 0))
    yet_spec = pl.BlockSpec((1, D_MODEL, N_EXPERTS * CAP // 2), lambda b, c: (b, 0, 0))
    return pl.pallas_call(
        _combine_kernel,
        grid=(BATCH, N_SCAT),
        in_specs=[pl.BlockSpec(memory_space=pltpu.SMEM), sel_c, sel_c, yet_spec, yet_spec,
                  tok_c,
                  pl.BlockSpec((1, 1, D_MODEL), lambda b, c: (b, 0, 0)),
                  pl.BlockSpec((1, D_MODEL), lambda b, c: (0, 0))],
        out_specs=tok_c,
        out_shape=jax.ShapeDtypeStruct((BATCH, SEQ, D_MODEL), F32),
        scratch_shapes=[pltpu.VMEM((D_MODEL, SCAT_TOK), F32)],
        compiler_params=_params("arbitrary", "arbitrary"),
        name="combine",
    )(band.reshape(-1), pos, gate, yet_lo, yet_hi, x1, g2, fg)


def _rope_tables():
    rows = SEQ // GRID_W
    row = jnp.repeat(jnp.arange(rows, dtype=F32), GRID_W)
    col = jnp.tile(jnp.arange(GRID_W, dtype=F32), rows)
    n_freq = HEAD_DIM // 4
    inv_freq = ROPE_BASE ** (-jnp.arange(n_freq, dtype=F32) / n_freq)
    ang_r = row[:, None] * inv_freq[None, :]
    ang_c = col[:, None] * inv_freq[None, :]
    ang = jnp.concatenate([ang_r, ang_r, ang_c, ang_c], axis=-1)
    sign = jnp.tile(jnp.concatenate([-jnp.ones(n_freq, F32), jnp.ones(n_freq, F32)]), 2)
    reps = LANES // HEAD_DIM
    return jnp.tile(jnp.cos(ang), (1, reps)), jnp.tile(jnp.sin(ang) * sign, (1, reps))


def kernel(x, c, ctx, c_ctx, w_mod, b_mod, norm1_g, w_in, attn_sink, conv_w, w_proj_attn,
           w_proj_conv, w_out, norm2_g, w_router, w_exp_gate, w_exp_up, w_exp_down, final_norm_g):
    assert x.shape == (BATCH, SEQ, D_MODEL) and ctx.shape == (BATCH, CTX_LEN, D_MODEL)
    assert w_mod.shape[0] == 1, "single-layer problem"
    layer = 0

    cvec = jnp.concatenate(
        [c, c_ctx[None, :], jnp.zeros((MOD_ROWS - BATCH - 1, D_MODEL), F32)], axis=0)
    mod = _mod(cvec, w_mod[layer], b_mod[layer][None, :])
    chunk = lambda rows, k: rows[:, k * D_MODEL:(k + 1) * D_MODEL]
    mod_x = mod[:BATCH][:, None, :]
    sh1, sc1, g1, sh2, sc2, g2 = (mod_x[..., k * D_MODEL:(k + 1) * D_MODEL] for k in range(N_MOD))
    mod_c = mod[BATCH:BATCH + 1]
    csh1, csc1 = chunk(mod_c, 0), chunk(mod_c, 1)

    n1 = norm1_g[layer][None, :]
    w_in_b = w_in[layer].astype(BF16)
    k_ctx, vt_ctx = _ctx_kv(ctx, n1, csh1, csc1, w_in_b[:, OFF_K:OFF_U])

    cos, sin = _rope_tables()
    q, k, vt, z, bg, ga, gc = _proj(x, n1, sh1, sc1, w_in_b, cos, sin)
    sink_log2 = attn_sink[layer].reshape(-1) * LOG2E
    o_attn = _attn(sink_log2, q, k, vt, k_ctx, vt_ctx)

    wr_hi, wr_lo = _split_bf16(w_router[layer].T)
    w3 = jnp.concatenate(
        [w_proj_attn[layer], w_proj_conv[layer], w_out[layer],
         jnp.zeros((D_MODEL, W3_WIDTH - 3 * D_MODEL), F32)], axis=1).astype(BF16)
    x1, h2, logits = _merge(
        x, o_attn, z, bg, ga, gc, conv_w[layer], w3,
        g1, sh2, sc2, norm2_g[layer][None, :], jnp.concatenate([wr_hi, wr_lo], axis=0))

    pos, gate, band = _route(logits)
    yet_lo, yet_hi = _ffn(band, pos, h2, w_exp_gate[layer], w_exp_up[layer], w_exp_down[layer])
    return _combine(band, pos, gate, yet_lo, yet_hi, x1, g2, final_norm_g[None, :])
```

```python
import jax
import jax.numpy as jnp
import numpy as np
from jax import lax
from jax.experimental import pallas as pl
from jax.experimental.pallas import tpu as pltpu

D_MODEL = 1024
BATCH = 16
SEQ = 2048
CTX_LEN = 256
GRID_W = 64
N_HEADS = 16
N_KV_HEADS = 4
GROUP = N_HEADS // N_KV_HEADS
HEAD_DIM = D_MODEL // N_HEADS
ATTN_WIDTH = N_HEADS * HEAD_DIM
KV_WIDTH = N_KV_HEADS * HEAD_DIM
WINDOW = 128
Q_BLOCK = 128
ROPE_BASE = 10000.0
CONV_WIDTH = D_MODEL
N_EXPERTS = 16
EXPERT_FF = D_MODEL
CAPACITY_FACTOR = 2
N_MOD = 6
EPS = 1e-6
NEG_INF = -1e30
LOG2E = 1.4426950408889634

CAP = CAPACITY_FACTOR * SEQ // N_EXPERTS
N_QBLK = SEQ // Q_BLOCK
LANES = 128
F32_ROWS = 8
BF16_ROWS = 16
W3_WIDTH = 3 * D_MODEL + LANES
TOK_TILE = 512
PROJ_SUB = 256
ATTN_TILE = 1024
MERGE_TILE = 512
SUB_TILE = 256
CTX_B = 4
ROUTE_B = 4
GATHER_LO_END = 5 * SEQ // 8
GATHER_HI_START = 3 * SEQ // 8
N_SCAT = 8
SCAT_TOK = SEQ // N_SCAT
MOD_ROWS = 32
VMEM_LIMIT = 56 * 1024 * 1024

OFF_Q = 0
OFF_K = OFF_Q + ATTN_WIDTH
OFF_V = OFF_K + KV_WIDTH
OFF_U = OFF_V + KV_WIDTH
OFF_B = OFF_U + CONV_WIDTH
OFF_C = OFF_B + CONV_WIDTH
OFF_GA = OFF_C + CONV_WIDTH
OFF_GC = OFF_GA + D_MODEL
IN_WIDTH = OFF_GC + D_MODEL

F32 = jnp.float32
BF16 = jnp.bfloat16


def _params(*sem):
    return pltpu.CompilerParams(dimension_semantics=sem, vmem_limit_bytes=VMEM_LIMIT)


def _dot(a, b):
    return jnp.dot(a, b, preferred_element_type=F32)


def _dot_nt(a, b):
    return lax.dot_general(a, b, (((1,), (1,)), ((), ())), preferred_element_type=F32)


def _sigmoid(x):
    return 1.0 / (1.0 + jnp.exp(-x))


def _split_bf16(x):
    hi = x.astype(BF16)
    lo = (x - hi.astype(F32)).astype(BF16)
    return hi, lo


def _norm_mod(x, g, shift, scale):
    y = x * lax.rsqrt(jnp.mean(x * x, axis=-1, keepdims=True) + EPS) * g
    return y * (1.0 + scale) + shift


def _mod_kernel(c_ref, w_ref, b_ref, o_ref):
    cv = c_ref[...]
    s_hi, s_lo = _split_bf16(cv * _sigmoid(cv))
    w_hi, w_lo = _split_bf16(w_ref[...])
    o_ref[...] = _dot(s_hi, w_hi) + _dot(s_lo, w_hi) + _dot(s_hi, w_lo) + b_ref[...]


def _mod(cvec, w_mod, b_mod):
    n_out = N_MOD * D_MODEL
    blk = D_MODEL
    return pl.pallas_call(
        _mod_kernel,
        grid=(n_out // blk,),
        in_specs=[
            pl.BlockSpec((MOD_ROWS, D_MODEL), lambda j: (0, 0)),
            pl.BlockSpec((D_MODEL, blk), lambda j: (0, j)),
            pl.BlockSpec((1, blk), lambda j: (0, j)),
        ],
        out_specs=pl.BlockSpec((MOD_ROWS, blk), lambda j: (0, j)),
        out_shape=jax.ShapeDtypeStruct((MOD_ROWS, n_out), F32),
        compiler_params=_params("arbitrary"),
        name="mod",
    )(cvec, w_mod, b_mod)


def _ctx_kernel(ctx_ref, g_ref, sh_ref, sc_ref, w_ref, k_ref, vt_ref):
    x = ctx_ref[...].reshape(CTX_B * CTX_LEN, D_MODEL)
    h = _norm_mod(x, g_ref[...], sh_ref[...], sc_ref[...]).astype(BF16)
    kv = _dot(h, w_ref[...])
    k_ref[...] = kv[:, :KV_WIDTH].astype(BF16).reshape(CTX_B, CTX_LEN, KV_WIDTH)
    for i in range(CTX_B):
        vt_ref[i] = kv[i * CTX_LEN:(i + 1) * CTX_LEN, KV_WIDTH:].T.astype(BF16)


def _ctx_kv(ctx, g1, csh1, csc1, w_kv):
    row = pl.BlockSpec((1, D_MODEL), lambda b: (0, 0))
    return pl.pallas_call(
        _ctx_kernel,
        grid=(BATCH // CTX_B,),
        in_specs=[
            pl.BlockSpec((CTX_B, CTX_LEN, D_MODEL), lambda b: (b, 0, 0)),
            row, row, row,
            pl.BlockSpec((D_MODEL, 2 * KV_WIDTH), lambda b: (0, 0)),
        ],
        out_specs=[pl.BlockSpec((CTX_B, CTX_LEN, KV_WIDTH), lambda b: (b, 0, 0)),
                   pl.BlockSpec((CTX_B, KV_WIDTH, CTX_LEN), lambda b: (b, 0, 0))],
        out_shape=[jax.ShapeDtypeStruct((BATCH, CTX_LEN, KV_WIDTH), BF16),
                   jax.ShapeDtypeStruct((BATCH, KV_WIDTH, CTX_LEN), BF16)],
        compiler_params=_params("arbitrary"),
        name="ctx_kv",
    )(ctx, g1, csh1, csc1, w_kv)


def _proj_kernel(x_ref, g_ref, sh_ref, sc_ref, w_ref, cos_ref, sin_ref,
                 q_ref, k_ref, vt_ref, z_ref, bg_ref, ga_ref, gc_ref):
    lane = lax.broadcasted_iota(jnp.int32, (PROJ_SUB, LANES), 1)
    first_half = (lane & (HEAD_DIM // 4)) == 0
    scale = HEAD_DIM ** -0.5 * LOG2E

    for r in range(TOK_TILE // PROJ_SUB):
        rs = slice(r * PROJ_SUB, (r + 1) * PROJ_SUB)
        h = _norm_mod(x_ref[0, rs, :], g_ref[...], sh_ref[0], sc_ref[0]).astype(BF16)
        cos = cos_ref[rs, :]
        sin = sin_ref[rs, :]

        def rope(a):
            rot = jnp.where(first_half,
                            pltpu.roll(a, LANES - HEAD_DIM // 4, 1),
                            pltpu.roll(a, HEAD_DIM // 4, 1))
            return a * cos + rot * sin

        def proj(off, width):
            return _dot(h, w_ref[:, off:off + width])

        qa = proj(OFF_Q, ATTN_WIDTH)
        for j in range(ATTN_WIDTH // LANES):
            sl = slice(j * LANES, (j + 1) * LANES)
            q_ref[0, rs, sl] = (rope(qa[:, sl]) * scale).astype(BF16)
        ka = proj(OFF_K, KV_WIDTH)
        for j in range(KV_WIDTH // LANES):
            sl = slice(j * LANES, (j + 1) * LANES)
            k_ref[0, rs, sl] = rope(ka[:, sl]).astype(BF16)
        vt_ref[0, :, rs] = proj(OFF_V, KV_WIDTH).T.astype(BF16)
        z_ref[0, rs, :] = (proj(OFF_C, CONV_WIDTH) * proj(OFF_U, CONV_WIDTH)).astype(BF16)
        bg_ref[0, rs, :] = proj(OFF_B, CONV_WIDTH).astype(BF16)
        ga_ref[0, rs, :] = _sigmoid(proj(OFF_GA, D_MODEL)).astype(BF16)
        gc_ref[0, rs, :] = _sigmoid(proj(OFF_GC, D_MODEL)).astype(BF16)


def _proj(x, g1, sh1, sc1, w_in, cos, sin):
    nt = SEQ // TOK_TILE
    tile = lambda w: pl.BlockSpec((1, TOK_TILE, w), lambda b, t: (b, t, 0))
    per_b = pl.BlockSpec((1, 1, D_MODEL), lambda b, t: (b, 0, 0))
    tab = pl.BlockSpec((TOK_TILE, LANES), lambda b, t: (t, 0))
    shp = lambda w: jax.ShapeDtypeStruct((BATCH, SEQ, w), BF16)
    vt_spec = pl.BlockSpec((1, KV_WIDTH, TOK_TILE), lambda b, t: (b, 0, t))
    vt_shape = jax.ShapeDtypeStruct((BATCH, KV_WIDTH, SEQ), BF16)
    return pl.pallas_call(
        _proj_kernel,
        grid=(BATCH, nt),
        in_specs=[
            tile(D_MODEL),
            pl.BlockSpec((1, D_MODEL), lambda b, t: (0, 0)),
            per_b, per_b,
            pl.BlockSpec((D_MODEL, IN_WIDTH), lambda b, t: (0, 0), pipeline_mode=pl.Buffered(1)),
            tab, tab,
        ],
        out_specs=[tile(ATTN_WIDTH), tile(KV_WIDTH), vt_spec, tile(CONV_WIDTH), tile(CONV_WIDTH),
                   tile(D_MODEL), tile(D_MODEL)],
        out_shape=[shp(ATTN_WIDTH), shp(KV_WIDTH), vt_shape, shp(CONV_WIDTH), shp(CONV_WIDTH),
                   shp(D_MODEL), shp(D_MODEL)],
        compiler_params=_params("arbitrary", "arbitrary"),
        name="proj",
    )(x, g1, sh1, sc1, w_in, cos, sin)


def _attn_kernel(sink_ref, q_ref, k_ref, vt_ref, kx_ref, vx_ref, o_ref):
    cols = GROUP * Q_BLOCK
    key = lax.broadcasted_iota(jnp.int32, (Q_BLOCK, cols), 0)
    qry = lax.broadcasted_iota(jnp.int32, (Q_BLOCK, cols), 1) & (Q_BLOCK - 1)
    lane_head = lax.broadcasted_iota(jnp.int32, (1, cols), 1) >> 7
    n_keys = Q_BLOCK + 2 * WINDOW + CTX_LEN
    ones = jnp.ones((BF16_ROWS, n_keys), BF16)

    blocks = []
    for sb in range(ATTN_TILE // Q_BLOCK):
        i = pl.program_id(1) * (ATTN_TILE // Q_BLOCK) + sb
        blocks.append(dict(
            rows=slice(sb * Q_BLOCK, (sb + 1) * Q_BLOCK),
            p0=pl.multiple_of(jnp.maximum(i - 1, 0) * Q_BLOCK, Q_BLOCK),
            c0=pl.multiple_of(i * Q_BLOCK, Q_BLOCK),
            n0=pl.multiple_of(jnp.minimum(i + 1, N_QBLK - 1) * Q_BLOCK, Q_BLOCK),
            ok_prev=key >= qry + jnp.where(i > 0, 0, Q_BLOCK),
            ok_next=key <= qry - jnp.where(i < N_QBLK - 1, 0, Q_BLOCK)))

    def scores(blk, g):
        ks = slice(g * HEAD_DIM, (g + 1) * HEAD_DIM)
        q4 = jnp.concatenate(
            [q_ref[0, blk["rows"], (g * GROUP + j) * HEAD_DIM:(g * GROUP + j + 1) * HEAD_DIM]
             for j in range(GROUP)], axis=0)
        k_all = jnp.concatenate(
            [k_ref[0, pl.ds(blk["p0"], Q_BLOCK), ks], k_ref[0, pl.ds(blk["c0"], Q_BLOCK), ks],
             k_ref[0, pl.ds(blk["n0"], Q_BLOCK), ks], kx_ref[0, :, ks]], axis=0)
        return _dot_nt(k_all, q4)

    def softmax(blk, g, st):
        pieces = [jnp.where(blk["ok_prev"], st[:WINDOW], NEG_INF),
                  st[WINDOW:WINDOW + Q_BLOCK],
                  jnp.where(blk["ok_next"], st[WINDOW + Q_BLOCK:2 * WINDOW + Q_BLOCK], NEG_INF),
                  st[2 * WINDOW + Q_BLOCK:]]
        sink = jnp.zeros((1, cols), F32)
        for j in range(GROUP):
            sink = jnp.where(lane_head == j, sink_ref[g * GROUP + j], sink)
        m = sink
        for piece in pieces:
            m = jnp.maximum(m, jnp.max(piece, axis=0, keepdims=True))
        pt = jnp.concatenate([jnp.exp2((piece - m).astype(BF16)) for piece in pieces], axis=0)
        return pt, jnp.exp2(sink - m)

    def values(blk, g, pt, p_sink):
        ks = slice(g * HEAD_DIM, (g + 1) * HEAD_DIM)
        vt = jnp.concatenate(
            [vt_ref[0, ks, pl.ds(blk["p0"], Q_BLOCK)], vt_ref[0, ks, pl.ds(blk["c0"], Q_BLOCK)],
             vt_ref[0, ks, pl.ds(blk["n0"], Q_BLOCK)], vx_ref[0, ks, :]], axis=1)
        ot = _dot(jnp.concatenate([vt, ones], axis=0), pt)
        denom = ot[HEAD_DIM:HEAD_DIM + 1] + p_sink
        on = ot[:HEAD_DIM] * (1.0 / denom)
        for jj in range(GROUP // 2):
            pair_t = jnp.concatenate(
                [on[:, (2 * jj) * Q_BLOCK:(2 * jj + 1) * Q_BLOCK],
                 on[:, (2 * jj + 1) * Q_BLOCK:(2 * jj + 2) * Q_BLOCK]], axis=0)
            col = (g * GROUP + 2 * jj) * HEAD_DIM
            o_ref[0, blk["rows"], col:col + 2 * HEAD_DIM] = pair_t.T.astype(BF16)

    chains = [(blk, g) for blk in blocks for g in range(N_KV_HEADS)]
    st_next = scores(*chains[0])
    for n, chain in enumerate(chains):
        st = st_next
        if n + 1 < len(chains):
            st_next = scores(*chains[n + 1])
        values(*chain, *softmax(*chain, st))


def _attn(sink, q, k, vt, k_ctx, vt_ctx):
    qspec = pl.BlockSpec((1, ATTN_TILE, ATTN_WIDTH), lambda b, i: (b, i, 0))
    return pl.pallas_call(
        _attn_kernel,
        grid=(BATCH, SEQ // ATTN_TILE),
        in_specs=[pl.BlockSpec(memory_space=pltpu.SMEM), qspec,
                  pl.BlockSpec((1, SEQ, KV_WIDTH), lambda b, i: (b, 0, 0)),
                  pl.BlockSpec((1, KV_WIDTH, SEQ), lambda b, i: (b, 0, 0)),
                  pl.BlockSpec((1, CTX_LEN, KV_WIDTH), lambda b, i: (b, 0, 0)),
                  pl.BlockSpec((1, KV_WIDTH, CTX_LEN), lambda b, i: (b, 0, 0))],
        out_specs=qspec,
        out_shape=jax.ShapeDtypeStruct((BATCH, SEQ, ATTN_WIDTH), BF16),
        compiler_params=_params("arbitrary", "arbitrary"),
        name="attn",
    )(sink, q, k, vt, k_ctx, vt_ctx)


def _merge_kernel(x_ref, oa_ref, z_ref, zp_ref, zn_ref, bg_ref, ga_ref, gc_ref, cw_ref,
                  w3_ref, g1_ref, sh2_ref, sc2_ref, n2_ref, wr_ref,
                  x1_ref, h2_ref, lg_ref):
    t = pl.program_id(1)
    nt = pl.num_programs(1)
    z_before = zp_ref[0, BF16_ROWS - 1:BF16_ROWS, :].astype(F32) * jnp.where(t > 0, 1.0, 0.0)
    z_after = zn_ref[0, 0:1, :].astype(F32) * jnp.where(t < nt - 1, 1.0, 0.0)
    row = lax.broadcasted_iota(jnp.int32, (F32_ROWS, 1), 0)
    n_sub = MERGE_TILE // SUB_TILE
    subs = [slice(r * SUB_TILE, (r + 1) * SUB_TILE) for r in range(n_sub)]

    def conv(r):
        lo, hi = subs[r].start, subs[r].stop
        z = z_ref[0, subs[r], :].astype(F32)
        before = z_before if r == 0 else z_ref[0, lo - 1:lo, :].astype(F32)
        after = z_after if r == n_sub - 1 else z_ref[0, hi:hi + 1, :].astype(F32)
        up = pltpu.roll(z, 1, 0)
        z_prev = jnp.concatenate(
            [jnp.where(row == 0, before, up[:F32_ROWS]), up[F32_ROWS:]], axis=0)
        dn = pltpu.roll(z, SUB_TILE - 1, 0)
        z_next = jnp.concatenate(
            [dn[:-F32_ROWS], jnp.where(row == F32_ROWS - 1, after, dn[-F32_ROWS:])], axis=0)
        y = bg_ref[0, subs[r], :].astype(F32) * (
            cw_ref[0:1, :] * z_prev + cw_ref[1:2, :] * z + cw_ref[2:3, :] * z_next)
        return y.astype(BF16)

    a, s = [], []
    for r in range(n_sub):
        a.append(_dot(oa_ref[0, subs[r], :], w3_ref[:, 0:D_MODEL]))
        s.append(_dot(conv(r), w3_ref[:, D_MODEL:2 * D_MODEL]))
    branch = []
    for r, rs in enumerate(subs):
        merged = ga_ref[0, rs, :].astype(F32) * a[r] + gc_ref[0, rs, :].astype(F32) * s[r]
        branch.append(_dot(merged.astype(BF16), w3_ref[:, 2 * D_MODEL:3 * D_MODEL]))
    for r, rs in enumerate(subs):
        x1 = x_ref[0, rs, :] + g1_ref[0] * branch[r]
        x1_ref[0, rs, :] = x1
        h2 = _norm_mod(x1, n2_ref[...], sh2_ref[0], sc2_ref[0])
        h_hi, h_lo = _split_bf16(h2)
        h2_ref[0, rs, :] = h_hi
        both = _dot_nt(wr_ref[...], h_hi)
        lg_ref[0, :, rs] = (both[:N_EXPERTS] + both[N_EXPERTS:]
                            + _dot_nt(wr_ref[:N_EXPERTS], h_lo))


def _merge(x, oa, z, bg, ga, gc, conv_w, w3, g1, sh2, sc2, n2, wr):
    nt = SEQ // MERGE_TILE
    hb = MERGE_TILE // BF16_ROWS
    n_hb = SEQ // BF16_ROWS
    tile = pl.BlockSpec((1, MERGE_TILE, D_MODEL), lambda b, t: (b, t, 0))
    halo_p = pl.BlockSpec((1, BF16_ROWS, D_MODEL), lambda b, t: (b, jnp.maximum(t * hb - 1, 0), 0))
    halo_n = pl.BlockSpec((1, BF16_ROWS, D_MODEL),
                          lambda b, t: (b, jnp.minimum((t + 1) * hb, n_hb - 1), 0))
    per_b = pl.BlockSpec((1, 1, D_MODEL), lambda b, t: (b, 0, 0))
    full = lambda r, c: pl.BlockSpec((r, c), lambda b, t: (0, 0))
    return pl.pallas_call(
        _merge_kernel,
        grid=(BATCH, nt),
        in_specs=[tile, tile, tile, halo_p, halo_n, tile, tile, tile,
                  full(3, D_MODEL), full(D_MODEL, W3_WIDTH),
                  per_b, per_b, per_b, full(1, D_MODEL), full(2 * N_EXPERTS, D_MODEL)],
        out_specs=[tile, tile, pl.BlockSpec((1, N_EXPERTS, MERGE_TILE), lambda b, t: (b, 0, t))],
        out_shape=[jax.ShapeDtypeStruct((BATCH, SEQ, D_MODEL), F32),
                   jax.ShapeDtypeStruct((BATCH, SEQ, D_MODEL), BF16),
                   jax.ShapeDtypeStruct((BATCH, N_EXPERTS, SEQ), F32)],
        compiler_params=_params("arbitrary", "arbitrary"),
        name="merge",
    )(x, oa, z, z, z, bg, ga, gc, conv_w, w3, g1, sh2, sc2, n2, wr)


def _cumsum_excl(mf, upper):
    nblk = SEQ // LANES
    rows = mf.shape[0]
    stacked = jnp.concatenate([mf[:, j * LANES:(j + 1) * LANES] for j in range(nblk)], axis=0)
    within = _dot(stacked.astype(BF16), upper)
    tot = jnp.sum(stacked, axis=1, keepdims=True)
    off = jnp.zeros((rows, 1), F32)
    out = []
    for j in range(nblk):
        rs = slice(j * rows, (j + 1) * rows)
        out.append(within[rs] + off)
        off = off + tot[rs]
    return jnp.concatenate(out, axis=1)


def _route_kernel(lg_ref, pos_ref, gate_ref, band_ref):
    rows = ROUTE_B * N_EXPERTS
    lg = lg_ref[...]
    ex = jnp.exp(lg - jnp.max(lg, axis=1, keepdims=True))
    aff = (ex / jnp.sum(ex, axis=1, keepdims=True)).reshape(rows, SEQ)

    def body(_, carry):
        lo, hi = carry
        mid = lo + ((hi - lo) >> 1)
        cnt = jnp.sum(jnp.where(aff >= pltpu.bitcast(mid, F32), 1.0, 0.0), axis=1, keepdims=True)
        ge = cnt >= CAP
        return jnp.where(ge, mid, lo), jnp.where(ge, hi, mid)

    one_bits = 0x3F800000
    lo0 = jnp.zeros((rows, 1), jnp.int32)
    hi0 = jnp.full((rows, 1), one_bits + 1, jnp.int32)
    lo, _ = lax.fori_loop(0, 31, body, (lo0, hi0))
    th = pltpu.bitcast(lo, F32)

    ku = lax.broadcasted_iota(jnp.int32, (LANES, LANES), 0)
    nu = lax.broadcasted_iota(jnp.int32, (LANES, LANES), 1)
    upper = jnp.where(ku < nu, 1.0, 0.0).astype(BF16)
    gtf = jnp.where(aff > th, 1.0, 0.0)
    eqf = jnp.where(aff == th, 1.0, 0.0)
    need = CAP - jnp.sum(gtf, axis=1, keepdims=True)
    sel = gtf + eqf * jnp.where(_cumsum_excl(eqf, upper) < need, 1.0, 0.0)
    pos = _cumsum_excl(sel, upper)
    pos_ref[...] = jnp.where(sel > 0.0, pos, -1.0).reshape(ROUTE_B, N_EXPERTS, SEQ)
    gate_ref[...] = jnp.where(sel > 0.0, aff, 0.0).reshape(ROUTE_B, N_EXPERTS, SEQ)
    count = pos + sel
    tok_last_lo = jnp.sum(jnp.where(count <= CAP // 2 - 1, 1.0, 0.0), axis=1, keepdims=True)
    tok_first_hi = jnp.sum(jnp.where(count <= CAP // 2, 1.0, 0.0), axis=1, keepdims=True)
    banded = jnp.logical_and(tok_last_lo < GATHER_LO_END, tok_first_hi >= GATHER_HI_START)
    band_ref[...] = jnp.where(banded, 1, 0).astype(jnp.int32).reshape(ROUTE_B, N_EXPERTS, 1)


def _route(logits):
    spec = pl.BlockSpec((ROUTE_B, N_EXPERTS, SEQ), lambda b: (b, 0, 0))
    shape = jax.ShapeDtypeStruct((BATCH, N_EXPERTS, SEQ), F32)
    return pl.pallas_call(
        _route_kernel,
        grid=(BATCH // ROUTE_B,),
        in_specs=[spec],
        out_specs=[spec, spec, pl.BlockSpec((ROUTE_B, N_EXPERTS, 1), lambda b: (b, 0, 0))],
        out_shape=[shape, shape, jax.ShapeDtypeStruct((BATCH, N_EXPERTS, 1), jnp.int32)],
        compiler_params=_params("arbitrary"),
        name="route",
    )(logits)


def _ffn_kernel(band_ref, pos_ref, h_ref, wg_ref, wu_ref, wd_ref, yet_lo_ref, yet_hi_ref,
                wg_b, wu_b, wdt_b, xe_ref):
    e, b = pl.program_id(0), pl.program_id(1)

    @pl.when(b == 0)
    def _():
        wg_b[...] = wg_ref[0].astype(BF16)
        wu_b[...] = wu_ref[0].astype(BF16)
        wdt_b[...] = wd_ref[0].T.astype(BF16)

    def gather(slots, toks):
        n_slots, n_toks = slots.stop - slots.start, toks.stop - toks.start
        slot = (lax.broadcasted_iota(jnp.int32, (n_slots, n_toks), 0) + slots.start).astype(F32)
        onehot = jnp.where(slot == pos_ref[0, 0, :, toks], 1.0, 0.0).astype(BF16)
        xe_ref[slots, :] = _dot(onehot, h_ref[0, toks, :]).astype(BF16)

    banded = band_ref[b * N_EXPERTS + e] == 1

    @pl.when(banded)
    def _():
        gather(slice(0, CAP // 2), slice(0, GATHER_LO_END))
        gather(slice(CAP // 2, CAP), slice(GATHER_HI_START, SEQ))

    @pl.when(jnp.logical_not(banded))
    def _():
        gather(slice(0, CAP), slice(0, SEQ))

    xe = xe_ref[...]
    act = _dot(xe, wg_b[...])
    up = _dot(xe, wu_b[...])
    hid = (act * _sigmoid(act) * up).astype(BF16)
    ye_t = _dot_nt(wdt_b[...], hid)
    yet_lo_ref[0] = ye_t[:, :CAP // 2].astype(BF16)
    yet_hi_ref[0] = ye_t[:, CAP // 2:].astype(BF16)


def _ffn(band, pos, h2, wg, wu, wd):
    def wspec(lead):
        return pl.BlockSpec(
            (1, D_MODEL, EXPERT_FF),
            lambda e, b: (jnp.minimum(e + jnp.where(b >= BATCH - lead, 1, 0), N_EXPERTS - 1),
                          0, 0))
    wscr = pltpu.VMEM((D_MODEL, EXPERT_FF), BF16)
    half = pl.BlockSpec((1, D_MODEL, CAP // 2), lambda e, b: (b, 0, e))
    half_shape = jax.ShapeDtypeStruct((BATCH, D_MODEL, N_EXPERTS * CAP // 2), BF16)
    return pl.pallas_call(
        _ffn_kernel,
        grid=(N_EXPERTS, BATCH),
        in_specs=[pl.BlockSpec(memory_space=pltpu.SMEM),
                  pl.BlockSpec((1, 1, 1, SEQ), lambda e, b: (b, e, 0, 0)),
                  pl.BlockSpec((1, SEQ, D_MODEL), lambda e, b: (b, 0, 0)),
                  wspec(3), wspec(2), wspec(1)],
        out_specs=[half, half],
        out_shape=[half_shape, half_shape],
        scratch_shapes=[wscr, wscr, wscr, pltpu.VMEM((CAP, D_MODEL), BF16)],
        compiler_params=_params("arbitrary", "arbitrary"),
        name="ffn",
    )(band.reshape(-1), pos[:, :, None, :], h2, wg, wu, wd)


def _combine_kernel(band_ref, pos_ref, gate_ref, yet_lo_ref, yet_hi_ref, x1_ref, g2_ref, fg_ref,
                    o_ref, moe_ref):
    b, c = pl.program_id(0), pl.program_id(1)
    n_banded = band_ref[b * N_EXPERTS]
    for ee in range(1, N_EXPERTS):
        n_banded = n_banded + band_ref[b * N_EXPERTS + ee]
    all_banded = n_banded == N_EXPERTS
    lo_only = jnp.logical_and(all_banded, (c + 1) * SCAT_TOK <= GATHER_HI_START)
    hi_only = jnp.logical_and(all_banded, c * SCAT_TOK >= GATHER_LO_END)

    def half(yet_ref, first_slot):
        slot = (lax.broadcasted_iota(jnp.int32, (CAP // 2, SCAT_TOK), 0) + first_slot).astype(F32)
        scat = jnp.concatenate(
            [jnp.where(slot == pos_ref[0, ee:ee + 1, :], gate_ref[0, ee:ee + 1, :], 0.0)
             .astype(BF16) for ee in range(N_EXPERTS)], axis=0)
        return _dot(yet_ref[0], scat)

    @pl.when(lo_only)
    def _():
        moe_ref[...] = half(yet_lo_ref, 0)

    @pl.when(hi_only)
    def _():
        moe_ref[...] = half(yet_hi_ref, CAP // 2)

    @pl.when(jnp.logical_not(jnp.logical_or(lo_only, hi_only)))
    def _():
        moe_ref[...] = half(yet_lo_ref, 0) + half(yet_hi_ref, CAP // 2)

    x2 = x1_ref[0] + g2_ref[0] * moe_ref[...].T
    o_ref[0] = x2 * lax.rsqrt(jnp.mean(x2 * x2, axis=-1, keepdims=True) + EPS) * fg_ref[...]


def _combine(band, pos, gate, yet_lo, yet_hi, x1, g2, fg):
    sel_c = pl.BlockSpec((1, N_EXPERTS, SCAT_TOK), lambda b, c: (b, 0, c))
    tok_c = pl.BlockSpec((1, SCAT_TOK, D_MODEL), lambda b, c: (b, c, 0))
    yet_spec = pl.BlockSpec((1, D_MODEL, N_EXPERTS * CAP // 2), lambda b, c: (b, 0, 0))
    return pl.pallas_call(
        _combine_kernel,
        grid=(BATCH, N_SCAT),
        in_specs=[pl.BlockSpec(memory_space=pltpu.SMEM), sel_c, sel_c, yet_spec, yet_spec,
                  tok_c,
                  pl.BlockSpec((1, 1, D_MODEL), lambda b, c: (b, 0, 0)),
                  pl.BlockSpec((1, D_MODEL), lambda b, c: (0, 0))],
        out_specs=tok_c,
        out_shape=jax.ShapeDtypeStruct((BATCH, SEQ, D_MODEL), F32),
        scratch_shapes=[pltpu.VMEM((D_MODEL, SCAT_TOK), F32)],
        compiler_params=_params("arbitrary", "arbitrary"),
        name="combine",
    )(band.reshape(-1), pos, gate, yet_lo, yet_hi, x1, g2, fg)


def _rope_tables():
    rows = SEQ // GRID_W
    row = np.repeat(np.arange(rows, dtype=np.float32), GRID_W)
    col = np.tile(np.arange(GRID_W, dtype=np.float32), rows)
    n_freq = HEAD_DIM // 4
    inv_freq = (ROPE_BASE ** (-np.arange(n_freq, dtype=np.float32) / n_freq)).astype(np.float32)
    ang_r = row[:, None] * inv_freq[None, :]
    ang_c = col[:, None] * inv_freq[None, :]
    ang = np.concatenate([ang_r, ang_r, ang_c, ang_c], axis=-1)
    sign = np.tile(np.concatenate([-np.ones(n_freq, np.float32), np.ones(n_freq, np.float32)]), 2)
    reps = LANES // HEAD_DIM
    cos = np.tile(np.cos(ang).astype(np.float32), (1, reps))
    sin = np.tile((np.sin(ang) * sign).astype(np.float32), (1, reps))
    return jnp.asarray(cos), jnp.asarray(sin)


def kernel(x, c, ctx, c_ctx, w_mod, b_mod, norm1_g, w_in, attn_sink, conv_w, w_proj_attn,
           w_proj_conv, w_out, norm2_g, w_router, w_exp_gate, w_exp_up, w_exp_down, final_norm_g):
    assert x.shape == (BATCH, SEQ, D_MODEL) and ctx.shape == (BATCH, CTX_LEN, D_MODEL)
    assert w_mod.shape[0] == 1, "single-layer problem"
    layer = 0

    cvec = jnp.concatenate(
        [c, c_ctx[None, :], jnp.zeros((MOD_ROWS - BATCH - 1, D_MODEL), F32)], axis=0)
    mod = _mod(cvec, w_mod[layer], b_mod[layer][None, :])
    chunk = lambda rows, k: rows[:, k * D_MODEL:(k + 1) * D_MODEL]
    mod_x = mod[:BATCH][:, None, :]
    sh1, sc1, g1, sh2, sc2, g2 = (mod_x[..., k * D_MODEL:(k + 1) * D_MODEL] for k in range(N_MOD))
    mod_c = mod[BATCH:BATCH + 1]
    csh1, csc1 = chunk(mod_c, 0), chunk(mod_c, 1)

    n1 = norm1_g[layer][None, :]
    w_in_b = w_in[layer].astype(BF16)
    k_ctx, vt_ctx = _ctx_kv(ctx, n1, csh1, csc1, w_in_b[:, OFF_K:OFF_U])

    cos, sin = _rope_tables()
    q, k, vt, z, bg, ga, gc = _proj(x, n1, sh1, sc1, w_in_b, cos, sin)
    sink_log2 = attn_sink[layer].reshape(-1) * LOG2E
    o_attn = _attn(sink_log2, q, k, vt, k_ctx, vt_ctx)

    wr_hi, wr_lo = _split_bf16(w_router[layer].T)
    w3 = jnp.concatenate(
        [w_proj_attn[layer], w_proj_conv[layer], w_out[layer],
         jnp.zeros((D_MODEL, W3_WIDTH - 3 * D_MODEL), F32)], axis=1).astype(BF16)
    x1, h2, logits = _merge(
        x, o_attn, z, bg, ga, gc, conv_w[layer], w3,
        g1, sh2, sc2, norm2_g[layer][None, :], jnp.concatenate([wr_hi, wr_lo], axis=0))

    pos, gate, band = _route(logits)
    yet_lo, yet_hi = _ffn(band, pos, h2, w_exp_gate[layer], w_exp_up[layer], w_exp_down[layer])
    return _combine(band, pos, gate, yet_lo, yet_hi, x1, g2, final_norm_g[None, :])
```

```python
import jax
import jax.numpy as jnp
import numpy as np
from jax import lax
from jax.experimental import pallas as pl
from jax.experimental.pallas import tpu as pltpu

D_MODEL = 1024
BATCH = 16
SEQ = 2048
CTX_LEN = 256
GRID_W = 64
N_HEADS = 16
N_KV_HEADS = 4
GROUP = N_HEADS // N_KV_HEADS
HEAD_DIM = D_MODEL // N_HEADS
ATTN_WIDTH = N_HEADS * HEAD_DIM
KV_WIDTH = N_KV_HEADS * HEAD_DIM
WINDOW = 128
Q_BLOCK = 128
ROPE_BASE = 10000.0
CONV_WIDTH = D_MODEL
N_EXPERTS = 16
EXPERT_FF = D_MODEL
CAPACITY_FACTOR = 2
N_MOD = 6
EPS = 1e-6
NEG_INF = -1e30
LOG2E = 1.4426950408889634

CAP = CAPACITY_FACTOR * SEQ // N_EXPERTS
N_QBLK = SEQ // Q_BLOCK
LANES = 128
F32_ROWS = 8
BF16_ROWS = 16
W3_WIDTH = 3 * D_MODEL + LANES
TOK_TILE = 512
PROJ_SUB = 256
ATTN_TILE = 1024
MERGE_TILE = 512
SUB_TILE = 256
CTX_B = 4
ROUTE_B = 4
GATHER_LO_END = 5 * SEQ // 8
GATHER_HI_START = 3 * SEQ // 8
N_SCAT = 8
SCAT_TOK = SEQ // N_SCAT
MOD_ROWS = 32
VMEM_LIMIT = 56 * 1024 * 1024

OFF_Q = 0
OFF_K = OFF_Q + ATTN_WIDTH
OFF_V = OFF_K + KV_WIDTH
OFF_U = OFF_V + KV_WIDTH
OFF_B = OFF_U + CONV_WIDTH
OFF_C = OFF_B + CONV_WIDTH
OFF_GA = OFF_C + CONV_WIDTH
OFF_GC = OFF_GA + D_MODEL
IN_WIDTH = OFF_GC + D_MODEL

F32 = jnp.float32
BF16 = jnp.bfloat16


def _params(*sem):
    return pltpu.CompilerParams(dimension_semantics=sem, vmem_limit_bytes=VMEM_LIMIT)


def _dot(a, b):
    return jnp.dot(a, b, preferred_element_type=F32)


def _dot_nt(a, b):
    return lax.dot_general(a, b, (((1,), (1,)), ((), ())), preferred_element_type=F32)


def _sigmoid(x):
    return 1.0 / (1.0 + jnp.exp(-x))


def _split_bf16(x):
    hi = x.astype(BF16)
    lo = (x - hi.astype(F32)).astype(BF16)
    return hi, lo


def _norm_mod(x, g, shift, scale):
    y = x * lax.rsqrt(jnp.mean(x * x, axis=-1, keepdims=True) + EPS) * g
    return y * (1.0 + scale) + shift


def _mod_kernel(c_ref, w_ref, b_ref, o_ref):
    cv = c_ref[...]
    s_hi, s_lo = _split_bf16(cv * _sigmoid(cv))
    w_hi, w_lo = _split_bf16(w_ref[...])
    o_ref[...] = _dot(s_hi, w_hi) + _dot(s_lo, w_hi) + _dot(s_hi, w_lo) + b_ref[...]


def _mod(cvec, w_mod, b_mod):
    n_out = N_MOD * D_MODEL
    blk = D_MODEL
    return pl.pallas_call(
        _mod_kernel,
        grid=(n_out // blk,),
        in_specs=[
            pl.BlockSpec((MOD_ROWS, D_MODEL), lambda j: (0, 0)),
            pl.BlockSpec((D_MODEL, blk), lambda j: (0, j)),
            pl.BlockSpec((1, blk), lambda j: (0, j)),
        ],
        out_specs=pl.BlockSpec((MOD_ROWS, blk), lambda j: (0, j)),
        out_shape=jax.ShapeDtypeStruct((MOD_ROWS, n_out), F32),
        compiler_params=_params("arbitrary"),
        name="mod",
    )(cvec, w_mod, b_mod)


def _ctx_kernel(ctx_ref, g_ref, sh_ref, sc_ref, w_ref, k_ref, vt_ref):
    x = ctx_ref[...].reshape(CTX_B * CTX_LEN, D_MODEL)
    h = _norm_mod(x, g_ref[...], sh_ref[...], sc_ref[...]).astype(BF16)
    kv = _dot(h, w_ref[...])
    k_ref[...] = kv[:, :KV_WIDTH].astype(BF16).reshape(CTX_B, CTX_LEN, KV_WIDTH)
    for i in range(CTX_B):
        vt_ref[i] = kv[i * CTX_LEN:(i + 1) * CTX_LEN, KV_WIDTH:].T.astype(BF16)


def _ctx_kv(ctx, g1, csh1, csc1, w_kv):
    row = pl.BlockSpec((1, D_MODEL), lambda b: (0, 0))
    return pl.pallas_call(
        _ctx_kernel,
        grid=(BATCH // CTX_B,),
        in_specs=[
            pl.BlockSpec((CTX_B, CTX_LEN, D_MODEL), lambda b: (b, 0, 0)),
            row, row, row,
            pl.BlockSpec((D_MODEL, 2 * KV_WIDTH), lambda b: (0, 0)),
        ],
        out_specs=[pl.BlockSpec((CTX_B, CTX_LEN, KV_WIDTH), lambda b: (b, 0, 0)),
                   pl.BlockSpec((CTX_B, KV_WIDTH, CTX_LEN), lambda b: (b, 0, 0))],
        out_shape=[jax.ShapeDtypeStruct((BATCH, CTX_LEN, KV_WIDTH), BF16),
                   jax.ShapeDtypeStruct((BATCH, KV_WIDTH, CTX_LEN), BF16)],
        compiler_params=_params("arbitrary"),
        name="ctx_kv",
    )(ctx, g1, csh1, csc1, w_kv)


def _proj_kernel(x_ref, g_ref, sh_ref, sc_ref, w_ref, cos_ref, sin_ref,
                 q_ref, k_ref, vt_ref, z_ref, bg_ref, ga_ref, gc_ref):
    lane = lax.broadcasted_iota(jnp.int32, (PROJ_SUB, LANES), 1)
    first_half = (lane & (HEAD_DIM // 4)) == 0
    scale = HEAD_DIM ** -0.5 * LOG2E

    for r in range(TOK_TILE // PROJ_SUB):
        rs = slice(r * PROJ_SUB, (r + 1) * PROJ_SUB)
        h = _norm_mod(x_ref[0, rs, :], g_ref[...], sh_ref[0], sc_ref[0]).astype(BF16)
        cos = cos_ref[rs, :]
        sin = sin_ref[rs, :]

        def rope(a):
            rot = jnp.where(first_half,
                            pltpu.roll(a, LANES - HEAD_DIM // 4, 1),
                            pltpu.roll(a, HEAD_DIM // 4, 1))
            return a * cos + rot * sin

        def proj(off, width):
            return _dot(h, w_ref[:, off:off + width])

        qa = proj(OFF_Q, ATTN_WIDTH)
        for j in range(ATTN_WIDTH // LANES):
            sl = slice(j * LANES, (j + 1) * LANES)
            q_ref[0, rs, sl] = (rope(qa[:, sl]) * scale).astype(BF16)
        ka = proj(OFF_K, KV_WIDTH)
        for j in range(KV_WIDTH // LANES):
            sl = slice(j * LANES, (j + 1) * LANES)
            k_ref[0, rs, sl] = rope(ka[:, sl]).astype(BF16)
        vt_ref[0, :, rs] = proj(OFF_V, KV_WIDTH).T.astype(BF16)
        z_ref[0, rs, :] = (proj(OFF_C, CONV_WIDTH) * proj(OFF_U, CONV_WIDTH)).astype(BF16)
        bg_ref[0, rs, :] = proj(OFF_B, CONV_WIDTH).astype(BF16)
        ga_ref[0, rs, :] = _sigmoid(proj(OFF_GA, D_MODEL)).astype(BF16)
        gc_ref[0, rs, :] = _sigmoid(proj(OFF_GC, D_MODEL)).astype(BF16)


def _proj(x, g1, sh1, sc1, w_in, cos, sin):
    nt = SEQ // TOK_TILE
    tile = lambda w: pl.BlockSpec((1, TOK_TILE, w), lambda b, t: (b, t, 0))
    per_b = pl.BlockSpec((1, 1, D_MODEL), lambda b, t: (b, 0, 0))
    tab = pl.BlockSpec((TOK_TILE, LANES), lambda b, t: (t, 0))
    shp = lambda w: jax.ShapeDtypeStruct((BATCH, SEQ, w), BF16)
    vt_spec = pl.BlockSpec((1, KV_WIDTH, TOK_TILE), lambda b, t: (b, 0, t))
    vt_shape = jax.ShapeDtypeStruct((BATCH, KV_WIDTH, SEQ), BF16)
    return pl.pallas_call(
        _proj_kernel,
        grid=(BATCH, nt),
        in_specs=[
            tile(D_MODEL),
            pl.BlockSpec((1, D_MODEL), lambda b, t: (0, 0)),
            per_b, per_b,
            pl.BlockSpec((D_MODEL, IN_WIDTH), lambda b, t: (0, 0), pipeline_mode=pl.Buffered(1)),
            tab, tab,
        ],
        out_specs=[tile(ATTN_WIDTH), tile(KV_WIDTH), vt_spec, tile(CONV_WIDTH), tile(CONV_WIDTH),
                   tile(D_MODEL), tile(D_MODEL)],
        out_shape=[shp(ATTN_WIDTH), shp(KV_WIDTH), vt_shape, shp(CONV_WIDTH), shp(CONV_WIDTH),
                   shp(D_MODEL), shp(D_MODEL)],
        compiler_params=_params("arbitrary", "arbitrary"),
        name="proj",
    )(x, g1, sh1, sc1, w_in, cos, sin)


def _attn_kernel(sink_ref, q_ref, k_ref, vt_ref, kx_ref, vx_ref, o_ref):
    cols = GROUP * Q_BLOCK
    key = lax.broadcasted_iota(jnp.int32, (Q_BLOCK, cols), 0)
    qry = lax.broadcasted_iota(jnp.int32, (Q_BLOCK, cols), 1) & (Q_BLOCK - 1)
    lane_head = lax.broadcasted_iota(jnp.int32, (1, cols), 1) >> 7
    n_keys = Q_BLOCK + 2 * WINDOW + CTX_LEN
    ones = jnp.ones((BF16_ROWS, n_keys), BF16)

    blocks = []
    for sb in range(ATTN_TILE // Q_BLOCK):
        i = pl.program_id(1) * (ATTN_TILE // Q_BLOCK) + sb
        blocks.append(dict(
            rows=slice(sb * Q_BLOCK, (sb + 1) * Q_BLOCK),
            p0=pl.multiple_of(jnp.maximum(i - 1, 0) * Q_BLOCK, Q_BLOCK),
            c0=pl.multiple_of(i * Q_BLOCK, Q_BLOCK),
            n0=pl.multiple_of(jnp.minimum(i + 1, N_QBLK - 1) * Q_BLOCK, Q_BLOCK),
            ok_prev=key >= qry + jnp.where(i > 0, 0, Q_BLOCK),
            ok_next=key <= qry - jnp.where(i < N_QBLK - 1, 0, Q_BLOCK)))

    def scores(blk, g):
        ks = slice(g * HEAD_DIM, (g + 1) * HEAD_DIM)
        q4 = jnp.concatenate(
            [q_ref[0, blk["rows"], (g * GROUP + j) * HEAD_DIM:(g * GROUP + j + 1) * HEAD_DIM]
             for j in range(GROUP)], axis=0)
        k_all = jnp.concatenate(
            [k_ref[0, pl.ds(blk["p0"], Q_BLOCK), ks], k_ref[0, pl.ds(blk["c0"], Q_BLOCK), ks],
             k_ref[0, pl.ds(blk["n0"], Q_BLOCK), ks], kx_ref[0, :, ks]], axis=0)
        return _dot_nt(k_all, q4)

    def softmax(blk, g, st):
        pieces = [jnp.where(blk["ok_prev"], st[:WINDOW], NEG_INF),
                  st[WINDOW:WINDOW + Q_BLOCK],
                  jnp.where(blk["ok_next"], st[WINDOW + Q_BLOCK:2 * WINDOW + Q_BLOCK], NEG_INF),
                  st[2 * WINDOW + Q_BLOCK:]]
        sink = jnp.zeros((1, cols), F32)
        for j in range(GROUP):
            sink = jnp.where(lane_head == j, sink_ref[g * GROUP + j], sink)
        m = sink
        for piece in pieces:
            m = jnp.maximum(m, jnp.max(piece, axis=0, keepdims=True))
        pt = jnp.concatenate([jnp.exp2((piece - m).astype(BF16)) for piece in pieces], axis=0)
        return pt, jnp.exp2(sink - m)

    def values(blk, g, pt, p_sink):
        ks = slice(g * HEAD_DIM, (g + 1) * HEAD_DIM)
        vt = jnp.concatenate(
            [vt_ref[0, ks, pl.ds(blk["p0"], Q_BLOCK)], vt_ref[0, ks, pl.ds(blk["c0"], Q_BLOCK)],
             vt_ref[0, ks, pl.ds(blk["n0"], Q_BLOCK)], vx_ref[0, ks, :]], axis=1)
        ot = _dot(jnp.concatenate([vt, ones], axis=0), pt)
        denom = ot[HEAD_DIM:HEAD_DIM + 1] + p_sink
        on = ot[:HEAD_DIM] * (1.0 / denom)
        for jj in range(GROUP // 2):
            pair_t = jnp.concatenate(
                [on[:, (2 * jj) * Q_BLOCK:(2 * jj + 1) * Q_BLOCK],
                 on[:, (2 * jj + 1) * Q_BLOCK:(2 * jj + 2) * Q_BLOCK]], axis=0)
            col = (g * GROUP + 2 * jj) * HEAD_DIM
            o_ref[0, blk["rows"], col:col + 2 * HEAD_DIM] = pair_t.T.astype(BF16)

    chains = [(blk, g) for blk in blocks for g in range(N_KV_HEADS)]
    st_next = scores(*chains[0])
    for n, chain in enumerate(chains):
        st = st_next
        if n + 1 < len(chains):
            st_next = scores(*chains[n + 1])
        values(*chain, *softmax(*chain, st))


def _attn(sink, q, k, vt, k_ctx, vt_ctx):
    qspec = pl.BlockSpec((1, ATTN_TILE, ATTN_WIDTH), lambda b, i: (b, i, 0))
    return pl.pallas_call(
        _attn_kernel,
        grid=(BATCH, SEQ // ATTN_TILE),
        in_specs=[pl.BlockSpec(memory_space=pltpu.SMEM), qspec,
                  pl.BlockSpec((1, SEQ, KV_WIDTH), lambda b, i: (b, 0, 0)),
                  pl.BlockSpec((1, KV_WIDTH, SEQ), lambda b, i: (b, 0, 0)),
                  pl.BlockSpec((1, CTX_LEN, KV_WIDTH), lambda b, i: (b, 0, 0)),
                  pl.BlockSpec((1, KV_WIDTH, CTX_LEN), lambda b, i: (b, 0, 0))],
        out_specs=qspec,
        out_shape=jax.ShapeDtypeStruct((BATCH, SEQ, ATTN_WIDTH), BF16),
        compiler_params=_params("arbitrary", "arbitrary"),
        name="attn",
    )(sink, q, k, vt, k_ctx, vt_ctx)


def _merge_kernel(x_ref, oa_ref, z_ref, zp_ref, zn_ref, bg_ref, ga_ref, gc_ref, cw_ref,
                  w3_ref, g1_ref, sh2_ref, sc2_ref, n2_ref, wr_ref,
                  x1_ref, h2_ref, lg_ref):
    t = pl.program_id(1)
    nt = pl.num_programs(1)
    z_before = zp_ref[0, BF16_ROWS - 1:BF16_ROWS, :].astype(F32) * jnp.where(t > 0, 1.0, 0.0)
    z_after = zn_ref[0, 0:1, :].astype(F32) * jnp.where(t < nt - 1, 1.0, 0.0)
    row = lax.broadcasted_iota(jnp.int32, (F32_ROWS, 1), 0)
    n_sub = MERGE_TILE // SUB_TILE
    subs = [slice(r * SUB_TILE, (r + 1) * SUB_TILE) for r in range(n_sub)]

    def conv(r):
        lo, hi = subs[r].start, subs[r].stop
        z = z_ref[0, subs[r], :].astype(F32)
        before = z_before if r == 0 else z_ref[0, lo - 1:lo, :].astype(F32)
        after = z_after if r == n_sub - 1 else z_ref[0, hi:hi + 1, :].astype(F32)
        up = pltpu.roll(z, 1, 0)
        z_prev = jnp.concatenate(
            [jnp.where(row == 0, before, up[:F32_ROWS]), up[F32_ROWS:]], axis=0)
        dn = pltpu.roll(z, SUB_TILE - 1, 0)
        z_next = jnp.concatenate(
            [dn[:-F32_ROWS], jnp.where(row == F32_ROWS - 1, after, dn[-F32_ROWS:])], axis=0)
        y = bg_ref[0, subs[r], :].astype(F32) * (
            cw_ref[0:1, :] * z_prev + cw_ref[1:2, :] * z + cw_ref[2:3, :] * z_next)
        return y.astype(BF16)

    a, s = [], []
    for r in range(n_sub):
        a.append(_dot(oa_ref[0, subs[r], :], w3_ref[:, 0:D_MODEL]))
        s.append(_dot(conv(r), w3_ref[:, D_MODEL:2 * D_MODEL]))
    branch = []
    for r, rs in enumerate(subs):
        merged = ga_ref[0, rs, :].astype(F32) * a[r] + gc_ref[0, rs, :].astype(F32) * s[r]
        branch.append(_dot(merged.astype(BF16), w3_ref[:, 2 * D_MODEL:3 * D_MODEL]))
    for r, rs in enumerate(subs):
        x1 = x_ref[0, rs, :] + g1_ref[0] * branch[r]
        x1_ref[0, rs, :] = x1
        h2 = _norm_mod(x1, n2_ref[...], sh2_ref[0], sc2_ref[0])
        h_hi, h_lo = _split_bf16(h2)
        h2_ref[0, rs, :] = h_hi
        both = _dot_nt(wr_ref[...], h_hi)
        lg_ref[0, :, rs] = (both[:N_EXPERTS] + both[N_EXPERTS:]
                            + _dot_nt(wr_ref[:N_EXPERTS], h_lo))


def _merge(x, oa, z, bg, ga, gc, conv_w, w3, g1, sh2, sc2, n2, wr):
    nt = SEQ // MERGE_TILE
    hb = MERGE_TILE // BF16_ROWS
    n_hb = SEQ // BF16_ROWS
    tile = pl.BlockSpec((1, MERGE_TILE, D_MODEL), lambda b, t: (b, t, 0))
    halo_p = pl.BlockSpec((1, BF16_ROWS, D_MODEL), lambda b, t: (b, jnp.maximum(t * hb - 1, 0), 0))
    halo_n = pl.BlockSpec((1, BF16_ROWS, D_MODEL),
                          lambda b, t: (b, jnp.minimum((t + 1) * hb, n_hb - 1), 0))
    per_b = pl.BlockSpec((1, 1, D_MODEL), lambda b, t: (b, 0, 0))
    full = lambda r, c: pl.BlockSpec((r, c), lambda b, t: (0, 0))
    return pl.pallas_call(
        _merge_kernel,
        grid=(BATCH, nt),
        in_specs=[tile, tile, tile, halo_p, halo_n, tile, tile, tile,
                  full(3, D_MODEL), full(D_MODEL, W3_WIDTH),
                  per_b, per_b, per_b, full(1, D_MODEL), full(2 * N_EXPERTS, D_MODEL)],
        out_specs=[tile, tile, pl.BlockSpec((1, N_EXPERTS, MERGE_TILE), lambda b, t: (b, 0, t))],
        out_shape=[jax.ShapeDtypeStruct((BATCH, SEQ, D_MODEL), F32),
                   jax.ShapeDtypeStruct((BATCH, SEQ, D_MODEL), BF16),
                   jax.ShapeDtypeStruct((BATCH, N_EXPERTS, SEQ), F32)],
        compiler_params=_params("arbitrary", "arbitrary"),
        name="merge",
    )(x, oa, z, z, z, bg, ga, gc, conv_w, w3, g1, sh2, sc2, n2, wr)


def _cumsum_excl(mf, upper):
    nblk = SEQ // LANES
    rows = mf.shape[0]
    stacked = jnp.concatenate([mf[:, j * LANES:(j + 1) * LANES] for j in range(nblk)], axis=0)
    within = _dot(stacked.astype(BF16), upper)
    tot = jnp.sum(stacked, axis=1, keepdims=True)
    off = jnp.zeros((rows, 1), F32)
    out = []
    for j in range(nblk):
        rs = slice(j * rows, (j + 1) * rows)
        out.append(within[rs] + off)
        off = off + tot[rs]
    return jnp.concatenate(out, axis=1)


def _route_kernel(lg_ref, pos_ref, gate_ref, band_ref):
    rows = ROUTE_B * N_EXPERTS
    lg = lg_ref[...]
    ex = jnp.exp(lg - jnp.max(lg, axis=1, keepdims=True))
    aff = (ex / jnp.sum(ex, axis=1, keepdims=True)).reshape(rows, SEQ)

    def body(_, carry):
        lo, hi = carry
        mid = lo + ((hi - lo) >> 1)
        cnt = jnp.sum(jnp.where(aff >= pltpu.bitcast(mid, F32), 1.0, 0.0), axis=1, keepdims=True)
        ge = cnt >= CAP
        return jnp.where(ge, mid, lo), jnp.where(ge, hi, mid)

    one_bits = 0x3F800000
    lo0 = jnp.zeros((rows, 1), jnp.int32)
    hi0 = jnp.full((rows, 1), one_bits + 1, jnp.int32)
    lo, _ = lax.fori_loop(0, 31, body, (lo0, hi0))
    th = pltpu.bitcast(lo, F32)

    ku = lax.broadcasted_iota(jnp.int32, (LANES, LANES), 0)
    nu = lax.broadcasted_iota(jnp.int32, (LANES, LANES), 1)
    upper = jnp.where(ku < nu, 1.0, 0.0).astype(BF16)
    gtf = jnp.where(aff > th, 1.0, 0.0)
    eqf = jnp.where(aff == th, 1.0, 0.0)
    need = CAP - jnp.sum(gtf, axis=1, keepdims=True)
    sel = gtf + eqf * jnp.where(_cumsum_excl(eqf, upper) < need, 1.0, 0.0)
    pos = _cumsum_excl(sel, upper)
    pos_ref[...] = jnp.where(sel > 0.0, pos, -1.0).reshape(ROUTE_B, N_EXPERTS, SEQ)
    gate_ref[...] = jnp.where(sel > 0.0, aff, 0.0).reshape(ROUTE_B, N_EXPERTS, SEQ)
    count = pos + sel
    tok_last_lo = jnp.sum(jnp.where(count <= CAP // 2 - 1, 1.0, 0.0), axis=1, keepdims=True)
    tok_first_hi = jnp.sum(jnp.where(count <= CAP // 2, 1.0, 0.0), axis=1, keepdims=True)
    banded = jnp.logical_and(tok_last_lo < GATHER_LO_END, tok_first_hi >= GATHER_HI_START)
    band_ref[...] = jnp.where(banded, 1, 0).astype(jnp.int32).reshape(ROUTE_B, N_EXPERTS, 1)


def _route(logits):
    spec = pl.BlockSpec((ROUTE_B, N_EXPERTS, SEQ), lambda b: (b, 0, 0))
    shape = jax.ShapeDtypeStruct((BATCH, N_EXPERTS, SEQ), F32)
    return pl.pallas_call(
        _route_kernel,
        grid=(BATCH // ROUTE_B,),
        in_specs=[spec],
        out_specs=[spec, spec, pl.BlockSpec((ROUTE_B, N_EXPERTS, 1), lambda b: (b, 0, 0))],
        out_shape=[shape, shape, jax.ShapeDtypeStruct((BATCH, N_EXPERTS, 1), jnp.int32)],
        compiler_params=_params("arbitrary"),
        name="route",
    )(logits)


def _ffn_kernel(band_ref, pos_ref, h_ref, wg_ref, wu_ref, wd_ref, yet_ref,
                wg_b, wu_b, wdt_b, dense_ref):
    e, b = pl.program_id(0), pl.program_id(1)

    @pl.when(b == 0)
    def _():
        wg_b[...] = wg_ref[0].astype(BF16)
        wu_b[...] = wu_ref[0].astype(BF16)
        wdt_b[...] = wd_ref[0].T.astype(BF16)

    def gather(slots, toks):
        n_slots, n_toks = slots.stop - slots.start, toks.stop - toks.start
        slot = (lax.broadcasted_iota(jnp.int32, (n_slots, n_toks), 0) + slots.start).astype(F32)
        onehot = jnp.where(slot == pos_ref[0, 0, :, toks], 1.0, 0.0).astype(BF16)
        return _dot(onehot, h_ref[0, toks, :]).astype(BF16)

    banded = band_ref[b * N_EXPERTS + e] == 1

    @pl.when(jnp.logical_and(e == 0, b == 0))
    def _():
        dense_ref[...] = jnp.zeros_like(dense_ref)

    @pl.when(jnp.logical_not(banded))
    def _():
        dense_ref[...] = gather(slice(0, CAP), slice(0, SEQ))

    windowed = jnp.concatenate([gather(slice(0, CAP // 2), slice(0, GATHER_LO_END)),
                                gather(slice(CAP // 2, CAP), slice(GATHER_HI_START, SEQ))], axis=0)
    xe = jnp.where(banded, windowed, dense_ref[...])
    act = _dot(xe, wg_b[...])
    up = _dot(xe, wu_b[...])
    hid = (act * _sigmoid(act) * up).astype(BF16)
    ye_t = _dot_nt(wdt_b[...], hid)
    yet_ref[0, 0] = ye_t[:, :CAP // 2].astype(BF16)
    yet_ref[0, 1] = ye_t[:, CAP // 2:].astype(BF16)


def _ffn(band, pos, h2, wg, wu, wd):
    def wspec(lead):
        return pl.BlockSpec(
            (1, D_MODEL, EXPERT_FF),
            lambda e, b: (jnp.minimum(e + jnp.where(b >= BATCH - lead, 1, 0), N_EXPERTS - 1),
                          0, 0))
    wscr = pltpu.VMEM((D_MODEL, EXPERT_FF), BF16)
    return pl.pallas_call(
        _ffn_kernel,
        grid=(N_EXPERTS, BATCH),
        in_specs=[pl.BlockSpec(memory_space=pltpu.SMEM),
                  pl.BlockSpec((1, 1, 1, SEQ), lambda e, b: (b, e, 0, 0)),
                  pl.BlockSpec((1, SEQ, D_MODEL), lambda e, b: (b, 0, 0)),
                  wspec(3), wspec(2), wspec(1)],
        out_specs=pl.BlockSpec((1, 2, D_MODEL, CAP // 2), lambda e, b: (b, 0, 0, e)),
        out_shape=jax.ShapeDtypeStruct((BATCH, 2, D_MODEL, N_EXPERTS * CAP // 2), BF16),
        scratch_shapes=[wscr, wscr, wscr, pltpu.VMEM((CAP, D_MODEL), BF16)],
        compiler_params=_params("arbitrary", "arbitrary"),
        name="ffn",
    )(band.reshape(-1), pos[:, :, None, :], h2, wg, wu, wd)


def _combine_kernel(band_ref, pos_ref, gate_ref, yet_ref, x1_ref, g2_ref, fg_ref, o_ref,
                    other_ref):
    b, c = pl.program_id(0), pl.program_id(1)
    n_banded = band_ref[b * N_EXPERTS]
    for ee in range(1, N_EXPERTS):
        n_banded = n_banded + band_ref[b * N_EXPERTS + ee]
    all_banded = n_banded == N_EXPERTS
    lo_only = jnp.logical_and(all_banded, (c + 1) * SCAT_TOK <= GATHER_HI_START)
    hi_only = jnp.logical_and(all_banded, c * SCAT_TOK >= GATHER_LO_END)
    both = jnp.logical_not(jnp.logical_or(lo_only, hi_only))

    def half(plane):
        slot = (lax.broadcasted_iota(jnp.int32, (CAP // 2, SCAT_TOK), 0)
                + plane * (CAP // 2)).astype(F32)
        scat = jnp.concatenate(
            [jnp.where(slot == pos_ref[0, ee:ee + 1, :], gate_ref[0, ee:ee + 1, :], 0.0)
             .astype(BF16) for ee in range(N_EXPERTS)], axis=0)
        return _dot(yet_ref[0, plane], scat)

    @pl.when(jnp.logical_and(b == 0, c == 0))
    def _():
        other_ref[...] = jnp.zeros_like(other_ref)

    @pl.when(both)
    def _():
        other_ref[...] = half(1)

    moe_t = half(jnp.where(hi_only, 1, 0)) + jnp.where(both, other_ref[...], 0.0)
    x2 = x1_ref[0] + g2_ref[0] * moe_t.T
    o_ref[0] = x2 * lax.rsqrt(jnp.mean(x2 * x2, axis=-1, keepdims=True) + EPS) * fg_ref[...]


def _combine(band, pos, gate, yet, x1, g2, fg):
    sel_c = pl.BlockSpec((1, N_EXPERTS, SCAT_TOK), lambda b, c: (b, 0, c))
    tok_c = pl.BlockSpec((1, SCAT_TOK, D_MODEL), lambda b, c: (b, c, 0))
    yet_spec = pl.BlockSpec((1, 2, D_MODEL, N_EXPERTS * CAP // 2), lambda b, c: (b, 0, 0, 0))
    return pl.pallas_call(
        _combine_kernel,
        grid=(BATCH, N_SCAT),
        in_specs=[pl.BlockSpec(memory_space=pltpu.SMEM), sel_c, sel_c, yet_spec,
                  tok_c,
                  pl.BlockSpec((1, 1, D_MODEL), lambda b, c: (b, 0, 0)),
                  pl.BlockSpec((1, D_MODEL), lambda b, c: (0, 0))],
        out_specs=tok_c,
        out_shape=jax.ShapeDtypeStruct((BATCH, SEQ, D_MODEL), F32),
        scratch_shapes=[pltpu.VMEM((D_MODEL, SCAT_TOK), F32)],
        compiler_params=_params("arbitrary", "arbitrary"),
        name="combine",
    )(band.reshape(-1), pos, gate, yet, x1, g2, fg)


def _rope_tables():
    rows = SEQ // GRID_W
    row = np.repeat(np.arange(rows, dtype=np.float32), GRID_W)
    col = np.tile(np.arange(GRID_W, dtype=np.float32), rows)
    n_freq = HEAD_DIM // 4
    inv_freq = (ROPE_BASE ** (-np.arange(n_freq, dtype=np.float32) / n_freq)).astype(np.float32)
    ang_r = row[:, None] * inv_freq[None, :]
    ang_c = col[:, None] * inv_freq[None, :]
    ang = np.concatenate([ang_r, ang_r, ang_c, ang_c], axis=-1)
    sign = np.tile(np.concatenate([-np.ones(n_freq, np.float32), np.ones(n_freq, np.float32)]), 2)
    reps = LANES // HEAD_DIM
    cos = np.tile(np.cos(ang).astype(np.float32), (1, reps))
    sin = np.tile((np.sin(ang) * sign).astype(np.float32), (1, reps))
    return jnp.asarray(cos), jnp.asarray(sin)


def kernel(x, c, ctx, c_ctx, w_mod, b_mod, norm1_g, w_in, attn_sink, conv_w, w_proj_attn,
           w_proj_conv, w_out, norm2_g, w_router, w_exp_gate, w_exp_up, w_exp_down, final_norm_g):
    assert x.shape == (BATCH, SEQ, D_MODEL) and ctx.shape == (BATCH, CTX_LEN, D_MODEL)
    assert w_mod.shape[0] == 1, "single-layer problem"
    layer = 0

    cvec = jnp.concatenate(
        [c, c_ctx[None, :], jnp.zeros((MOD_ROWS - BATCH - 1, D_MODEL), F32)], axis=0)
    mod = _mod(cvec, w_mod[layer], b_mod[layer][None, :])
    chunk = lambda rows, k: rows[:, k * D_MODEL:(k + 1) * D_MODEL]
    mod_x = mod[:BATCH][:, None, :]
    sh1, sc1, g1, sh2, sc2, g2 = (mod_x[..., k * D_MODEL:(k + 1) * D_MODEL] for k in range(N_MOD))
    mod_c = mod[BATCH:BATCH + 1]
    csh1, csc1 = chunk(mod_c, 0), chunk(mod_c, 1)

    n1 = norm1_g[layer][None, :]
    w_in_b = w_in[layer].astype(BF16)
    k_ctx, vt_ctx = _ctx_kv(ctx, n1, csh1, csc1, w_in_b[:, OFF_K:OFF_U])

    cos, sin = _rope_tables()
    q, k, vt, z, bg, ga, gc = _proj(x, n1, sh1, sc1, w_in_b, cos, sin)
    sink_log2 = attn_sink[layer].reshape(-1) * LOG2E
    o_attn = _attn(sink_log2, q, k, vt, k_ctx, vt_ctx)

    wr_hi, wr_lo = _split_bf16(w_router[layer].T)
    w3 = jnp.concatenate(
        [w_proj_attn[layer], w_proj_conv[layer], w_out[layer],
         jnp.zeros((D_MODEL, W3_WIDTH - 3 * D_MODEL), F32)], axis=1).astype(BF16)
    x1, h2, logits = _merge(
        x, o_attn, z, bg, ga, gc, conv_w[layer], w3,
        g1, sh2, sc2, norm2_g[layer][None, :], jnp.concatenate([wr_hi, wr_lo], axis=0))

    pos, gate, band = _route(logits)
    yet = _ffn(band, pos, h2, w_exp_gate[layer], w_exp_up[layer], w_exp_down[layer])
    return _combine(band, pos, gate, yet, x1, g2, final_norm_g[None, :])
```

```python
import jax
import jax.numpy as jnp
import numpy as np
from jax import lax
from jax.experimental import pallas as pl
from jax.experimental.pallas import tpu as pltpu

D_MODEL = 1024
BATCH = 16
SEQ = 2048
CTX_LEN = 256
GRID_W = 64
N_HEADS = 16
N_KV_HEADS = 4
GROUP = N_HEADS // N_KV_HEADS
HEAD_DIM = D_MODEL // N_HEADS
ATTN_WIDTH = N_HEADS * HEAD_DIM
KV_WIDTH = N_KV_HEADS * HEAD_DIM
WINDOW = 128
Q_BLOCK = 128
ROPE_BASE = 10000.0
CONV_WIDTH = D_MODEL
N_EXPERTS = 16
EXPERT_FF = D_MODEL
CAPACITY_FACTOR = 2
N_MOD = 6
EPS = 1e-6
NEG_INF = -1e30
LOG2E = 1.4426950408889634

CAP = CAPACITY_FACTOR * SEQ // N_EXPERTS
N_QBLK = SEQ // Q_BLOCK
LANES = 128
F32_ROWS = 8
BF16_ROWS = 16
W3_WIDTH = 3 * D_MODEL + LANES
TOK_TILE = 512
PROJ_SUB = 256
ATTN_TILE = 1024
MERGE_TILE = 512
SUB_TILE = 256
CTX_B = 4
ROUTE_B = 4
GATHER_LO_END = 5 * SEQ // 8
GATHER_HI_START = 3 * SEQ // 8
N_PLANES = 3
N_SCAT = 8
SCAT_TOK = SEQ // N_SCAT
MOD_ROWS = 32
VMEM_LIMIT = 56 * 1024 * 1024

OFF_Q = 0
OFF_K = OFF_Q + ATTN_WIDTH
OFF_V = OFF_K + KV_WIDTH
OFF_U = OFF_V + KV_WIDTH
OFF_B = OFF_U + CONV_WIDTH
OFF_C = OFF_B + CONV_WIDTH
OFF_GA = OFF_C + CONV_WIDTH
OFF_GC = OFF_GA + D_MODEL
IN_WIDTH = OFF_GC + D_MODEL

F32 = jnp.float32
BF16 = jnp.bfloat16


def _params(*sem):
    return pltpu.CompilerParams(dimension_semantics=sem, vmem_limit_bytes=VMEM_LIMIT)


def _dot(a, b):
    return jnp.dot(a, b, preferred_element_type=F32)


def _dot_nt(a, b):
    return lax.dot_general(a, b, (((1,), (1,)), ((), ())), preferred_element_type=F32)


def _sigmoid(x):
    return 1.0 / (1.0 + jnp.exp(-x))


def _split_bf16(x):
    hi = x.astype(BF16)
    lo = (x - hi.astype(F32)).astype(BF16)
    return hi, lo


def _norm_mod(x, g, shift, scale):
    y = x * lax.rsqrt(jnp.mean(x * x, axis=-1, keepdims=True) + EPS) * g
    return y * (1.0 + scale) + shift


def _mod_kernel(c_ref, w_ref, b_ref, o_ref):
    cv = c_ref[...]
    s_hi, s_lo = _split_bf16(cv * _sigmoid(cv))
    w_hi, w_lo = _split_bf16(w_ref[...])
    o_ref[...] = _dot(s_hi, w_hi) + _dot(s_lo, w_hi) + _dot(s_hi, w_lo) + b_ref[...]


def _mod(cvec, w_mod, b_mod):
    n_out = N_MOD * D_MODEL
    blk = D_MODEL
    return pl.pallas_call(
        _mod_kernel,
        grid=(n_out // blk,),
        in_specs=[
            pl.BlockSpec((MOD_ROWS, D_MODEL), lambda j: (0, 0)),
            pl.BlockSpec((D_MODEL, blk), lambda j: (0, j)),
            pl.BlockSpec((1, blk), lambda j: (0, j)),
        ],
        out_specs=pl.BlockSpec((MOD_ROWS, blk), lambda j: (0, j)),
        out_shape=jax.ShapeDtypeStruct((MOD_ROWS, n_out), F32),
        compiler_params=_params("arbitrary"),
        name="mod",
    )(cvec, w_mod, b_mod)


def _ctx_kernel(ctx_ref, g_ref, sh_ref, sc_ref, w_ref, k_ref, vt_ref):
    x = ctx_ref[...].reshape(CTX_B * CTX_LEN, D_MODEL)
    h = _norm_mod(x, g_ref[...], sh_ref[...], sc_ref[...]).astype(BF16)
    kv = _dot(h, w_ref[...])
    k_ref[...] = kv[:, :KV_WIDTH].astype(BF16).reshape(CTX_B, CTX_LEN, KV_WIDTH)
    for i in range(CTX_B):
        vt_ref[i] = kv[i * CTX_LEN:(i + 1) * CTX_LEN, KV_WIDTH:].T.astype(BF16)


def _ctx_kv(ctx, g1, csh1, csc1, w_kv):
    row = pl.BlockSpec((1, D_MODEL), lambda b: (0, 0))
    return pl.pallas_call(
        _ctx_kernel,
        grid=(BATCH // CTX_B,),
        in_specs=[
            pl.BlockSpec((CTX_B, CTX_LEN, D_MODEL), lambda b: (b, 0, 0)),
            row, row, row,
            pl.BlockSpec((D_MODEL, 2 * KV_WIDTH), lambda b: (0, 0)),
        ],
        out_specs=[pl.BlockSpec((CTX_B, CTX_LEN, KV_WIDTH), lambda b: (b, 0, 0)),
                   pl.BlockSpec((CTX_B, KV_WIDTH, CTX_LEN), lambda b: (b, 0, 0))],
        out_shape=[jax.ShapeDtypeStruct((BATCH, CTX_LEN, KV_WIDTH), BF16),
                   jax.ShapeDtypeStruct((BATCH, KV_WIDTH, CTX_LEN), BF16)],
        compiler_params=_params("arbitrary"),
        name="ctx_kv",
    )(ctx, g1, csh1, csc1, w_kv)


def _proj_kernel(x_ref, g_ref, sh_ref, sc_ref, w_ref, cos_ref, sin_ref,
                 q_ref, k_ref, vt_ref, z_ref, bg_ref, ga_ref, gc_ref):
    lane = lax.broadcasted_iota(jnp.int32, (PROJ_SUB, LANES), 1)
    first_half = (lane & (HEAD_DIM // 4)) == 0
    scale = HEAD_DIM ** -0.5 * LOG2E

    for r in range(TOK_TILE // PROJ_SUB):
        rs = slice(r * PROJ_SUB, (r + 1) * PROJ_SUB)
        h = _norm_mod(x_ref[0, rs, :], g_ref[...], sh_ref[0], sc_ref[0]).astype(BF16)
        cos = cos_ref[rs, :]
        sin = sin_ref[rs, :]

        def rope(a):
            rot = jnp.where(first_half,
                            pltpu.roll(a, LANES - HEAD_DIM // 4, 1),
                            pltpu.roll(a, HEAD_DIM // 4, 1))
            return a * cos + rot * sin

        def proj(off, width):
            return _dot(h, w_ref[:, off:off + width])

        qa = proj(OFF_Q, ATTN_WIDTH)
        for j in range(ATTN_WIDTH // LANES):
            sl = slice(j * LANES, (j + 1) * LANES)
            q_ref[0, rs, sl] = (rope(qa[:, sl]) * scale).astype(BF16)
        ka = proj(OFF_K, KV_WIDTH)
        for j in range(KV_WIDTH // LANES):
            sl = slice(j * LANES, (j + 1) * LANES)
            k_ref[0, rs, sl] = rope(ka[:, sl]).astype(BF16)
        vt_ref[0, :, rs] = proj(OFF_V, KV_WIDTH).T.astype(BF16)
        z_ref[0, rs, :] = (proj(OFF_C, CONV_WIDTH) * proj(OFF_U, CONV_WIDTH)).astype(BF16)
        bg_ref[0, rs, :] = proj(OFF_B, CONV_WIDTH).astype(BF16)
        ga_ref[0, rs, :] = _sigmoid(proj(OFF_GA, D_MODEL)).astype(BF16)
        gc_ref[0, rs, :] = _sigmoid(proj(OFF_GC, D_MODEL)).astype(BF16)


def _proj(x, g1, sh1, sc1, w_in, cos, sin):
    nt = SEQ // TOK_TILE
    tile = lambda w: pl.BlockSpec((1, TOK_TILE, w), lambda b, t: (b, t, 0))
    per_b = pl.BlockSpec((1, 1, D_MODEL), lambda b, t: (b, 0, 0))
    tab = pl.BlockSpec((TOK_TILE, LANES), lambda b, t: (t, 0))
    shp = lambda w: jax.ShapeDtypeStruct((BATCH, SEQ, w), BF16)
    vt_spec = pl.BlockSpec((1, KV_WIDTH, TOK_TILE), lambda b, t: (b, 0, t))
    vt_shape = jax.ShapeDtypeStruct((BATCH, KV_WIDTH, SEQ), BF16)
    return pl.pallas_call(
        _proj_kernel,
        grid=(BATCH, nt),
        in_specs=[
            tile(D_MODEL),
            pl.BlockSpec((1, D_MODEL), lambda b, t: (0, 0)),
            per_b, per_b,
            pl.BlockSpec((D_MODEL, IN_WIDTH), lambda b, t: (0, 0), pipeline_mode=pl.Buffered(1)),
            tab, tab,
        ],
        out_specs=[tile(ATTN_WIDTH), tile(KV_WIDTH), vt_spec, tile(CONV_WIDTH), tile(CONV_WIDTH),
                   tile(D_MODEL), tile(D_MODEL)],
        out_shape=[shp(ATTN_WIDTH), shp(KV_WIDTH), vt_shape, shp(CONV_WIDTH), shp(CONV_WIDTH),
                   shp(D_MODEL), shp(D_MODEL)],
        compiler_params=_params("arbitrary", "arbitrary"),
        name="proj",
    )(x, g1, sh1, sc1, w_in, cos, sin)


def _attn_kernel(sink_ref, q_ref, k_ref, vt_ref, kx_ref, vx_ref, o_ref):
    cols = GROUP * Q_BLOCK
    key = lax.broadcasted_iota(jnp.int32, (Q_BLOCK, cols), 0)
    qry = lax.broadcasted_iota(jnp.int32, (Q_BLOCK, cols), 1) & (Q_BLOCK - 1)
    lane_head = lax.broadcasted_iota(jnp.int32, (1, cols), 1) >> 7
    n_keys = Q_BLOCK + 2 * WINDOW + CTX_LEN
    ones = jnp.ones((BF16_ROWS, n_keys), BF16)

    blocks = []
    for sb in range(ATTN_TILE // Q_BLOCK):
        i = pl.program_id(1) * (ATTN_TILE // Q_BLOCK) + sb
        blocks.append(dict(
            rows=slice(sb * Q_BLOCK, (sb + 1) * Q_BLOCK),
            p0=pl.multiple_of(jnp.maximum(i - 1, 0) * Q_BLOCK, Q_BLOCK),
            c0=pl.multiple_of(i * Q_BLOCK, Q_BLOCK),
            n0=pl.multiple_of(jnp.minimum(i + 1, N_QBLK - 1) * Q_BLOCK, Q_BLOCK),
            ok_prev=key >= qry + jnp.where(i > 0, 0, Q_BLOCK),
            ok_next=key <= qry - jnp.where(i < N_QBLK - 1, 0, Q_BLOCK)))

    def scores(blk, g):
        ks = slice(g * HEAD_DIM, (g + 1) * HEAD_DIM)
        q4 = jnp.concatenate(
            [q_ref[0, blk["rows"], (g * GROUP + j) * HEAD_DIM:(g * GROUP + j + 1) * HEAD_DIM]
             for j in range(GROUP)], axis=0)
        k_all = jnp.concatenate(
            [k_ref[0, pl.ds(blk["p0"], Q_BLOCK), ks], k_ref[0, pl.ds(blk["c0"], Q_BLOCK), ks],
             k_ref[0, pl.ds(blk["n0"], Q_BLOCK), ks], kx_ref[0, :, ks]], axis=0)
        return _dot_nt(k_all, q4)

    def softmax(blk, g, st):
        pieces = [jnp.where(blk["ok_prev"], st[:WINDOW], NEG_INF),
                  st[WINDOW:WINDOW + Q_BLOCK],
                  jnp.where(blk["ok_next"], st[WINDOW + Q_BLOCK:2 * WINDOW + Q_BLOCK], NEG_INF),
                  st[2 * WINDOW + Q_BLOCK:]]
        sink = jnp.zeros((1, cols), F32)
        for j in range(GROUP):
            sink = jnp.where(lane_head == j, sink_ref[g * GROUP + j], sink)
        m = sink
        for piece in pieces:
            m = jnp.maximum(m, jnp.max(piece, axis=0, keepdims=True))
        pt = jnp.concatenate([jnp.exp2((piece - m).astype(BF16)) for piece in pieces], axis=0)
        return pt, jnp.exp2(sink - m)

    def values(blk, g, pt, p_sink):
        ks = slice(g * HEAD_DIM, (g + 1) * HEAD_DIM)
        vt = jnp.concatenate(
            [vt_ref[0, ks, pl.ds(blk["p0"], Q_BLOCK)], vt_ref[0, ks, pl.ds(blk["c0"], Q_BLOCK)],
             vt_ref[0, ks, pl.ds(blk["n0"], Q_BLOCK)], vx_ref[0, ks, :]], axis=1)
        ot = _dot(jnp.concatenate([vt, ones], axis=0), pt)
        denom = ot[HEAD_DIM:HEAD_DIM + 1] + p_sink
        on = ot[:HEAD_DIM] * (1.0 / denom)
        for jj in range(GROUP // 2):
            pair_t = jnp.concatenate(
                [on[:, (2 * jj) * Q_BLOCK:(2 * jj + 1) * Q_BLOCK],
                 on[:, (2 * jj + 1) * Q_BLOCK:(2 * jj + 2) * Q_BLOCK]], axis=0)
            col = (g * GROUP + 2 * jj) * HEAD_DIM
            o_ref[0, blk["rows"], col:col + 2 * HEAD_DIM] = pair_t.T.astype(BF16)

    chains = [(blk, g) for blk in blocks for g in range(N_KV_HEADS)]
    st_next = scores(*chains[0])
    for n, chain in enumerate(chains):
        st = st_next
        if n + 1 < len(chains):
            st_next = scores(*chains[n + 1])
        values(*chain, *softmax(*chain, st))


def _attn(sink, q, k, vt, k_ctx, vt_ctx):
    qspec = pl.BlockSpec((1, ATTN_TILE, ATTN_WIDTH), lambda b, i: (b, i, 0))
    return pl.pallas_call(
        _attn_kernel,
        grid=(BATCH, SEQ // ATTN_TILE),
        in_specs=[pl.BlockSpec(memory_space=pltpu.SMEM), qspec,
                  pl.BlockSpec((1, SEQ, KV_WIDTH), lambda b, i: (b, 0, 0)),
                  pl.BlockSpec((1, KV_WIDTH, SEQ), lambda b, i: (b, 0, 0)),
                  pl.BlockSpec((1, CTX_LEN, KV_WIDTH), lambda b, i: (b, 0, 0)),
                  pl.BlockSpec((1, KV_WIDTH, CTX_LEN), lambda b, i: (b, 0, 0))],
        out_specs=qspec,
        out_shape=jax.ShapeDtypeStruct((BATCH, SEQ, ATTN_WIDTH), BF16),
        compiler_params=_params("arbitrary", "arbitrary"),
        name="attn",
    )(sink, q, k, vt, k_ctx, vt_ctx)


def _merge_kernel(x_ref, oa_ref, z_ref, zp_ref, zn_ref, bg_ref, ga_ref, gc_ref, cw_ref,
                  w3_ref, g1_ref, sh2_ref, sc2_ref, n2_ref, wr_ref,
                  x1_ref, h2_ref, lg_ref):
    t = pl.program_id(1)
    nt = pl.num_programs(1)
    z_before = zp_ref[0, BF16_ROWS - 1:BF16_ROWS, :].astype(F32) * jnp.where(t > 0, 1.0, 0.0)
    z_after = zn_ref[0, 0:1, :].astype(F32) * jnp.where(t < nt - 1, 1.0, 0.0)
    row = lax.broadcasted_iota(jnp.int32, (F32_ROWS, 1), 0)
    n_sub = MERGE_TILE // SUB_TILE
    subs = [slice(r * SUB_TILE, (r + 1) * SUB_TILE) for r in range(n_sub)]

    def conv(r):
        lo, hi = subs[r].start, subs[r].stop
        z = z_ref[0, subs[r], :].astype(F32)
        before = z_before if r == 0 else z_ref[0, lo - 1:lo, :].astype(F32)
        after = z_after if r == n_sub - 1 else z_ref[0, hi:hi + 1, :].astype(F32)
        up = pltpu.roll(z, 1, 0)
        z_prev = jnp.concatenate(
            [jnp.where(row == 0, before, up[:F32_ROWS]), up[F32_ROWS:]], axis=0)
        dn = pltpu.roll(z, SUB_TILE - 1, 0)
        z_next = jnp.concatenate(
            [dn[:-F32_ROWS], jnp.where(row == F32_ROWS - 1, after, dn[-F32_ROWS:])], axis=0)
        y = bg_ref[0, subs[r], :].astype(F32) * (
            cw_ref[0:1, :] * z_prev + cw_ref[1:2, :] * z + cw_ref[2:3, :] * z_next)
        return y.astype(BF16)

    a, s = [], []
    for r in range(n_sub):
        a.append(_dot(oa_ref[0, subs[r], :], w3_ref[:, 0:D_MODEL]))
        s.append(_dot(conv(r), w3_ref[:, D_MODEL:2 * D_MODEL]))
    branch = []
    for r, rs in enumerate(subs):
        merged = ga_ref[0, rs, :].astype(F32) * a[r] + gc_ref[0, rs, :].astype(F32) * s[r]
        branch.append(_dot(merged.astype(BF16), w3_ref[:, 2 * D_MODEL:3 * D_MODEL]))
    for r, rs in enumerate(subs):
        x1 = x_ref[0, rs, :] + g1_ref[0] * branch[r]
        x1_ref[0, rs, :] = x1
        h2 = _norm_mod(x1, n2_ref[...], sh2_ref[0], sc2_ref[0])
        h_hi, h_lo = _split_bf16(h2)
        h2_ref[0, rs, :] = h_hi
        both = _dot_nt(wr_ref[...], h_hi)
        lg_ref[0, :, rs] = (both[:N_EXPERTS] + both[N_EXPERTS:]
                            + _dot_nt(wr_ref[:N_EXPERTS], h_lo))


def _merge(x, oa, z, bg, ga, gc, conv_w, w3, g1, sh2, sc2, n2, wr):
    nt = SEQ // MERGE_TILE
    hb = MERGE_TILE // BF16_ROWS
    n_hb = SEQ // BF16_ROWS
    tile = pl.BlockSpec((1, MERGE_TILE, D_MODEL), lambda b, t: (b, t, 0))
    halo_p = pl.BlockSpec((1, BF16_ROWS, D_MODEL), lambda b, t: (b, jnp.maximum(t * hb - 1, 0), 0))
    halo_n = pl.BlockSpec((1, BF16_ROWS, D_MODEL),
                          lambda b, t: (b, jnp.minimum((t + 1) * hb, n_hb - 1), 0))
    per_b = pl.BlockSpec((1, 1, D_MODEL), lambda b, t: (b, 0, 0))
    full = lambda r, c: pl.BlockSpec((r, c), lambda b, t: (0, 0))
    return pl.pallas_call(
        _merge_kernel,
        grid=(BATCH, nt),
        in_specs=[tile, tile, tile, halo_p, halo_n, tile, tile, tile,
                  full(3, D_MODEL), full(D_MODEL, W3_WIDTH),
                  per_b, per_b, per_b, full(1, D_MODEL), full(2 * N_EXPERTS, D_MODEL)],
        out_specs=[tile, tile, pl.BlockSpec((1, N_EXPERTS, MERGE_TILE), lambda b, t: (b, 0, t))],
        out_shape=[jax.ShapeDtypeStruct((BATCH, SEQ, D_MODEL), F32),
                   jax.ShapeDtypeStruct((BATCH, SEQ, D_MODEL), BF16),
                   jax.ShapeDtypeStruct((BATCH, N_EXPERTS, SEQ), F32)],
        compiler_params=_params("arbitrary", "arbitrary"),
        name="merge",
    )(x, oa, z, z, z, bg, ga, gc, conv_w, w3, g1, sh2, sc2, n2, wr)


def _cumsum_excl(mf, upper):
    nblk = SEQ // LANES
    rows = mf.shape[0]
    stacked = jnp.concatenate([mf[:, j * LANES:(j + 1) * LANES] for j in range(nblk)], axis=0)
    within = _dot(stacked.astype(BF16), upper)
    tot = jnp.sum(stacked, axis=1, keepdims=True)
    off = jnp.zeros((rows, 1), F32)
    out = []
    for j in range(nblk):
        rs = slice(j * rows, (j + 1) * rows)
        out.append(within[rs] + off)
        off = off + tot[rs]
    return jnp.concatenate(out, axis=1)


def _route_kernel(lg_ref, pos_ref, gate_ref, band_ref):
    rows = ROUTE_B * N_EXPERTS
    lg = lg_ref[...]
    ex = jnp.exp(lg - jnp.max(lg, axis=1, keepdims=True))
    aff = (ex / jnp.sum(ex, axis=1, keepdims=True)).reshape(rows, SEQ)

    def body(_, carry):
        lo, hi = carry
        mid = lo + ((hi - lo) >> 1)
        cnt = jnp.sum(jnp.where(aff >= pltpu.bitcast(mid, F32), 1.0, 0.0), axis=1, keepdims=True)
        ge = cnt >= CAP
        return jnp.where(ge, mid, lo), jnp.where(ge, hi, mid)

    one_bits = 0x3F800000
    lo0 = jnp.zeros((rows, 1), jnp.int32)
    hi0 = jnp.full((rows, 1), one_bits + 1, jnp.int32)
    lo, _ = lax.fori_loop(0, 31, body, (lo0, hi0))
    th = pltpu.bitcast(lo, F32)

    ku = lax.broadcasted_iota(jnp.int32, (LANES, LANES), 0)
    nu = lax.broadcasted_iota(jnp.int32, (LANES, LANES), 1)
    upper = jnp.where(ku < nu, 1.0, 0.0).astype(BF16)
    gtf = jnp.where(aff > th, 1.0, 0.0)
    eqf = jnp.where(aff == th, 1.0, 0.0)
    need = CAP - jnp.sum(gtf, axis=1, keepdims=True)
    sel = gtf + eqf * jnp.where(_cumsum_excl(eqf, upper) < need, 1.0, 0.0)
    pos = _cumsum_excl(sel, upper)
    pos_ref[...] = jnp.where(sel > 0.0, pos, -1.0).reshape(ROUTE_B, N_EXPERTS, SEQ)
    gate_ref[...] = jnp.where(sel > 0.0, aff, 0.0).reshape(ROUTE_B, N_EXPERTS, SEQ)
    count = pos + sel
    tok_last_lo = jnp.sum(jnp.where(count <= CAP // 2 - 1, 1.0, 0.0), axis=1, keepdims=True)
    tok_first_hi = jnp.sum(jnp.where(count <= CAP // 2, 1.0, 0.0), axis=1, keepdims=True)
    banded = jnp.logical_and(tok_last_lo < GATHER_LO_END, tok_first_hi >= GATHER_HI_START)
    tok_mid_lo = jnp.sum(jnp.where(count <= CAP // 4 - 1, 1.0, 0.0), axis=1, keepdims=True)
    tok_mid_hi = jnp.sum(jnp.where(count <= 3 * CAP // 4, 1.0, 0.0), axis=1, keepdims=True)
    mid = jnp.logical_and(tok_mid_lo < GATHER_HI_START, tok_mid_hi >= GATHER_LO_END)
    flags = jnp.where(banded, 1, 0) + jnp.where(mid, 2, 0)
    band_ref[...] = flags.astype(jnp.int32).reshape(ROUTE_B, N_EXPERTS, 1)


def _route(logits):
    spec = pl.BlockSpec((ROUTE_B, N_EXPERTS, SEQ), lambda b: (b, 0, 0))
    shape = jax.ShapeDtypeStruct((BATCH, N_EXPERTS, SEQ), F32)
    return pl.pallas_call(
        _route_kernel,
        grid=(BATCH // ROUTE_B,),
        in_specs=[spec],
        out_specs=[spec, spec, pl.BlockSpec((ROUTE_B, N_EXPERTS, 1), lambda b: (b, 0, 0))],
        out_shape=[shape, shape, jax.ShapeDtypeStruct((BATCH, N_EXPERTS, 1), jnp.int32)],
        compiler_params=_params("arbitrary"),
        name="route",
    )(logits)


def _ffn_kernel(band_ref, pos_ref, h_ref, wg_ref, wu_ref, wd_ref, yet_ref,
                wg_b, wu_b, wdt_b, dense_ref):
    e, b = pl.program_id(0), pl.program_id(1)

    @pl.when(b == 0)
    def _():
        wg_b[...] = wg_ref[0].astype(BF16)
        wu_b[...] = wu_ref[0].astype(BF16)
        wdt_b[...] = wd_ref[0].T.astype(BF16)

    def gather(slots, toks):
        n_slots, n_toks = slots.stop - slots.start, toks.stop - toks.start
        slot = (lax.broadcasted_iota(jnp.int32, (n_slots, n_toks), 0) + slots.start).astype(F32)
        onehot = jnp.where(slot == pos_ref[0, 0, :, toks], 1.0, 0.0).astype(BF16)
        return _dot(onehot, h_ref[0, toks, :]).astype(BF16)

    banded = (band_ref[b * N_EXPERTS + e] & 1) == 1

    @pl.when(jnp.logical_and(e == 0, b == 0))
    def _():
        dense_ref[...] = jnp.zeros_like(dense_ref)

    @pl.when(jnp.logical_not(banded))
    def _():
        dense_ref[...] = gather(slice(0, CAP), slice(0, SEQ))

    windowed = jnp.concatenate([gather(slice(0, CAP // 2), slice(0, GATHER_LO_END)),
                                gather(slice(CAP // 2, CAP), slice(GATHER_HI_START, SEQ))], axis=0)
    xe = jnp.where(banded, windowed, dense_ref[...])
    act = _dot(xe, wg_b[...])
    up = _dot(xe, wu_b[...])
    hid = (act * _sigmoid(act) * up).astype(BF16)
    ye_t = _dot_nt(wdt_b[...], hid)
    yet_ref[0, 0] = ye_t[:, :CAP // 2].astype(BF16)
    yet_ref[0, 1] = ye_t[:, CAP // 2:].astype(BF16)
    yet_ref[0, 2] = ye_t[:, CAP // 4:3 * CAP // 4].astype(BF16)


def _ffn(band, pos, h2, wg, wu, wd):
    def wspec(lead):
        return pl.BlockSpec(
            (1, D_MODEL, EXPERT_FF),
            lambda e, b: (jnp.minimum(e + jnp.where(b >= BATCH - lead, 1, 0), N_EXPERTS - 1),
                          0, 0))
    wscr = pltpu.VMEM((D_MODEL, EXPERT_FF), BF16)
    return pl.pallas_call(
        _ffn_kernel,
        grid=(N_EXPERTS, BATCH),
        in_specs=[pl.BlockSpec(memory_space=pltpu.SMEM),
                  pl.BlockSpec((1, 1, 1, SEQ), lambda e, b: (b, e, 0, 0)),
                  pl.BlockSpec((1, SEQ, D_MODEL), lambda e, b: (b, 0, 0)),
                  wspec(3), wspec(2), wspec(1)],
        out_specs=pl.BlockSpec((1, N_PLANES, D_MODEL, CAP // 2), lambda e, b: (b, 0, 0, e)),
        out_shape=jax.ShapeDtypeStruct((BATCH, N_PLANES, D_MODEL, N_EXPERTS * CAP // 2), BF16),
        scratch_shapes=[wscr, wscr, wscr, pltpu.VMEM((CAP, D_MODEL), BF16)],
        compiler_params=_params("arbitrary", "arbitrary"),
        name="ffn",
    )(band.reshape(-1), pos[:, :, None, :], h2, wg, wu, wd)


def _combine_kernel(band_ref, pos_ref, gate_ref, yet_ref, x1_ref, g2_ref, fg_ref, o_ref,
                    other_ref):
    b, c = pl.program_id(0), pl.program_id(1)
    n_half = n_mid = 0
    for ee in range(N_EXPERTS):
        flags = band_ref[b * N_EXPERTS + ee]
        n_half, n_mid = n_half + (flags & 1), n_mid + (flags >> 1)
    tok0, tok1 = c * SCAT_TOK, (c + 1) * SCAT_TOK
    lo_only = jnp.logical_and(n_half == N_EXPERTS, tok1 <= GATHER_HI_START)
    hi_only = jnp.logical_and(n_half == N_EXPERTS, tok0 >= GATHER_LO_END)
    mid_only = jnp.logical_and(n_mid == N_EXPERTS,
                               jnp.logical_and(tok0 >= GATHER_HI_START, tok1 <= GATHER_LO_END))
    both = jnp.logical_not(jnp.logical_or(jnp.logical_or(lo_only, hi_only), mid_only))

    def half(plane, first_slot):
        slot = (lax.broadcasted_iota(jnp.int32, (CAP // 2, SCAT_TOK), 0) + first_slot).astype(F32)
        scat = jnp.concatenate(
            [jnp.where(slot == pos_ref[0, ee:ee + 1, :], gate_ref[0, ee:ee + 1, :], 0.0)
             .astype(BF16) for ee in range(N_EXPERTS)], axis=0)
        return _dot(yet_ref[0, plane], scat)

    @pl.when(jnp.logical_and(b == 0, c == 0))
    def _():
        other_ref[...] = jnp.zeros_like(other_ref)

    @pl.when(both)
    def _():
        other_ref[...] = half(1, CAP // 2)

    plane = jnp.where(hi_only, 1, jnp.where(mid_only, 2, 0))
    first_slot = jnp.where(hi_only, CAP // 2, jnp.where(mid_only, CAP // 4, 0))
    moe_t = half(plane, first_slot) + jnp.where(both, other_ref[...], 0.0)
    x2 = x1_ref[0] + g2_ref[0] * moe_t.T
    o_ref[0] = x2 * lax.rsqrt(jnp.mean(x2 * x2, axis=-1, keepdims=True) + EPS) * fg_ref[...]


def _combine(band, pos, gate, yet, x1, g2, fg):
    sel_c = pl.BlockSpec((1, N_EXPERTS, SCAT_TOK), lambda b, c: (b, 0, c))
    tok_c = pl.BlockSpec((1, SCAT_TOK, D_MODEL), lambda b, c: (b, c, 0))
    yet_spec = pl.BlockSpec((1, N_PLANES, D_MODEL, N_EXPERTS * CAP // 2),
                            lambda b, c: (b, 0, 0, 0))
    return pl.pallas_call(
        _combine_kernel,
        grid=(BATCH, N_SCAT),
        in_specs=[pl.BlockSpec(memory_space=pltpu.SMEM), sel_c, sel_c, yet_spec,
                  tok_c,
                  pl.BlockSpec((1, 1, D_MODEL), lambda b, c: (b, 0, 0)),
                  pl.BlockSpec((1, D_MODEL), lambda b, c: (0, 0))],
        out_specs=tok_c,
        out_shape=jax.ShapeDtypeStruct((BATCH, SEQ, D_MODEL), F32),
        scratch_shapes=[pltpu.VMEM((D_MODEL, SCAT_TOK), F32)],
        compiler_params=_params("arbitrary", "arbitrary"),
        name="combine",
    )(band.reshape(-1), pos, gate, yet, x1, g2, fg)


def _rope_tables():
    rows = SEQ // GRID_W
    row = np.repeat(np.arange(rows, dtype=np.float32), GRID_W)
    col = np.tile(np.arange(GRID_W, dtype=np.float32), rows)
    n_freq = HEAD_DIM // 4
    inv_freq = (ROPE_BASE ** (-np.arange(n_freq, dtype=np.float32) / n_freq)).astype(np.float32)
    ang_r = row[:, None] * inv_freq[None, :]
    ang_c = col[:, None] * inv_freq[None, :]
    ang = np.concatenate([ang_r, ang_r, ang_c, ang_c], axis=-1)
    sign = np.tile(np.concatenate([-np.ones(n_freq, np.float32), np.ones(n_freq, np.float32)]), 2)
    reps = LANES // HEAD_DIM
    cos = np.tile(np.cos(ang).astype(np.float32), (1, reps))
    sin = np.tile((np.sin(ang) * sign).astype(np.float32), (1, reps))
    return jnp.asarray(cos), jnp.asarray(sin)


def kernel(x, c, ctx, c_ctx, w_mod, b_mod, norm1_g, w_in, attn_sink, conv_w, w_proj_attn,
           w_proj_conv, w_out, norm2_g, w_router, w_exp_gate, w_exp_up, w_exp_down, final_norm_g):
    assert x.shape == (BATCH, SEQ, D_MODEL) and ctx.shape == (BATCH, CTX_LEN, D_MODEL)
    assert w_mod.shape[0] == 1, "single-layer problem"
    layer = 0

    cvec = jnp.concatenate(
        [c, c_ctx[None, :], jnp.zeros((MOD_ROWS - BATCH - 1, D_MODEL), F32)], axis=0)
    mod = _mod(cvec, w_mod[layer], b_mod[layer][None, :])
    chunk = lambda rows, k: rows[:, k * D_MODEL:(k + 1) * D_MODEL]
    mod_x = mod[:BATCH][:, None, :]
    sh1, sc1, g1, sh2, sc2, g2 = (mod_x[..., k * D_MODEL:(k + 1) * D_MODEL] for k in range(N_MOD))
    mod_c = mod[BATCH:BATCH + 1]
    csh1, csc1 = chunk(mod_c, 0), chunk(mod_c, 1)

    n1 = norm1_g[layer][None, :]
    w_in_b = w_in[layer].astype(BF16)
    k_ctx, vt_ctx = _ctx_kv(ctx, n1, csh1, csc1, w_in_b[:, OFF_K:OFF_U])

    cos, sin = _rope_tables()
    q, k, vt, z, bg, ga, gc = _proj(x, n1, sh1, sc1, w_in_b, cos, sin)
    sink_log2 = attn_sink[layer].reshape(-1) * LOG2E
    o_attn = _attn(sink_log2, q, k, vt, k_ctx, vt_ctx)

    wr_hi, wr_lo = _split_bf16(w_router[layer].T)
    w3 = jnp.concatenate(
        [w_proj_attn[layer], w_proj_conv[layer], w_out[layer],
         jnp.zeros((D_MODEL, W3_WIDTH - 3 * D_MODEL), F32)], axis=1).astype(BF16)
    x1, h2, logits = _merge(
        x, o_attn, z, bg, ga, gc, conv_w[layer], w3,
        g1, sh2, sc2, norm2_g[layer][None, :], jnp.concatenate([wr_hi, wr_lo], axis=0))

    pos, gate, band = _route(logits)
    yet = _ffn(band, pos, h2, w_exp_gate[layer], w_exp_up[layer], w_exp_down[layer])
    return _combine(band, pos, gate, yet, x1, g2, final_norm_g[None, :])
```

```python
import jax
import jax.numpy as jnp
import numpy as np
from jax import lax
from jax.experimental import pallas as pl
from jax.experimental.pallas import tpu as pltpu

D_MODEL = 1024
BATCH = 16
SEQ = 2048
CTX_LEN = 256
GRID_W = 64
N_HEADS = 16
N_KV_HEADS = 4
GROUP = N_HEADS // N_KV_HEADS
HEAD_DIM = D_MODEL // N_HEADS
ATTN_WIDTH = N_HEADS * HEAD_DIM
KV_WIDTH = N_KV_HEADS * HEAD_DIM
WINDOW = 128
Q_BLOCK = 128
ROPE_BASE = 10000.0
CONV_WIDTH = D_MODEL
N_EXPERTS = 16
EXPERT_FF = D_MODEL
CAPACITY_FACTOR = 2
N_MOD = 6
EPS = 1e-6
NEG_INF = -1e30
LOG2E = 1.4426950408889634

CAP = CAPACITY_FACTOR * SEQ // N_EXPERTS
N_QBLK = SEQ // Q_BLOCK
LANES = 128
F32_ROWS = 8
BF16_ROWS = 16
W3_WIDTH = 3 * D_MODEL + LANES
TOK_TILE = 512
PROJ_SUB = 256
ATTN_TILE = 1024
MERGE_TILE = 512
SUB_TILE = 256
CTX_B = 4
ROUTE_B = 4
FFN_B = 2
GATHER_LO_END = 5 * SEQ // 8
GATHER_HI_START = 3 * SEQ // 8
N_SCAT = 8
SCAT_TOK = SEQ // N_SCAT
MOD_ROWS = 32
VMEM_LIMIT = 56 * 1024 * 1024

OFF_Q = 0
OFF_K = OFF_Q + ATTN_WIDTH
OFF_V = OFF_K + KV_WIDTH
OFF_U = OFF_V + KV_WIDTH
OFF_B = OFF_U + CONV_WIDTH
OFF_C = OFF_B + CONV_WIDTH
OFF_GA = OFF_C + CONV_WIDTH
OFF_GC = OFF_GA + D_MODEL
IN_WIDTH = OFF_GC + D_MODEL

F32 = jnp.float32
BF16 = jnp.bfloat16


def _params(*sem):
    return pltpu.CompilerParams(dimension_semantics=sem, vmem_limit_bytes=VMEM_LIMIT)


def _dot(a, b):
    return jnp.dot(a, b, preferred_element_type=F32)


def _dot_nt(a, b):
    return lax.dot_general(a, b, (((1,), (1,)), ((), ())), preferred_element_type=F32)


def _sigmoid(x):
    return 1.0 / (1.0 + jnp.exp(-x))


def _split_bf16(x):
    hi = x.astype(BF16)
    lo = (x - hi.astype(F32)).astype(BF16)
    return hi, lo


def _norm_mod(x, g, shift, scale):
    y = x * lax.rsqrt(jnp.mean(x * x, axis=-1, keepdims=True) + EPS) * g
    return y * (1.0 + scale) + shift


def _mod_kernel(c_ref, w_ref, b_ref, o_ref):
    cv = c_ref[...]
    s_hi, s_lo = _split_bf16(cv * _sigmoid(cv))
    w_hi, w_lo = _split_bf16(w_ref[...])
    o_ref[...] = _dot(s_hi, w_hi) + _dot(s_lo, w_hi) + _dot(s_hi, w_lo) + b_ref[...]


def _mod(cvec, w_mod, b_mod):
    n_out = N_MOD * D_MODEL
    blk = D_MODEL
    return pl.pallas_call(
        _mod_kernel,
        grid=(n_out // blk,),
        in_specs=[
            pl.BlockSpec((MOD_ROWS, D_MODEL), lambda j: (0, 0)),
            pl.BlockSpec((D_MODEL, blk), lambda j: (0, j)),
            pl.BlockSpec((1, blk), lambda j: (0, j)),
        ],
        out_specs=pl.BlockSpec((MOD_ROWS, blk), lambda j: (0, j)),
        out_shape=jax.ShapeDtypeStruct((MOD_ROWS, n_out), F32),
        compiler_params=_params("arbitrary"),
        name="mod",
    )(cvec, w_mod, b_mod)


def _ctx_kernel(ctx_ref, g_ref, sh_ref, sc_ref, w_ref, k_ref, vt_ref):
    x = ctx_ref[...].reshape(CTX_B * CTX_LEN, D_MODEL)
    h = _norm_mod(x, g_ref[...], sh_ref[...], sc_ref[...]).astype(BF16)
    kv = _dot(h, w_ref[...])
    k_ref[...] = kv[:, :KV_WIDTH].astype(BF16).reshape(CTX_B, CTX_LEN, KV_WIDTH)
    for i in range(CTX_B):
        vt_ref[i] = kv[i * CTX_LEN:(i + 1) * CTX_LEN, KV_WIDTH:].T.astype(BF16)


def _ctx_kv(ctx, g1, csh1, csc1, w_kv):
    row = pl.BlockSpec((1, D_MODEL), lambda b: (0, 0))
    return pl.pallas_call(
        _ctx_kernel,
        grid=(BATCH // CTX_B,),
        in_specs=[
            pl.BlockSpec((CTX_B, CTX_LEN, D_MODEL), lambda b: (b, 0, 0)),
            row, row, row,
            pl.BlockSpec((D_MODEL, 2 * KV_WIDTH), lambda b: (0, 0)),
        ],
        out_specs=[pl.BlockSpec((CTX_B, CTX_LEN, KV_WIDTH), lambda b: (b, 0, 0)),
                   pl.BlockSpec((CTX_B, KV_WIDTH, CTX_LEN), lambda b: (b, 0, 0))],
        out_shape=[jax.ShapeDtypeStruct((BATCH, CTX_LEN, KV_WIDTH), BF16),
                   jax.ShapeDtypeStruct((BATCH, KV_WIDTH, CTX_LEN), BF16)],
        compiler_params=_params("arbitrary"),
        name="ctx_kv",
    )(ctx, g1, csh1, csc1, w_kv)


def _proj_kernel(x_ref, g_ref, sh_ref, sc_ref, w_ref, cos_ref, sin_ref,
                 q_ref, k_ref, vt_ref, z_ref, bg_ref, ga_ref, gc_ref):
    lane = lax.broadcasted_iota(jnp.int32, (PROJ_SUB, LANES), 1)
    first_half = (lane & (HEAD_DIM // 4)) == 0
    scale = HEAD_DIM ** -0.5 * LOG2E

    for r in range(TOK_TILE // PROJ_SUB):
        rs = slice(r * PROJ_SUB, (r + 1) * PROJ_SUB)
        h = _norm_mod(x_ref[0, rs, :], g_ref[...], sh_ref[0], sc_ref[0]).astype(BF16)
        cos = cos_ref[rs, :]
        sin = sin_ref[rs, :]

        def rope(a):
            rot = jnp.where(first_half,
                            pltpu.roll(a, LANES - HEAD_DIM // 4, 1),
                            pltpu.roll(a, HEAD_DIM // 4, 1))
            return a * cos + rot * sin

        def proj(off, width):
            return _dot(h, w_ref[:, off:off + width])

        qa = proj(OFF_Q, ATTN_WIDTH)
        for j in range(ATTN_WIDTH // LANES):
            sl = slice(j * LANES, (j + 1) * LANES)
            q_ref[0, rs, sl] = (rope(qa[:, sl]) * scale).astype(BF16)
        ka = proj(OFF_K, KV_WIDTH)
        for j in range(KV_WIDTH // LANES):
            sl = slice(j * LANES, (j + 1) * LANES)
            k_ref[0, rs, sl] = rope(ka[:, sl]).astype(BF16)
        vt_ref[0, :, rs] = proj(OFF_V, KV_WIDTH).T.astype(BF16)
        z_ref[0, rs, :] = (proj(OFF_C, CONV_WIDTH) * proj(OFF_U, CONV_WIDTH)).astype(BF16)
        bg_ref[0, rs, :] = proj(OFF_B, CONV_WIDTH).astype(BF16)
        ga_ref[0, rs, :] = _sigmoid(proj(OFF_GA, D_MODEL)).astype(BF16)
        gc_ref[0, rs, :] = _sigmoid(proj(OFF_GC, D_MODEL)).astype(BF16)


def _proj(x, g1, sh1, sc1, w_in, cos, sin):
    nt = SEQ // TOK_TILE
    tile = lambda w: pl.BlockSpec((1, TOK_TILE, w), lambda b, t: (b, t, 0))
    per_b = pl.BlockSpec((1, 1, D_MODEL), lambda b, t: (b, 0, 0))
    tab = pl.BlockSpec((TOK_TILE, LANES), lambda b, t: (t, 0))
    shp = lambda w: jax.ShapeDtypeStruct((BATCH, SEQ, w), BF16)
    vt_spec = pl.BlockSpec((1, KV_WIDTH, TOK_TILE), lambda b, t: (b, 0, t))
    vt_shape = jax.ShapeDtypeStruct((BATCH, KV_WIDTH, SEQ), BF16)
    return pl.pallas_call(
        _proj_kernel,
        grid=(BATCH, nt),
        in_specs=[
            tile(D_MODEL),
            pl.BlockSpec((1, D_MODEL), lambda b, t: (0, 0)),
            per_b, per_b,
            pl.BlockSpec((D_MODEL, IN_WIDTH), lambda b, t: (0, 0), pipeline_mode=pl.Buffered(1)),
            tab, tab,
        ],
        out_specs=[tile(ATTN_WIDTH), tile(KV_WIDTH), vt_spec, tile(CONV_WIDTH), tile(CONV_WIDTH),
                   tile(D_MODEL), tile(D_MODEL)],
        out_shape=[shp(ATTN_WIDTH), shp(KV_WIDTH), vt_shape, shp(CONV_WIDTH), shp(CONV_WIDTH),
                   shp(D_MODEL), shp(D_MODEL)],
        compiler_params=_params("arbitrary", "arbitrary"),
        name="proj",
    )(x, g1, sh1, sc1, w_in, cos, sin)


def _attn_kernel(sink_ref, q_ref, k_ref, vt_ref, kx_ref, vx_ref, o_ref):
    cols = GROUP * Q_BLOCK
    key = lax.broadcasted_iota(jnp.int32, (Q_BLOCK, cols), 0)
    qry = lax.broadcasted_iota(jnp.int32, (Q_BLOCK, cols), 1) & (Q_BLOCK - 1)
    lane_head = lax.broadcasted_iota(jnp.int32, (1, cols), 1) >> 7
    n_keys = Q_BLOCK + 2 * WINDOW + CTX_LEN
    ones = jnp.ones((BF16_ROWS, n_keys), BF16)

    blocks = []
    for sb in range(ATTN_TILE // Q_BLOCK):
        i = pl.program_id(1) * (ATTN_TILE // Q_BLOCK) + sb
        blocks.append(dict(
            rows=slice(sb * Q_BLOCK, (sb + 1) * Q_BLOCK),
            p0=pl.multiple_of(jnp.maximum(i - 1, 0) * Q_BLOCK, Q_BLOCK),
            c0=pl.multiple_of(i * Q_BLOCK, Q_BLOCK),
            n0=pl.multiple_of(jnp.minimum(i + 1, N_QBLK - 1) * Q_BLOCK, Q_BLOCK),
            ok_prev=key >= qry + jnp.where(i > 0, 0, Q_BLOCK),
            ok_next=key <= qry - jnp.where(i < N_QBLK - 1, 0, Q_BLOCK)))

    def scores(blk, g):
        ks = slice(g * HEAD_DIM, (g + 1) * HEAD_DIM)
        q4 = jnp.concatenate(
            [q_ref[0, blk["rows"], (g * GROUP + j) * HEAD_DIM:(g * GROUP + j + 1) * HEAD_DIM]
             for j in range(GROUP)], axis=0)
        k_all = jnp.concatenate(
            [k_ref[0, pl.ds(blk["p0"], Q_BLOCK), ks], k_ref[0, pl.ds(blk["c0"], Q_BLOCK), ks],
             k_ref[0, pl.ds(blk["n0"], Q_BLOCK), ks], kx_ref[0, :, ks]], axis=0)
        return _dot_nt(k_all, q4)

    def softmax(blk, g, st):
        pieces = [jnp.where(blk["ok_prev"], st[:WINDOW], NEG_INF),
                  st[WINDOW:WINDOW + Q_BLOCK],
                  jnp.where(blk["ok_next"], st[WINDOW + Q_BLOCK:2 * WINDOW + Q_BLOCK], NEG_INF),
                  st[2 * WINDOW + Q_BLOCK:]]
        sink = jnp.zeros((1, cols), F32)
        for j in range(GROUP):
            sink = jnp.where(lane_head == j, sink_ref[g * GROUP + j], sink)
        m = sink
        for piece in pieces:
            m = jnp.maximum(m, jnp.max(piece, axis=0, keepdims=True))
        pt = jnp.concatenate([jnp.exp2((piece - m).astype(BF16)) for piece in pieces], axis=0)
        return pt, jnp.exp2(sink - m)

    def values(blk, g, pt, p_sink):
        ks = slice(g * HEAD_DIM, (g + 1) * HEAD_DIM)
        vt = jnp.concatenate(
            [vt_ref[0, ks, pl.ds(blk["p0"], Q_BLOCK)], vt_ref[0, ks, pl.ds(blk["c0"], Q_BLOCK)],
             vt_ref[0, ks, pl.ds(blk["n0"], Q_BLOCK)], vx_ref[0, ks, :]], axis=1)
        ot = _dot(jnp.concatenate([vt, ones], axis=0), pt)
        denom = ot[HEAD_DIM:HEAD_DIM + 1] + p_sink
        on = ot[:HEAD_DIM] * (1.0 / denom)
        for jj in range(GROUP // 2):
            pair_t = jnp.concatenate(
                [on[:, (2 * jj) * Q_BLOCK:(2 * jj + 1) * Q_BLOCK],
                 on[:, (2 * jj + 1) * Q_BLOCK:(2 * jj + 2) * Q_BLOCK]], axis=0)
            col = (g * GROUP + 2 * jj) * HEAD_DIM
            o_ref[0, blk["rows"], col:col + 2 * HEAD_DIM] = pair_t.T.astype(BF16)

    chains = [(blk, g) for blk in blocks for g in range(N_KV_HEADS)]
    st_next = scores(*chains[0])
    for n, chain in enumerate(chains):
        st = st_next
        if n + 1 < len(chains):
            st_next = scores(*chains[n + 1])
        values(*chain, *softmax(*chain, st))


def _attn(sink, q, k, vt, k_ctx, vt_ctx):
    qspec = pl.BlockSpec((1, ATTN_TILE, ATTN_WIDTH), lambda b, i: (b, i, 0))
    return pl.pallas_call(
        _attn_kernel,
        grid=(BATCH, SEQ // ATTN_TILE),
        in_specs=[pl.BlockSpec(memory_space=pltpu.SMEM), qspec,
                  pl.BlockSpec((1, SEQ, KV_WIDTH), lambda b, i: (b, 0, 0)),
                  pl.BlockSpec((1, KV_WIDTH, SEQ), lambda b, i: (b, 0, 0)),
                  pl.BlockSpec((1, CTX_LEN, KV_WIDTH), lambda b, i: (b, 0, 0)),
                  pl.BlockSpec((1, KV_WIDTH, CTX_LEN), lambda b, i: (b, 0, 0))],
        out_specs=qspec,
        out_shape=jax.ShapeDtypeStruct((BATCH, SEQ, ATTN_WIDTH), BF16),
        compiler_params=_params("arbitrary", "arbitrary"),
        name="attn",
    )(sink, q, k, vt, k_ctx, vt_ctx)


def _merge_kernel(x_ref, oa_ref, z_ref, zp_ref, zn_ref, bg_ref, ga_ref, gc_ref, cw_ref,
                  w3_ref, g1_ref, sh2_ref, sc2_ref, n2_ref, wr_ref,
                  x1_ref, h2_ref, lg_ref):
    t = pl.program_id(1)
    nt = pl.num_programs(1)
    z_before = zp_ref[0, BF16_ROWS - 1:BF16_ROWS, :].astype(F32) * jnp.where(t > 0, 1.0, 0.0)
    z_after = zn_ref[0, 0:1, :].astype(F32) * jnp.where(t < nt - 1, 1.0, 0.0)
    row = lax.broadcasted_iota(jnp.int32, (F32_ROWS, 1), 0)
    n_sub = MERGE_TILE // SUB_TILE
    subs = [slice(r * SUB_TILE, (r + 1) * SUB_TILE) for r in range(n_sub)]

    def conv(r):
        lo, hi = subs[r].start, subs[r].stop
        z = z_ref[0, subs[r], :].astype(F32)
        before = z_before if r == 0 else z_ref[0, lo - 1:lo, :].astype(F32)
        after = z_after if r == n_sub - 1 else z_ref[0, hi:hi + 1, :].astype(F32)
        up = pltpu.roll(z, 1, 0)
        z_prev = jnp.concatenate(
            [jnp.where(row == 0, before, up[:F32_ROWS]), up[F32_ROWS:]], axis=0)
        dn = pltpu.roll(z, SUB_TILE - 1, 0)
        z_next = jnp.concatenate(
            [dn[:-F32_ROWS], jnp.where(row == F32_ROWS - 1, after, dn[-F32_ROWS:])], axis=0)
        y = bg_ref[0, subs[r], :].astype(F32) * (
            cw_ref[0:1, :] * z_prev + cw_ref[1:2, :] * z + cw_ref[2:3, :] * z_next)
        return y.astype(BF16)

    a, s = [], []
    for r in range(n_sub):
        a.append(_dot(oa_ref[0, subs[r], :], w3_ref[:, 0:D_MODEL]))
        s.append(_dot(conv(r), w3_ref[:, D_MODEL:2 * D_MODEL]))
    branch = []
    for r, rs in enumerate(subs):
        merged = ga_ref[0, rs, :].astype(F32) * a[r] + gc_ref[0, rs, :].astype(F32) * s[r]
        branch.append(_dot(merged.astype(BF16), w3_ref[:, 2 * D_MODEL:3 * D_MODEL]))
    for r, rs in enumerate(subs):
        x1 = x_ref[0, rs, :] + g1_ref[0] * branch[r]
        x1_ref[0, rs, :] = x1
        h2 = _norm_mod(x1, n2_ref[...], sh2_ref[0], sc2_ref[0])
        h_hi, h_lo = _split_bf16(h2)
        h2_ref[0, rs, :] = h_hi
        both = _dot_nt(wr_ref[...], h_hi)
        lg_ref[0, :, rs] = (both[:N_EXPERTS] + both[N_EXPERTS:]
                            + _dot_nt(wr_ref[:N_EXPERTS], h_lo))


def _merge(x, oa, z, bg, ga, gc, conv_w, w3, g1, sh2, sc2, n2, wr):
    nt = SEQ // MERGE_TILE
    hb = MERGE_TILE // BF16_ROWS
    n_hb = SEQ // BF16_ROWS
    tile = pl.BlockSpec((1, MERGE_TILE, D_MODEL), lambda b, t: (b, t, 0))
    halo_p = pl.BlockSpec((1, BF16_ROWS, D_MODEL), lambda b, t: (b, jnp.maximum(t * hb - 1, 0), 0))
    halo_n = pl.BlockSpec((1, BF16_ROWS, D_MODEL),
                          lambda b, t: (b, jnp.minimum((t + 1) * hb, n_hb - 1), 0))
    per_b = pl.BlockSpec((1, 1, D_MODEL), lambda b, t: (b, 0, 0))
    full = lambda r, c: pl.BlockSpec((r, c), lambda b, t: (0, 0))
    return pl.pallas_call(
        _merge_kernel,
        grid=(BATCH, nt),
        in_specs=[tile, tile, tile, halo_p, halo_n, tile, tile, tile,
                  full(3, D_MODEL), full(D_MODEL, W3_WIDTH),
                  per_b, per_b, per_b, full(1, D_MODEL), full(2 * N_EXPERTS, D_MODEL)],
        out_specs=[tile, tile, pl.BlockSpec((1, N_EXPERTS, MERGE_TILE), lambda b, t: (b, 0, t))],
        out_shape=[jax.ShapeDtypeStruct((BATCH, SEQ, D_MODEL), F32),
                   jax.ShapeDtypeStruct((BATCH, SEQ, D_MODEL), BF16),
                   jax.ShapeDtypeStruct((BATCH, N_EXPERTS, SEQ), F32)],
        compiler_params=_params("arbitrary", "arbitrary"),
        name="merge",
    )(x, oa, z, z, z, bg, ga, gc, conv_w, w3, g1, sh2, sc2, n2, wr)


def _cumsum_excl(mf, upper):
    nblk = SEQ // LANES
    rows = mf.shape[0]
    stacked = jnp.concatenate([mf[:, j * LANES:(j + 1) * LANES] for j in range(nblk)], axis=0)
    within = _dot(stacked.astype(BF16), upper)
    tot = jnp.sum(stacked, axis=1, keepdims=True)
    off = jnp.zeros((rows, 1), F32)
    out = []
    for j in range(nblk):
        rs = slice(j * rows, (j + 1) * rows)
        out.append(within[rs] + off)
        off = off + tot[rs]
    return jnp.concatenate(out, axis=1)


def _route_kernel(lg_ref, pos_ref, gate_ref, band_ref):
    rows = ROUTE_B * N_EXPERTS
    lg = lg_ref[...]
    ex = jnp.exp(lg - jnp.max(lg, axis=1, keepdims=True))
    aff = (ex / jnp.sum(ex, axis=1, keepdims=True)).reshape(rows, SEQ)

    def body(_, carry):
        lo, hi = carry
        mid = lo + ((hi - lo) >> 1)
        cnt = jnp.sum(jnp.where(aff >= pltpu.bitcast(mid, F32), 1.0, 0.0), axis=1, keepdims=True)
        ge = cnt >= CAP
        return jnp.where(ge, mid, lo), jnp.where(ge, hi, mid)

    one_bits = 0x3F800000
    lo0 = jnp.zeros((rows, 1), jnp.int32)
    hi0 = jnp.full((rows, 1), one_bits + 1, jnp.int32)
    lo, _ = lax.fori_loop(0, 31, body, (lo0, hi0))
    th = pltpu.bitcast(lo, F32)

    ku = lax.broadcasted_iota(jnp.int32, (LANES, LANES), 0)
    nu = lax.broadcasted_iota(jnp.int32, (LANES, LANES), 1)
    upper = jnp.where(ku < nu, 1.0, 0.0).astype(BF16)
    gtf = jnp.where(aff > th, 1.0, 0.0)
    eqf = jnp.where(aff == th, 1.0, 0.0)
    need = CAP - jnp.sum(gtf, axis=1, keepdims=True)
    sel = gtf + eqf * jnp.where(_cumsum_excl(eqf, upper) < need, 1.0, 0.0)
    pos = _cumsum_excl(sel, upper)
    pos_ref[...] = jnp.where(sel > 0.0, pos, -1.0).reshape(ROUTE_B, N_EXPERTS, SEQ)
    gate_ref[...] = jnp.where(sel > 0.0, aff, 0.0).reshape(ROUTE_B, N_EXPERTS, SEQ)
    count = pos + sel
    tok_last_lo = jnp.sum(jnp.where(count <= CAP // 2 - 1, 1.0, 0.0), axis=1, keepdims=True)
    tok_first_hi = jnp.sum(jnp.where(count <= CAP // 2, 1.0, 0.0), axis=1, keepdims=True)
    banded = jnp.logical_and(tok_last_lo < GATHER_LO_END, tok_first_hi >= GATHER_HI_START)
    band_ref[...] = jnp.where(banded, 1, 0).astype(jnp.int32).reshape(ROUTE_B, N_EXPERTS, 1)


def _route(logits):
    spec = pl.BlockSpec((ROUTE_B, N_EXPERTS, SEQ), lambda b: (b, 0, 0))
    shape = jax.ShapeDtypeStruct((BATCH, N_EXPERTS, SEQ), F32)
    return pl.pallas_call(
        _route_kernel,
        grid=(BATCH // ROUTE_B,),
        in_specs=[spec],
        out_specs=[spec, spec, pl.BlockSpec((ROUTE_B, N_EXPERTS, 1), lambda b: (b, 0, 0))],
        out_shape=[shape, shape, jax.ShapeDtypeStruct((BATCH, N_EXPERTS, 1), jnp.int32)],
        compiler_params=_params("arbitrary"),
        name="route",
    )(logits)


def _ffn_kernel(band_ref, pos_ref, h_ref, wg_ref, wu_ref, wd_ref, yet_ref,
                wg_b, wu_b, wdt_b, dense_ref):
    e, grp = pl.program_id(0), pl.program_id(1)

    @pl.when(grp == 0)
    def _():
        wg_b[...] = wg_ref[0].astype(BF16)
        wu_b[...] = wu_ref[0].astype(BF16)
        wdt_b[...] = wd_ref[0].T.astype(BF16)

    def gather(i, slots, toks):
        n_slots, n_toks = slots.stop - slots.start, toks.stop - toks.start
        slot = (lax.broadcasted_iota(jnp.int32, (n_slots, n_toks), 0) + slots.start).astype(F32)
        onehot = jnp.where(slot == pos_ref[i, 0, :, toks], 1.0, 0.0).astype(BF16)
        return _dot(onehot, h_ref[i, toks, :]).astype(BF16)

    @pl.when(jnp.logical_and(e == 0, grp == 0))
    def _():
        dense_ref[...] = jnp.zeros_like(dense_ref)

    banded = [band_ref[(grp * FFN_B + i) * N_EXPERTS + e] == 1 for i in range(FFN_B)]
    for i in range(FFN_B):
        @pl.when(jnp.logical_not(banded[i]))
        def _(i=i):
            dense_ref[i] = gather(i, slice(0, CAP), slice(0, SEQ))

    xes = []
    for i in range(FFN_B):
        windowed = jnp.concatenate(
            [gather(i, slice(0, CAP // 2), slice(0, GATHER_LO_END)),
             gather(i, slice(CAP // 2, CAP), slice(GATHER_HI_START, SEQ))], axis=0)
        xes.append(jnp.where(banded[i], windowed, dense_ref[i]))
    xe = jnp.concatenate(xes, axis=0)
    act = _dot(xe, wg_b[...])
    up = _dot(xe, wu_b[...])
    hid = (act * _sigmoid(act) * up).astype(BF16)
    ye_t = _dot_nt(wdt_b[...], hid)
    for i in range(FFN_B):
        yet_ref[i, 0] = ye_t[:, i * CAP:i * CAP + CAP // 2].astype(BF16)
        yet_ref[i, 1] = ye_t[:, i * CAP + CAP // 2:(i + 1) * CAP].astype(BF16)


def _ffn(band, pos, h2, wg, wu, wd):
    n_grp = BATCH // FFN_B

    def wspec(lead):
        return pl.BlockSpec(
            (1, D_MODEL, EXPERT_FF),
            lambda e, g: (jnp.minimum(e + jnp.where(g >= n_grp - lead, 1, 0), N_EXPERTS - 1),
                          0, 0))
    wscr = pltpu.VMEM((D_MODEL, EXPERT_FF), BF16)
    return pl.pallas_call(
        _ffn_kernel,
        grid=(N_EXPERTS, n_grp),
        in_specs=[pl.BlockSpec(memory_space=pltpu.SMEM),
                  pl.BlockSpec((FFN_B, 1, 1, SEQ), lambda e, g: (g, e, 0, 0)),
                  pl.BlockSpec((FFN_B, SEQ, D_MODEL), lambda e, g: (g, 0, 0)),
                  wspec(3), wspec(2), wspec(1)],
        out_specs=pl.BlockSpec((FFN_B, 2, D_MODEL, CAP // 2), lambda e, g: (g, 0, 0, e)),
        out_shape=jax.ShapeDtypeStruct((BATCH, 2, D_MODEL, N_EXPERTS * CAP // 2), BF16),
        scratch_shapes=[wscr, wscr, wscr, pltpu.VMEM((FFN_B, CAP, D_MODEL), BF16)],
        compiler_params=_params("arbitrary", "arbitrary"),
        name="ffn",
    )(band.reshape(-1), pos[:, :, None, :], h2, wg, wu, wd)


def _combine_kernel(band_ref, pos_ref, gate_ref, yet_ref, x1_ref, g2_ref, fg_ref, o_ref,
                    other_ref):
    b, c = pl.program_id(0), pl.program_id(1)
    n_banded = band_ref[b * N_EXPERTS]
    for ee in range(1, N_EXPERTS):
        n_banded = n_banded + band_ref[b * N_EXPERTS + ee]
    all_banded = n_banded == N_EXPERTS
    lo_only = jnp.logical_and(all_banded, (c + 1) * SCAT_TOK <= GATHER_HI_START)
    hi_only = jnp.logical_and(all_banded, c * SCAT_TOK >= GATHER_LO_END)
    both = jnp.logical_not(jnp.logical_or(lo_only, hi_only))

    def half(plane):
        slot = (lax.broadcasted_iota(jnp.int32, (CAP // 2, SCAT_TOK), 0)
                + plane * (CAP // 2)).astype(F32)
        scat = jnp.concatenate(
            [jnp.where(slot == pos_ref[0, ee:ee + 1, :], gate_ref[0, ee:ee + 1, :], 0.0)
             .astype(BF16) for ee in range(N_EXPERTS)], axis=0)
        return _dot(yet_ref[0, plane], scat)

    @pl.when(jnp.logical_and(b == 0, c == 0))
    def _():
        other_ref[...] = jnp.zeros_like(other_ref)

    @pl.when(both)
    def _():
        other_ref[...] = half(1)

    moe_t = half(jnp.where(hi_only, 1, 0)) + jnp.where(both, other_ref[...], 0.0)
    x2 = x1_ref[0] + g2_ref[0] * moe_t.T
    o_ref[0] = x2 * lax.rsqrt(jnp.mean(x2 * x2, axis=-1, keepdims=True) + EPS) * fg_ref[...]


def _combine(band, pos, gate, yet, x1, g2, fg):
    sel_c = pl.BlockSpec((1, N_EXPERTS, SCAT_TOK), lambda b, c: (b, 0, c))
    tok_c = pl.BlockSpec((1, SCAT_TOK, D_MODEL), lambda b, c: (b, c, 0))
    yet_spec = pl.BlockSpec((1, 2, D_MODEL, N_EXPERTS * CAP // 2), lambda b, c: (b, 0, 0, 0))
    return pl.pallas_call(
        _combine_kernel,
        grid=(BATCH, N_SCAT),
        in_specs=[pl.BlockSpec(memory_space=pltpu.SMEM), sel_c, sel_c, yet_spec,
                  tok_c,
                  pl.BlockSpec((1, 1, D_MODEL), lambda b, c: (b, 0, 0)),
                  pl.BlockSpec((1, D_MODEL), lambda b, c: (0, 0))],
        out_specs=tok_c,
        out_shape=jax.ShapeDtypeStruct((BATCH, SEQ, D_MODEL), F32),
        scratch_shapes=[pltpu.VMEM((D_MODEL, SCAT_TOK), F32)],
        compiler_params=_params("arbitrary", "arbitrary"),
        name="combine",
    )(band.reshape(-1), pos, gate, yet, x1, g2, fg)


def _rope_tables():
    rows = SEQ // GRID_W
    row = np.repeat(np.arange(rows, dtype=np.float32), GRID_W)
    col = np.tile(np.arange(GRID_W, dtype=np.float32), rows)
    n_freq = HEAD_DIM // 4
    inv_freq = (ROPE_BASE ** (-np.arange(n_freq, dtype=np.float32) / n_freq)).astype(np.float32)
    ang_r = row[:, None] * inv_freq[None, :]
    ang_c = col[:, None] * inv_freq[None, :]
    ang = np.concatenate([ang_r, ang_r, ang_c, ang_c], axis=-1)
    sign = np.tile(np.concatenate([-np.ones(n_freq, np.float32), np.ones(n_freq, np.float32)]), 2)
    reps = LANES // HEAD_DIM
    cos = np.tile(np.cos(ang).astype(np.float32), (1, reps))
    sin = np.tile((np.sin(ang) * sign).astype(np.float32), (1, reps))
    return jnp.asarray(cos), jnp.asarray(sin)


def kernel(x, c, ctx, c_ctx, w_mod, b_mod, norm1_g, w_in, attn_sink, conv_w, w_proj_attn,
           w_proj_conv, w_out, norm2_g, w_router, w_exp_gate, w_exp_up, w_exp_down, final_norm_g):
    assert x.shape == (BATCH, SEQ, D_MODEL) and ctx.shape == (BATCH, CTX_LEN, D_MODEL)
    assert w_mod.shape[0] == 1, "single-layer problem"
    layer = 0

    cvec = jnp.concatenate(
        [c, c_ctx[None, :], jnp.zeros((MOD_ROWS - BATCH - 1, D_MODEL), F32)], axis=0)
    mod = _mod(cvec, w_mod[layer], b_mod[layer][None, :])
    chunk = lambda rows, k: rows[:, k * D_MODEL:(k + 1) * D_MODEL]
    mod_x = mod[:BATCH][:, None, :]
    sh1, sc1, g1, sh2, sc2, g2 = (mod_x[..., k * D_MODEL:(k + 1) * D_MODEL] for k in range(N_MOD))
    mod_c = mod[BATCH:BATCH + 1]
    csh1, csc1 = chunk(mod_c, 0), chunk(mod_c, 1)

    n1 = norm1_g[layer][None, :]
    w_in_b = w_in[layer].astype(BF16)
    k_ctx, vt_ctx = _ctx_kv(ctx, n1, csh1, csc1, w_in_b[:, OFF_K:OFF_U])

    cos, sin = _rope_tables()
    q, k, vt, z, bg, ga, gc = _proj(x, n1, sh1, sc1, w_in_b, cos, sin)
    sink_log2 = attn_sink[layer].reshape(-1) * LOG2E
    o_attn = _attn(sink_log2, q, k, vt, k_ctx, vt_ctx)

    wr_hi, wr_lo = _split_bf16(w_router[layer].T)
    w3 = jnp.concatenate(
        [w_proj_attn[layer], w_proj_conv[layer], w_out[layer],
         jnp.zeros((D_MODEL, W3_WIDTH - 3 * D_MODEL), F32)], axis=1).astype(BF16)
    x1, h2, logits = _merge(
        x, o_attn, z, bg, ga, gc, conv_w[layer], w3,
        g1, sh2, sc2, norm2_g[layer][None, :], jnp.concatenate([wr_hi, wr_lo], axis=0))

    pos, gate, band = _route(logits)
    yet = _ffn(band, pos, h2, w_exp_gate[layer], w_exp_up[layer], w_exp_down[layer])
    return _combine(band, pos, gate, yet, x1, g2, final_norm_g[None, :])
```

```python
import jax
import jax.numpy as jnp
import numpy as np
from jax import lax
from jax.experimental import pallas as pl
from jax.experimental.pallas import tpu as pltpu

D_MODEL = 1024
BATCH = 16
SEQ = 2048
CTX_LEN = 256
GRID_W = 64
N_HEADS = 16
N_KV_HEADS = 4
GROUP = N_HEADS // N_KV_HEADS
HEAD_DIM = D_MODEL // N_HEADS
ATTN_WIDTH = N_HEADS * HEAD_DIM
KV_WIDTH = N_KV_HEADS * HEAD_DIM
WINDOW = 128
Q_BLOCK = 128
ROPE_BASE = 10000.0
CONV_WIDTH = D_MODEL
N_EXPERTS = 16
EXPERT_FF = D_MODEL
CAPACITY_FACTOR = 2
N_MOD = 6
EPS = 1e-6
NEG_INF = -1e30
LOG2E = 1.4426950408889634

CAP = CAPACITY_FACTOR * SEQ // N_EXPERTS
N_QBLK = SEQ // Q_BLOCK
LANES = 128
F32_ROWS = 8
BF16_ROWS = 16
W3_WIDTH = 3 * D_MODEL + LANES
TOK_TILE = 512
PROJ_SUB = 256
ATTN_TILE = 1024
MERGE_TILE = 512
SUB_TILE = 256
CTX_B = 4
ROUTE_B = 4
FFN_B = 2
GATHER_LO_END = 5 * SEQ // 8
GATHER_HI_START = 3 * SEQ // 8
N_SCAT = 8
SCAT_TOK = SEQ // N_SCAT
SCAT_PER_STEP = 4
MOD_ROWS = 32
VMEM_LIMIT = 56 * 1024 * 1024

OFF_Q = 0
OFF_K = OFF_Q + ATTN_WIDTH
OFF_V = OFF_K + KV_WIDTH
OFF_U = OFF_V + KV_WIDTH
OFF_B = OFF_U + CONV_WIDTH
OFF_C = OFF_B + CONV_WIDTH
OFF_GA = OFF_C + CONV_WIDTH
OFF_GC = OFF_GA + D_MODEL
IN_WIDTH = OFF_GC + D_MODEL

F32 = jnp.float32
BF16 = jnp.bfloat16


def _params(*sem):
    return pltpu.CompilerParams(dimension_semantics=sem, vmem_limit_bytes=VMEM_LIMIT)


def _dot(a, b):
    return jnp.dot(a, b, preferred_element_type=F32)


def _dot_nt(a, b):
    return lax.dot_general(a, b, (((1,), (1,)), ((), ())), preferred_element_type=F32)


def _sigmoid(x):
    return 1.0 / (1.0 + jnp.exp(-x))


def _split_bf16(x):
    hi = x.astype(BF16)
    lo = (x - hi.astype(F32)).astype(BF16)
    return hi, lo


def _norm_mod(x, g, shift, scale):
    y = x * lax.rsqrt(jnp.mean(x * x, axis=-1, keepdims=True) + EPS) * g
    return y * (1.0 + scale) + shift


def _mod_kernel(c_ref, w_ref, b_ref, o_ref):
    cv = c_ref[...]
    s_hi, s_lo = _split_bf16(cv * _sigmoid(cv))
    w_hi, w_lo = _split_bf16(w_ref[...])
    o_ref[...] = _dot(s_hi, w_hi) + _dot(s_lo, w_hi) + _dot(s_hi, w_lo) + b_ref[...]


def _mod(cvec, w_mod, b_mod):
    n_out = N_MOD * D_MODEL
    blk = D_MODEL
    return pl.pallas_call(
        _mod_kernel,
        grid=(n_out // blk,),
        in_specs=[
            pl.BlockSpec((MOD_ROWS, D_MODEL), lambda j: (0, 0)),
            pl.BlockSpec((D_MODEL, blk), lambda j: (0, j)),
            pl.BlockSpec((1, blk), lambda j: (0, j)),
        ],
        out_specs=pl.BlockSpec((MOD_ROWS, blk), lambda j: (0, j)),
        out_shape=jax.ShapeDtypeStruct((MOD_ROWS, n_out), F32),
        compiler_params=_params("arbitrary"),
        name="mod",
    )(cvec, w_mod, b_mod)


def _ctx_kernel(ctx_ref, g_ref, sh_ref, sc_ref, w_ref, k_ref, vt_ref):
    x = ctx_ref[...].reshape(CTX_B * CTX_LEN, D_MODEL)
    h = _norm_mod(x, g_ref[...], sh_ref[...], sc_ref[...]).astype(BF16)
    kv = _dot(h, w_ref[...])
    k_ref[...] = kv[:, :KV_WIDTH].astype(BF16).reshape(CTX_B, CTX_LEN, KV_WIDTH)
    for i in range(CTX_B):
        vt_ref[i] = kv[i * CTX_LEN:(i + 1) * CTX_LEN, KV_WIDTH:].T.astype(BF16)


def _ctx_kv(ctx, g1, csh1, csc1, w_kv):
    row = pl.BlockSpec((1, D_MODEL), lambda b: (0, 0))
    return pl.pallas_call(
        _ctx_kernel,
        grid=(BATCH // CTX_B,),
        in_specs=[
            pl.BlockSpec((CTX_B, CTX_LEN, D_MODEL), lambda b: (b, 0, 0)),
            row, row, row,
            pl.BlockSpec((D_MODEL, 2 * KV_WIDTH), lambda b: (0, 0)),
        ],
        out_specs=[pl.BlockSpec((CTX_B, CTX_LEN, KV_WIDTH), lambda b: (b, 0, 0)),
                   pl.BlockSpec((CTX_B, KV_WIDTH, CTX_LEN), lambda b: (b, 0, 0))],
        out_shape=[jax.ShapeDtypeStruct((BATCH, CTX_LEN, KV_WIDTH), BF16),
                   jax.ShapeDtypeStruct((BATCH, KV_WIDTH, CTX_LEN), BF16)],
        compiler_params=_params("arbitrary"),
        name="ctx_kv",
    )(ctx, g1, csh1, csc1, w_kv)


def _proj_kernel(x_ref, g_ref, sh_ref, sc_ref, w_ref, cos_ref, sin_ref,
                 q_ref, k_ref, vt_ref, z_ref, bg_ref, ga_ref, gc_ref):
    lane = lax.broadcasted_iota(jnp.int32, (PROJ_SUB, LANES), 1)
    first_half = (lane & (HEAD_DIM // 4)) == 0
    scale = HEAD_DIM ** -0.5 * LOG2E

    for r in range(TOK_TILE // PROJ_SUB):
        rs = slice(r * PROJ_SUB, (r + 1) * PROJ_SUB)
        h = _norm_mod(x_ref[0, rs, :], g_ref[...], sh_ref[0], sc_ref[0]).astype(BF16)
        cos = cos_ref[rs, :]
        sin = sin_ref[rs, :]

        def rope(a):
            rot = jnp.where(first_half,
                            pltpu.roll(a, LANES - HEAD_DIM // 4, 1),
                            pltpu.roll(a, HEAD_DIM // 4, 1))
            return a * cos + rot * sin

        def proj(off, width):
            return _dot(h, w_ref[:, off:off + width])

        qa = proj(OFF_Q, ATTN_WIDTH)
        for j in range(ATTN_WIDTH // LANES):
            sl = slice(j * LANES, (j + 1) * LANES)
            q_ref[0, rs, sl] = (rope(qa[:, sl]) * scale).astype(BF16)
        ka = proj(OFF_K, KV_WIDTH)
        for j in range(KV_WIDTH // LANES):
            sl = slice(j * LANES, (j + 1) * LANES)
            k_ref[0, rs, sl] = rope(ka[:, sl]).astype(BF16)
        vt_ref[0, :, rs] = proj(OFF_V, KV_WIDTH).T.astype(BF16)
        z_ref[0, rs, :] = (proj(OFF_C, CONV_WIDTH) * proj(OFF_U, CONV_WIDTH)).astype(BF16)
        bg_ref[0, rs, :] = proj(OFF_B, CONV_WIDTH).astype(BF16)
        ga_ref[0, rs, :] = _sigmoid(proj(OFF_GA, D_MODEL)).astype(BF16)
        gc_ref[0, rs, :] = _sigmoid(proj(OFF_GC, D_MODEL)).astype(BF16)


def _proj(x, g1, sh1, sc1, w_in, cos, sin):
    nt = SEQ // TOK_TILE
    tile = lambda w: pl.BlockSpec((1, TOK_TILE, w), lambda b, t: (b, t, 0))
    per_b = pl.BlockSpec((1, 1, D_MODEL), lambda b, t: (b, 0, 0))
    tab = pl.BlockSpec((TOK_TILE, LANES), lambda b, t: (t, 0))
    shp = lambda w: jax.ShapeDtypeStruct((BATCH, SEQ, w), BF16)
    vt_spec = pl.BlockSpec((1, KV_WIDTH, TOK_TILE), lambda b, t: (b, 0, t))
    vt_shape = jax.ShapeDtypeStruct((BATCH, KV_WIDTH, SEQ), BF16)
    return pl.pallas_call(
        _proj_kernel,
        grid=(BATCH, nt),
        in_specs=[
            tile(D_MODEL),
            pl.BlockSpec((1, D_MODEL), lambda b, t: (0, 0)),
            per_b, per_b,
            pl.BlockSpec((D_MODEL, IN_WIDTH), lambda b, t: (0, 0), pipeline_mode=pl.Buffered(1)),
            tab, tab,
        ],
        out_specs=[tile(ATTN_WIDTH), tile(KV_WIDTH), vt_spec, tile(CONV_WIDTH), tile(CONV_WIDTH),
                   tile(D_MODEL), tile(D_MODEL)],
        out_shape=[shp(ATTN_WIDTH), shp(KV_WIDTH), vt_shape, shp(CONV_WIDTH), shp(CONV_WIDTH),
                   shp(D_MODEL), shp(D_MODEL)],
        compiler_params=_params("arbitrary", "arbitrary"),
        name="proj",
    )(x, g1, sh1, sc1, w_in, cos, sin)


def _attn_kernel(sink_ref, q_ref, k_ref, vt_ref, kx_ref, vx_ref, o_ref):
    cols = GROUP * Q_BLOCK
    key = lax.broadcasted_iota(jnp.int32, (Q_BLOCK, cols), 0)
    qry = lax.broadcasted_iota(jnp.int32, (Q_BLOCK, cols), 1) & (Q_BLOCK - 1)
    lane_head = lax.broadcasted_iota(jnp.int32, (1, cols), 1) >> 7
    n_keys = Q_BLOCK + 2 * WINDOW + CTX_LEN
    ones = jnp.ones((BF16_ROWS, n_keys), BF16)

    blocks = []
    for sb in range(ATTN_TILE // Q_BLOCK):
        i = pl.program_id(1) * (ATTN_TILE // Q_BLOCK) + sb
        blocks.append(dict(
            rows=slice(sb * Q_BLOCK, (sb + 1) * Q_BLOCK),
            p0=pl.multiple_of(jnp.maximum(i - 1, 0) * Q_BLOCK, Q_BLOCK),
            c0=pl.multiple_of(i * Q_BLOCK, Q_BLOCK),
            n0=pl.multiple_of(jnp.minimum(i + 1, N_QBLK - 1) * Q_BLOCK, Q_BLOCK),
            ok_prev=key >= qry + jnp.where(i > 0, 0, Q_BLOCK),
            ok_next=key <= qry - jnp.where(i < N_QBLK - 1, 0, Q_BLOCK)))

    def scores(blk, g):
        ks = slice(g * HEAD_DIM, (g + 1) * HEAD_DIM)
        q4 = jnp.concatenate(
            [q_ref[0, blk["rows"], (g * GROUP + j) * HEAD_DIM:(g * GROUP + j + 1) * HEAD_DIM]
             for j in range(GROUP)], axis=0)
        k_all = jnp.concatenate(
            [k_ref[0, pl.ds(blk["p0"], Q_BLOCK), ks], k_ref[0, pl.ds(blk["c0"], Q_BLOCK), ks],
             k_ref[0, pl.ds(blk["n0"], Q_BLOCK), ks], kx_ref[0, :, ks]], axis=0)
        return _dot_nt(k_all, q4)

    def softmax(blk, g, st):
        pieces = [jnp.where(blk["ok_prev"], st[:WINDOW], NEG_INF),
                  st[WINDOW:WINDOW + Q_BLOCK],
                  jnp.where(blk["ok_next"], st[WINDOW + Q_BLOCK:2 * WINDOW + Q_BLOCK], NEG_INF),
                  st[2 * WINDOW + Q_BLOCK:]]
        sink = jnp.zeros((1, cols), F32)
        for j in range(GROUP):
            sink = jnp.where(lane_head == j, sink_ref[g * GROUP + j], sink)
        m = sink
        for piece in pieces:
            m = jnp.maximum(m, jnp.max(piece, axis=0, keepdims=True))
        pt = jnp.concatenate([jnp.exp2((piece - m).astype(BF16)) for piece in pieces], axis=0)
        return pt, jnp.exp2(sink - m)

    def values(blk, g, pt, p_sink):
        ks = slice(g * HEAD_DIM, (g + 1) * HEAD_DIM)
        vt = jnp.concatenate(
            [vt_ref[0, ks, pl.ds(blk["p0"], Q_BLOCK)], vt_ref[0, ks, pl.ds(blk["c0"], Q_BLOCK)],
             vt_ref[0, ks, pl.ds(blk["n0"], Q_BLOCK)], vx_ref[0, ks, :]], axis=1)
        ot = _dot(jnp.concatenate([vt, ones], axis=0), pt)
        denom = ot[HEAD_DIM:HEAD_DIM + 1] + p_sink
        on = ot[:HEAD_DIM] * (1.0 / denom)
        for jj in range(GROUP // 2):
            pair_t = jnp.concatenate(
                [on[:, (2 * jj) * Q_BLOCK:(2 * jj + 1) * Q_BLOCK],
                 on[:, (2 * jj + 1) * Q_BLOCK:(2 * jj + 2) * Q_BLOCK]], axis=0)
            col = (g * GROUP + 2 * jj) * HEAD_DIM
            o_ref[0, blk["rows"], col:col + 2 * HEAD_DIM] = pair_t.T.astype(BF16)

    chains = [(blk, g) for blk in blocks for g in range(N_KV_HEADS)]
    st_next = scores(*chains[0])
    for n, chain in enumerate(chains):
        st = st_next
        if n + 1 < len(chains):
            st_next = scores(*chains[n + 1])
        values(*chain, *softmax(*chain, st))


def _attn(sink, q, k, vt, k_ctx, vt_ctx):
    qspec = pl.BlockSpec((1, ATTN_TILE, ATTN_WIDTH), lambda b, i: (b, i, 0))
    return pl.pallas_call(
        _attn_kernel,
        grid=(BATCH, SEQ // ATTN_TILE),
        in_specs=[pl.BlockSpec(memory_space=pltpu.SMEM), qspec,
                  pl.BlockSpec((1, SEQ, KV_WIDTH), lambda b, i: (b, 0, 0)),
                  pl.BlockSpec((1, KV_WIDTH, SEQ), lambda b, i: (b, 0, 0)),
                  pl.BlockSpec((1, CTX_LEN, KV_WIDTH), lambda b, i: (b, 0, 0)),
                  pl.BlockSpec((1, KV_WIDTH, CTX_LEN), lambda b, i: (b, 0, 0))],
        out_specs=qspec,
        out_shape=jax.ShapeDtypeStruct((BATCH, SEQ, ATTN_WIDTH), BF16),
        compiler_params=_params("arbitrary", "arbitrary"),
        name="attn",
    )(sink, q, k, vt, k_ctx, vt_ctx)


def _merge_kernel(x_ref, oa_ref, z_ref, zp_ref, zn_ref, bg_ref, ga_ref, gc_ref, cw_ref,
                  w3_ref, g1_ref, sh2_ref, sc2_ref, n2_ref, wr_ref,
                  x1_ref, h2_ref, lg_ref):
    t = pl.program_id(1)
    nt = pl.num_programs(1)
    z_before = zp_ref[0, BF16_ROWS - 1:BF16_ROWS, :].astype(F32) * jnp.where(t > 0, 1.0, 0.0)
    z_after = zn_ref[0, 0:1, :].astype(F32) * jnp.where(t < nt - 1, 1.0, 0.0)
    row = lax.broadcasted_iota(jnp.int32, (F32_ROWS, 1), 0)
    n_sub = MERGE_TILE // SUB_TILE
    subs = [slice(r * SUB_TILE, (r + 1) * SUB_TILE) for r in range(n_sub)]

    def conv(r):
        lo, hi = subs[r].start, subs[r].stop
        z = z_ref[0, subs[r], :].astype(F32)
        before = z_before if r == 0 else z_ref[0, lo - 1:lo, :].astype(F32)
        after = z_after if r == n_sub - 1 else z_ref[0, hi:hi + 1, :].astype(F32)
        up = pltpu.roll(z, 1, 0)
        z_prev = jnp.concatenate(
            [jnp.where(row == 0, before, up[:F32_ROWS]), up[F32_ROWS:]], axis=0)
        dn = pltpu.roll(z, SUB_TILE - 1, 0)
        z_next = jnp.concatenate(
            [dn[:-F32_ROWS], jnp.where(row == F32_ROWS - 1, after, dn[-F32_ROWS:])], axis=0)
        y = bg_ref[0, subs[r], :].astype(F32) * (
            cw_ref[0:1, :] * z_prev + cw_ref[1:2, :] * z + cw_ref[2:3, :] * z_next)
        return y.astype(BF16)

    a, s = [], []
    for r in range(n_sub):
        a.append(_dot(oa_ref[0, subs[r], :], w3_ref[:, 0:D_MODEL]))
        s.append(_dot(conv(r), w3_ref[:, D_MODEL:2 * D_MODEL]))
    branch = []
    for r, rs in enumerate(subs):
        merged = ga_ref[0, rs, :].astype(F32) * a[r] + gc_ref[0, rs, :].astype(F32) * s[r]
        branch.append(_dot(merged.astype(BF16), w3_ref[:, 2 * D_MODEL:3 * D_MODEL]))
    for r, rs in enumerate(subs):
        x1 = x_ref[0, rs, :] + g1_ref[0] * branch[r]
        x1_ref[0, rs, :] = x1
        h2 = _norm_mod(x1, n2_ref[...], sh2_ref[0], sc2_ref[0])
        h_hi, h_lo = _split_bf16(h2)
        h2_ref[0, rs, :] = h_hi
        both = _dot_nt(wr_ref[...], h_hi)
        lg_ref[0, :, rs] = (both[:N_EXPERTS] + both[N_EXPERTS:]
                            + _dot_nt(wr_ref[:N_EXPERTS], h_lo))


def _merge(x, oa, z, bg, ga, gc, conv_w, w3, g1, sh2, sc2, n2, wr):
    nt = SEQ // MERGE_TILE
    hb = MERGE_TILE // BF16_ROWS
    n_hb = SEQ // BF16_ROWS
    tile = pl.BlockSpec((1, MERGE_TILE, D_MODEL), lambda b, t: (b, t, 0))
    halo_p = pl.BlockSpec((1, BF16_ROWS, D_MODEL), lambda b, t: (b, jnp.maximum(t * hb - 1, 0), 0))
    halo_n = pl.BlockSpec((1, BF16_ROWS, D_MODEL),
                          lambda b, t: (b, jnp.minimum((t + 1) * hb, n_hb - 1), 0))
    per_b = pl.BlockSpec((1, 1, D_MODEL), lambda b, t: (b, 0, 0))
    full = lambda r, c: pl.BlockSpec((r, c), lambda b, t: (0, 0))
    return pl.pallas_call(
        _merge_kernel,
        grid=(BATCH, nt),
        in_specs=[tile, tile, tile, halo_p, halo_n, tile, tile, tile,
                  full(3, D_MODEL), full(D_MODEL, W3_WIDTH),
                  per_b, per_b, per_b, full(1, D_MODEL), full(2 * N_EXPERTS, D_MODEL)],
        out_specs=[tile, tile, pl.BlockSpec((1, N_EXPERTS, MERGE_TILE), lambda b, t: (b, 0, t))],
        out_shape=[jax.ShapeDtypeStruct((BATCH, SEQ, D_MODEL), F32),
                   jax.ShapeDtypeStruct((BATCH, SEQ, D_MODEL), BF16),
                   jax.ShapeDtypeStruct((BATCH, N_EXPERTS, SEQ), F32)],
        compiler_params=_params("arbitrary", "arbitrary"),
        name="merge",
    )(x, oa, z, z, z, bg, ga, gc, conv_w, w3, g1, sh2, sc2, n2, wr)


def _cumsum_excl(mf, upper):
    nblk = SEQ // LANES
    rows = mf.shape[0]
    stacked = jnp.concatenate([mf[:, j * LANES:(j + 1) * LANES] for j in range(nblk)], axis=0)
    within = _dot(stacked.astype(BF16), upper)
    tot = jnp.sum(stacked, axis=1, keepdims=True)
    off = jnp.zeros((rows, 1), F32)
    out = []
    for j in range(nblk):
        rs = slice(j * rows, (j + 1) * rows)
        out.append(within[rs] + off)
        off = off + tot[rs]
    return jnp.concatenate(out, axis=1)


def _route_kernel(lg_ref, pos_ref, gate_ref, band_ref):
    rows = ROUTE_B * N_EXPERTS
    lg = lg_ref[...]
    ex = jnp.exp(lg - jnp.max(lg, axis=1, keepdims=True))
    aff = (ex / jnp.sum(ex, axis=1, keepdims=True)).reshape(rows, SEQ)

    def body(_, carry):
        lo, hi = carry
        mid = lo + ((hi - lo) >> 1)
        cnt = jnp.sum(jnp.where(aff >= pltpu.bitcast(mid, F32), 1.0, 0.0), axis=1, keepdims=True)
        ge = cnt >= CAP
        return jnp.where(ge, mid, lo), jnp.where(ge, hi, mid)

    one_bits = 0x3F800000
    lo0 = jnp.zeros((rows, 1), jnp.int32)
    hi0 = jnp.full((rows, 1), one_bits + 1, jnp.int32)
    lo, _ = lax.fori_loop(0, 31, body, (lo0, hi0))
    th = pltpu.bitcast(lo, F32)

    ku = lax.broadcasted_iota(jnp.int32, (LANES, LANES), 0)
    nu = lax.broadcasted_iota(jnp.int32, (LANES, LANES), 1)
    upper = jnp.where(ku < nu, 1.0, 0.0).astype(BF16)
    gtf = jnp.where(aff > th, 1.0, 0.0)
    eqf = jnp.where(aff == th, 1.0, 0.0)
    need = CAP - jnp.sum(gtf, axis=1, keepdims=True)
    sel = gtf + eqf * jnp.where(_cumsum_excl(eqf, upper) < need, 1.0, 0.0)
    pos = _cumsum_excl(sel, upper)
    pos_ref[...] = jnp.where(sel > 0.0, pos, -1.0).reshape(ROUTE_B, N_EXPERTS, SEQ)
    gate_ref[...] = jnp.where(sel > 0.0, aff, 0.0).reshape(ROUTE_B, N_EXPERTS, SEQ)
    count = pos + sel
    tok_last_lo = jnp.sum(jnp.where(count <= CAP // 2 - 1, 1.0, 0.0), axis=1, keepdims=True)
    tok_first_hi = jnp.sum(jnp.where(count <= CAP // 2, 1.0, 0.0), axis=1, keepdims=True)
    banded = jnp.logical_and(tok_last_lo < GATHER_LO_END, tok_first_hi >= GATHER_HI_START)
    band_ref[...] = jnp.where(banded, 1, 0).astype(jnp.int32).reshape(ROUTE_B, N_EXPERTS, 1)


def _route(logits):
    spec = pl.BlockSpec((ROUTE_B, N_EXPERTS, SEQ), lambda b: (b, 0, 0))
    shape = jax.ShapeDtypeStruct((BATCH, N_EXPERTS, SEQ), F32)
    return pl.pallas_call(
        _route_kernel,
        grid=(BATCH // ROUTE_B,),
        in_specs=[spec],
        out_specs=[spec, spec, pl.BlockSpec((ROUTE_B, N_EXPERTS, 1), lambda b: (b, 0, 0))],
        out_shape=[shape, shape, jax.ShapeDtypeStruct((BATCH, N_EXPERTS, 1), jnp.int32)],
        compiler_params=_params("arbitrary"),
        name="route",
    )(logits)


def _ffn_kernel(band_ref, pos_ref, h_ref, wg_ref, wu_ref, wd_ref, yet_ref,
                wg_b, wu_b, wdt_b, dense_ref):
    e, grp = pl.program_id(0), pl.program_id(1)

    @pl.when(grp == 0)
    def _():
        wg_b[...] = wg_ref[0].astype(BF16)
        wu_b[...] = wu_ref[0].astype(BF16)
        wdt_b[...] = wd_ref[0].T.astype(BF16)

    def gather(i, slots, toks):
        n_slots, n_toks = slots.stop - slots.start, toks.stop - toks.start
        slot = (lax.broadcasted_iota(jnp.int32, (n_slots, n_toks), 0) + slots.start).astype(F32)
        onehot = jnp.where(slot == pos_ref[i, 0, :, toks], 1.0, 0.0).astype(BF16)
        return _dot(onehot, h_ref[i, toks, :]).astype(BF16)

    @pl.when(jnp.logical_and(e == 0, grp == 0))
    def _():
        dense_ref[...] = jnp.zeros_like(dense_ref)

    banded = [band_ref[(grp * FFN_B + i) * N_EXPERTS + e] == 1 for i in range(FFN_B)]
    for i in range(FFN_B):
        @pl.when(jnp.logical_not(banded[i]))
        def _(i=i):
            dense_ref[i] = gather(i, slice(0, CAP), slice(0, SEQ))

    xes = []
    for i in range(FFN_B):
        windowed = jnp.concatenate(
            [gather(i, slice(0, CAP // 2), slice(0, GATHER_LO_END)),
             gather(i, slice(CAP // 2, CAP), slice(GATHER_HI_START, SEQ))], axis=0)
        xes.append(jnp.where(banded[i], windowed, dense_ref[i]))
    xe = jnp.concatenate(xes, axis=0)
    act = _dot(xe, wg_b[...])
    up = _dot(xe, wu_b[...])
    hid = (act * _sigmoid(act) * up).astype(BF16)
    ye_t = _dot_nt(wdt_b[...], hid)
    for i in range(FFN_B):
        yet_ref[i, 0] = ye_t[:, i * CAP:i * CAP + CAP // 2].astype(BF16)
        yet_ref[i, 1] = ye_t[:, i * CAP + CAP // 2:(i + 1) * CAP].astype(BF16)


def _ffn(band, pos, h2, wg, wu, wd):
    n_grp = BATCH // FFN_B

    def wspec(lead):
        return pl.BlockSpec(
            (1, D_MODEL, EXPERT_FF),
            lambda e, g: (jnp.minimum(e + jnp.where(g >= n_grp - lead, 1, 0), N_EXPERTS - 1),
                          0, 0))
    wscr = pltpu.VMEM((D_MODEL, EXPERT_FF), BF16)
    return pl.pallas_call(
        _ffn_kernel,
        grid=(N_EXPERTS, n_grp),
        in_specs=[pl.BlockSpec(memory_space=pltpu.SMEM),
                  pl.BlockSpec((FFN_B, 1, 1, SEQ), lambda e, g: (g, e, 0, 0)),
                  pl.BlockSpec((FFN_B, SEQ, D_MODEL), lambda e, g: (g, 0, 0)),
                  wspec(3), wspec(2), wspec(1)],
        out_specs=pl.BlockSpec((FFN_B, 2, D_MODEL, CAP // 2), lambda e, g: (g, 0, 0, e)),
        out_shape=jax.ShapeDtypeStruct((BATCH, 2, D_MODEL, N_EXPERTS * CAP // 2), BF16),
        scratch_shapes=[wscr, wscr, wscr, pltpu.VMEM((FFN_B, CAP, D_MODEL), BF16)],
        compiler_params=_params("arbitrary", "arbitrary"),
        name="ffn",
    )(band.reshape(-1), pos[:, :, None, :], h2, wg, wu, wd)


def _combine_kernel(band_ref, pos_ref, gate_ref, yet_ref, x1_ref, g2_ref, fg_ref, o_ref,
                    other_ref):
    b, step = pl.program_id(0), pl.program_id(1)
    n_banded = band_ref[b * N_EXPERTS]
    for ee in range(1, N_EXPERTS):
        n_banded = n_banded + band_ref[b * N_EXPERTS + ee]
    all_banded = n_banded == N_EXPERTS

    @pl.when(jnp.logical_and(b == 0, step == 0))
    def _():
        other_ref[...] = jnp.zeros_like(other_ref)

    def half(cols, plane):
        slot = (lax.broadcasted_iota(jnp.int32, (CAP // 2, SCAT_TOK), 0)
                + plane * (CAP // 2)).astype(F32)
        scat = jnp.concatenate(
            [jnp.where(slot == pos_ref[0, ee:ee + 1, cols], gate_ref[0, ee:ee + 1, cols], 0.0)
             .astype(BF16) for ee in range(N_EXPERTS)], axis=0)
        return _dot(yet_ref[0, plane], scat)

    chains = []
    for j in range(SCAT_PER_STEP):
        c = step * SCAT_PER_STEP + j
        cols = slice(j * SCAT_TOK, (j + 1) * SCAT_TOK)
        lo_only = jnp.logical_and(all_banded, (c + 1) * SCAT_TOK <= GATHER_HI_START)
        hi_only = jnp.logical_and(all_banded, c * SCAT_TOK >= GATHER_LO_END)
        both = jnp.logical_not(jnp.logical_or(lo_only, hi_only))
        chains.append((j, cols, hi_only, both))

    for j, cols, _, both in chains:
        @pl.when(both)
        def _(j=j, cols=cols):
            other_ref[j] = half(cols, 1)

    moe_ts = [half(cols, jnp.where(hi_only, 1, 0)) + jnp.where(both, other_ref[j], 0.0)
              for j, cols, hi_only, both in chains]
    for (j, cols, _, _), moe_t in zip(chains, moe_ts):
        x2 = x1_ref[0, cols, :] + g2_ref[0] * moe_t.T
        o_ref[0, cols, :] = (x2 * lax.rsqrt(jnp.mean(x2 * x2, axis=-1, keepdims=True) + EPS)
                             * fg_ref[...])


def _combine(band, pos, gate, yet, x1, g2, fg):
    step_tok = SCAT_PER_STEP * SCAT_TOK
    sel_c = pl.BlockSpec((1, N_EXPERTS, step_tok), lambda b, c: (b, 0, c))
    tok_c = pl.BlockSpec((1, step_tok, D_MODEL), lambda b, c: (b, c, 0))
    yet_spec = pl.BlockSpec((1, 2, D_MODEL, N_EXPERTS * CAP // 2), lambda b, c: (b, 0, 0, 0))
    return pl.pallas_call(
        _combine_kernel,
        grid=(BATCH, N_SCAT // SCAT_PER_STEP),
        in_specs=[pl.BlockSpec(memory_space=pltpu.SMEM), sel_c, sel_c, yet_spec,
                  tok_c,
                  pl.BlockSpec((1, 1, D_MODEL), lambda b, c: (b, 0, 0)),
                  pl.BlockSpec((1, D_MODEL), lambda b, c: (0, 0))],
        out_specs=tok_c,
        out_shape=jax.ShapeDtypeStruct((BATCH, SEQ, D_MODEL), F32),
        scratch_shapes=[pltpu.VMEM((SCAT_PER_STEP, D_MODEL, SCAT_TOK), F32)],
        compiler_params=_params("arbitrary", "arbitrary"),
        name="combine",
    )(band.reshape(-1), pos, gate, yet, x1, g2, fg)


def _rope_tables():
    rows = SEQ // GRID_W
    row = np.repeat(np.arange(rows, dtype=np.float32), GRID_W)
    col = np.tile(np.arange(GRID_W, dtype=np.float32), rows)
    n_freq = HEAD_DIM // 4
    inv_freq = (ROPE_BASE ** (-np.arange(n_freq, dtype=np.float32) / n_freq)).astype(np.float32)
    ang_r = row[:, None] * inv_freq[None, :]
    ang_c = col[:, None] * inv_freq[None, :]
    ang = np.concatenate([ang_r, ang_r, ang_c, ang_c], axis=-1)
    sign = np.tile(np.concatenate([-np.ones(n_freq, np.float32), np.ones(n_freq, np.float32)]), 2)
    reps = LANES // HEAD_DIM
    cos = np.tile(np.cos(ang).astype(np.float32), (1, reps))
    sin = np.tile((np.sin(ang) * sign).astype(np.float32), (1, reps))
    return jnp.asarray(cos), jnp.asarray(sin)


def kernel(x, c, ctx, c_ctx, w_mod, b_mod, norm1_g, w_in, attn_sink, conv_w, w_proj_attn,
           w_proj_conv, w_out, norm2_g, w_router, w_exp_gate, w_exp_up, w_exp_down, final_norm_g):
    assert x.shape == (BATCH, SEQ, D_MODEL) and ctx.shape == (BATCH, CTX_LEN, D_MODEL)
    assert w_mod.shape[0] == 1, "single-layer problem"
    layer = 0

    cvec = jnp.concatenate(
        [c, c_ctx[None, :], jnp.zeros((MOD_ROWS - BATCH - 1, D_MODEL), F32)], axis=0)
    mod = _mod(cvec, w_mod[layer], b_mod[layer][None, :])
    chunk = lambda rows, k: rows[:, k * D_MODEL:(k + 1) * D_MODEL]
    mod_x = mod[:BATCH][:, None, :]
    sh1, sc1, g1, sh2, sc2, g2 = (mod_x[..., k * D_MODEL:(k + 1) * D_MODEL] for k in range(N_MOD))
    mod_c = mod[BATCH:BATCH + 1]
    csh1, csc1 = chunk(mod_c, 0), chunk(mod_c, 1)

    n1 = norm1_g[layer][None, :]
    w_in_b = w_in[layer].astype(BF16)
    k_ctx, vt_ctx = _ctx_kv(ctx, n1, csh1, csc1, w_in_b[:, OFF_K:OFF_U])

    cos, sin = _rope_tables()
    q, k, vt, z, bg, ga, gc = _proj(x, n1, sh1, sc1, w_in_b, cos, sin)
    sink_log2 = attn_sink[layer].reshape(-1) * LOG2E
    o_attn = _attn(sink_log2, q, k, vt, k_ctx, vt_ctx)

    wr_hi, wr_lo = _split_bf16(w_router[layer].T)
    w3 = jnp.concatenate(
        [w_proj_attn[layer], w_proj_conv[layer], w_out[layer],
         jnp.zeros((D_MODEL, W3_WIDTH - 3 * D_MODEL), F32)], axis=1).astype(BF16)
    x1, h2, logits = _merge(
        x, o_attn, z, bg, ga, gc, conv_w[layer], w3,
        g1, sh2, sc2, norm2_g[layer][None, :], jnp.concatenate([wr_hi, wr_lo], axis=0))

    pos, gate, band = _route(logits)
    yet = _ffn(band, pos, h2, w_exp_gate[layer], w_exp_up[layer], w_exp_down[layer])
    return _combine(band, pos, gate, yet, x1, g2, final_norm_g[None, :])
```

```python
import jax
import jax.numpy as jnp
import numpy as np
from jax import lax
from jax.experimental import pallas as pl
from jax.experimental.pallas import tpu as pltpu

D_MODEL = 1024
BATCH = 16
SEQ = 2048
CTX_LEN = 256
GRID_W = 64
N_HEADS = 16
N_KV_HEADS = 4
GROUP = N_HEADS // N_KV_HEADS
HEAD_DIM = D_MODEL // N_HEADS
ATTN_WIDTH = N_HEADS * HEAD_DIM
KV_WIDTH = N_KV_HEADS * HEAD_DIM
WINDOW = 128
Q_BLOCK = 128
ROPE_BASE = 10000.0
CONV_WIDTH = D_MODEL
N_EXPERTS = 16
EXPERT_FF = D_MODEL
CAPACITY_FACTOR = 2
N_MOD = 6
EPS = 1e-6
NEG_INF = -1e30
LOG2E = 1.4426950408889634

CAP = CAPACITY_FACTOR * SEQ // N_EXPERTS
N_QBLK = SEQ // Q_BLOCK
LANES = 128
F32_ROWS = 8
BF16_ROWS = 16
W3_WIDTH = 3 * D_MODEL + LANES
TOK_TILE = 1024
PROJ_SUB = 256
ATTN_TILE = 1024
MERGE_TILE = 1024
SUB_TILE = 256
CTX_B = 4
ROUTE_B = 4
FFN_B = 2
GATHER_LO_END = 5 * SEQ // 8
GATHER_HI_START = 3 * SEQ // 8
N_SCAT = 8
SCAT_TOK = SEQ // N_SCAT
SCAT_PER_STEP = 4
MOD_ROWS = 32
VMEM_LIMIT = 56 * 1024 * 1024

OFF_Q = 0
OFF_K = OFF_Q + ATTN_WIDTH
OFF_V = OFF_K + KV_WIDTH
OFF_U = OFF_V + KV_WIDTH
OFF_B = OFF_U + CONV_WIDTH
OFF_C = OFF_B + CONV_WIDTH
OFF_GA = OFF_C + CONV_WIDTH
OFF_GC = OFF_GA + D_MODEL
IN_WIDTH = OFF_GC + D_MODEL

F32 = jnp.float32
BF16 = jnp.bfloat16


def _params(*sem):
    return pltpu.CompilerParams(dimension_semantics=sem, vmem_limit_bytes=VMEM_LIMIT)


def _dot(a, b):
    return jnp.dot(a, b, preferred_element_type=F32)


def _dot_nt(a, b):
    return lax.dot_general(a, b, (((1,), (1,)), ((), ())), preferred_element_type=F32)


def _sigmoid(x):
    return 1.0 / (1.0 + jnp.exp(-x))


def _split_bf16(x):
    hi = x.astype(BF16)
    lo = (x - hi.astype(F32)).astype(BF16)
    return hi, lo


def _norm_mod(x, g, shift, scale):
    y = x * lax.rsqrt(jnp.mean(x * x, axis=-1, keepdims=True) + EPS) * g
    return y * (1.0 + scale) + shift


def _mod_kernel(c_ref, w_ref, b_ref, o_ref):
    cv = c_ref[...]
    s_hi, s_lo = _split_bf16(cv * _sigmoid(cv))
    w_hi, w_lo = _split_bf16(w_ref[...])
    o_ref[...] = _dot(s_hi, w_hi) + _dot(s_lo, w_hi) + _dot(s_hi, w_lo) + b_ref[...]


def _mod(cvec, w_mod, b_mod):
    n_out = N_MOD * D_MODEL
    blk = D_MODEL
    return pl.pallas_call(
        _mod_kernel,
        grid=(n_out // blk,),
        in_specs=[
            pl.BlockSpec((MOD_ROWS, D_MODEL), lambda j: (0, 0)),
            pl.BlockSpec((D_MODEL, blk), lambda j: (0, j)),
            pl.BlockSpec((1, blk), lambda j: (0, j)),
        ],
        out_specs=pl.BlockSpec((MOD_ROWS, blk), lambda j: (0, j)),
        out_shape=jax.ShapeDtypeStruct((MOD_ROWS, n_out), F32),
        compiler_params=_params("arbitrary"),
        name="mod",
    )(cvec, w_mod, b_mod)


def _ctx_kernel(ctx_ref, g_ref, sh_ref, sc_ref, w_ref, k_ref, vt_ref):
    x = ctx_ref[...].reshape(CTX_B * CTX_LEN, D_MODEL)
    h = _norm_mod(x, g_ref[...], sh_ref[...], sc_ref[...]).astype(BF16)
    kv = _dot(h, w_ref[...])
    k_ref[...] = kv[:, :KV_WIDTH].astype(BF16).reshape(CTX_B, CTX_LEN, KV_WIDTH)
    for i in range(CTX_B):
        vt_ref[i] = kv[i * CTX_LEN:(i + 1) * CTX_LEN, KV_WIDTH:].T.astype(BF16)


def _ctx_kv(ctx, g1, csh1, csc1, w_kv):
    row = pl.BlockSpec((1, D_MODEL), lambda b: (0, 0))
    return pl.pallas_call(
        _ctx_kernel,
        grid=(BATCH // CTX_B,),
        in_specs=[
            pl.BlockSpec((CTX_B, CTX_LEN, D_MODEL), lambda b: (b, 0, 0)),
            row, row, row,
            pl.BlockSpec((D_MODEL, 2 * KV_WIDTH), lambda b: (0, 0)),
        ],
        out_specs=[pl.BlockSpec((CTX_B, CTX_LEN, KV_WIDTH), lambda b: (b, 0, 0)),
                   pl.BlockSpec((CTX_B, KV_WIDTH, CTX_LEN), lambda b: (b, 0, 0))],
        out_shape=[jax.ShapeDtypeStruct((BATCH, CTX_LEN, KV_WIDTH), BF16),
                   jax.ShapeDtypeStruct((BATCH, KV_WIDTH, CTX_LEN), BF16)],
        compiler_params=_params("arbitrary"),
        name="ctx_kv",
    )(ctx, g1, csh1, csc1, w_kv)


def _proj_kernel(x_ref, g_ref, sh_ref, sc_ref, w_ref, cos_ref, sin_ref,
                 q_ref, k_ref, vt_ref, z_ref, bg_ref, ga_ref, gc_ref):
    lane = lax.broadcasted_iota(jnp.int32, (PROJ_SUB, LANES), 1)
    first_half = (lane & (HEAD_DIM // 4)) == 0
    scale = HEAD_DIM ** -0.5 * LOG2E

    for r in range(TOK_TILE // PROJ_SUB):
        rs = slice(r * PROJ_SUB, (r + 1) * PROJ_SUB)
        h = _norm_mod(x_ref[0, rs, :], g_ref[...], sh_ref[0], sc_ref[0]).astype(BF16)
        cos = cos_ref[rs, :]
        sin = sin_ref[rs, :]

        def rope(a):
            rot = jnp.where(first_half,
                            pltpu.roll(a, LANES - HEAD_DIM // 4, 1),
                            pltpu.roll(a, HEAD_DIM // 4, 1))
            return a * cos + rot * sin

        def proj(off, width):
            return _dot(h, w_ref[:, off:off + width])

        qa = proj(OFF_Q, ATTN_WIDTH)
        for j in range(ATTN_WIDTH // LANES):
            sl = slice(j * LANES, (j + 1) * LANES)
            q_ref[0, rs, sl] = (rope(qa[:, sl]) * scale).astype(BF16)
        ka = proj(OFF_K, KV_WIDTH)
        for j in range(KV_WIDTH // LANES):
            sl = slice(j * LANES, (j + 1) * LANES)
            k_ref[0, rs, sl] = rope(ka[:, sl]).astype(BF16)
        vt_ref[0, :, rs] = proj(OFF_V, KV_WIDTH).T.astype(BF16)
        z_ref[0, rs, :] = (proj(OFF_C, CONV_WIDTH) * proj(OFF_U, CONV_WIDTH)).astype(BF16)
        bg_ref[0, rs, :] = proj(OFF_B, CONV_WIDTH).astype(BF16)
        ga_ref[0, rs, :] = _sigmoid(proj(OFF_GA, D_MODEL)).astype(BF16)
        gc_ref[0, rs, :] = _sigmoid(proj(OFF_GC, D_MODEL)).astype(BF16)


def _proj(x, g1, sh1, sc1, w_in, cos, sin):
    nt = SEQ // TOK_TILE
    tile = lambda w: pl.BlockSpec((1, TOK_TILE, w), lambda b, t: (b, t, 0))
    per_b = pl.BlockSpec((1, 1, D_MODEL), lambda b, t: (b, 0, 0))
    tab = pl.BlockSpec((TOK_TILE, LANES), lambda b, t: (t, 0))
    shp = lambda w: jax.ShapeDtypeStruct((BATCH, SEQ, w), BF16)
    vt_spec = pl.BlockSpec((1, KV_WIDTH, TOK_TILE), lambda b, t: (b, 0, t))
    vt_shape = jax.ShapeDtypeStruct((BATCH, KV_WIDTH, SEQ), BF16)
    return pl.pallas_call(
        _proj_kernel,
        grid=(BATCH, nt),
        in_specs=[
            tile(D_MODEL),
            pl.BlockSpec((1, D_MODEL), lambda b, t: (0, 0)),
            per_b, per_b,
            pl.BlockSpec((D_MODEL, IN_WIDTH), lambda b, t: (0, 0), pipeline_mode=pl.Buffered(1)),
            tab, tab,
        ],
        out_specs=[tile(ATTN_WIDTH), tile(KV_WIDTH), vt_spec, tile(CONV_WIDTH), tile(CONV_WIDTH),
                   tile(D_MODEL), tile(D_MODEL)],
        out_shape=[shp(ATTN_WIDTH), shp(KV_WIDTH), vt_shape, shp(CONV_WIDTH), shp(CONV_WIDTH),
                   shp(D_MODEL), shp(D_MODEL)],
        compiler_params=_params("arbitrary", "arbitrary"),
        name="proj",
    )(x, g1, sh1, sc1, w_in, cos, sin)


def _attn_kernel(sink_ref, q_ref, k_ref, vt_ref, kx_ref, vx_ref, o_ref):
    cols = GROUP * Q_BLOCK
    key = lax.broadcasted_iota(jnp.int32, (Q_BLOCK, cols), 0)
    qry = lax.broadcasted_iota(jnp.int32, (Q_BLOCK, cols), 1) & (Q_BLOCK - 1)
    lane_head = lax.broadcasted_iota(jnp.int32, (1, cols), 1) >> 7
    n_keys = Q_BLOCK + 2 * WINDOW + CTX_LEN
    ones = jnp.ones((BF16_ROWS, n_keys), BF16)

    blocks = []
    for sb in range(ATTN_TILE // Q_BLOCK):
        i = pl.program_id(1) * (ATTN_TILE // Q_BLOCK) + sb
        blocks.append(dict(
            rows=slice(sb * Q_BLOCK, (sb + 1) * Q_BLOCK),
            p0=pl.multiple_of(jnp.maximum(i - 1, 0) * Q_BLOCK, Q_BLOCK),
            c0=pl.multiple_of(i * Q_BLOCK, Q_BLOCK),
            n0=pl.multiple_of(jnp.minimum(i + 1, N_QBLK - 1) * Q_BLOCK, Q_BLOCK),
            ok_prev=key >= qry + jnp.where(i > 0, 0, Q_BLOCK),
            ok_next=key <= qry - jnp.where(i < N_QBLK - 1, 0, Q_BLOCK)))

    def scores(blk, g):
        ks = slice(g * HEAD_DIM, (g + 1) * HEAD_DIM)
        q4 = jnp.concatenate(
            [q_ref[0, blk["rows"], (g * GROUP + j) * HEAD_DIM:(g * GROUP + j + 1) * HEAD_DIM]
             for j in range(GROUP)], axis=0)
        k_all = jnp.concatenate(
            [k_ref[0, pl.ds(blk["p0"], Q_BLOCK), ks], k_ref[0, pl.ds(blk["c0"], Q_BLOCK), ks],
             k_ref[0, pl.ds(blk["n0"], Q_BLOCK), ks], kx_ref[0, :, ks]], axis=0)
        return _dot_nt(k_all, q4)

    def softmax(blk, g, st):
        pieces = [jnp.where(blk["ok_prev"], st[:WINDOW], NEG_INF),
                  st[WINDOW:WINDOW + Q_BLOCK],
                  jnp.where(blk["ok_next"], st[WINDOW + Q_BLOCK:2 * WINDOW + Q_BLOCK], NEG_INF),
                  st[2 * WINDOW + Q_BLOCK:]]
        sink = jnp.zeros((1, cols), F32)
        for j in range(GROUP):
            sink = jnp.where(lane_head == j, sink_ref[g * GROUP + j], sink)
        m = sink
        for piece in pieces:
            m = jnp.maximum(m, jnp.max(piece, axis=0, keepdims=True))
        pt = jnp.concatenate([jnp.exp2((piece - m).astype(BF16)) for piece in pieces], axis=0)
        return pt, jnp.exp2(sink - m)

    def values(blk, g, pt, p_sink):
        ks = slice(g * HEAD_DIM, (g + 1) * HEAD_DIM)
        vt = jnp.concatenate(
            [vt_ref[0, ks, pl.ds(blk["p0"], Q_BLOCK)], vt_ref[0, ks, pl.ds(blk["c0"], Q_BLOCK)],
             vt_ref[0, ks, pl.ds(blk["n0"], Q_BLOCK)], vx_ref[0, ks, :]], axis=1)
        ot = _dot(jnp.concatenate([vt, ones], axis=0), pt)
        denom = ot[HEAD_DIM:HEAD_DIM + 1] + p_sink
        on = ot[:HEAD_DIM] * (1.0 / denom)
        for jj in range(GROUP // 2):
            pair_t = jnp.concatenate(
                [on[:, (2 * jj) * Q_BLOCK:(2 * jj + 1) * Q_BLOCK],
                 on[:, (2 * jj + 1) * Q_BLOCK:(2 * jj + 2) * Q_BLOCK]], axis=0)
            col = (g * GROUP + 2 * jj) * HEAD_DIM
            o_ref[0, blk["rows"], col:col + 2 * HEAD_DIM] = pair_t.T.astype(BF16)

    chains = [(blk, g) for blk in blocks for g in range(N_KV_HEADS)]
    st_next = scores(*chains[0])
    for n, chain in enumerate(chains):
        st = st_next
        if n + 1 < len(chains):
            st_next = scores(*chains[n + 1])
        values(*chain, *softmax(*chain, st))


def _attn(sink, q, k, vt, k_ctx, vt_ctx):
    qspec = pl.BlockSpec((1, ATTN_TILE, ATTN_WIDTH), lambda b, i: (b, i, 0))
    return pl.pallas_call(
        _attn_kernel,
        grid=(BATCH, SEQ // ATTN_TILE),
        in_specs=[pl.BlockSpec(memory_space=pltpu.SMEM), qspec,
                  pl.BlockSpec((1, SEQ, KV_WIDTH), lambda b, i: (b, 0, 0)),
                  pl.BlockSpec((1, KV_WIDTH, SEQ), lambda b, i: (b, 0, 0)),
                  pl.BlockSpec((1, CTX_LEN, KV_WIDTH), lambda b, i: (b, 0, 0)),
                  pl.BlockSpec((1, KV_WIDTH, CTX_LEN), lambda b, i: (b, 0, 0))],
        out_specs=qspec,
        out_shape=jax.ShapeDtypeStruct((BATCH, SEQ, ATTN_WIDTH), BF16),
        compiler_params=_params("arbitrary", "arbitrary"),
        name="attn",
    )(sink, q, k, vt, k_ctx, vt_ctx)


def _merge_kernel(x_ref, oa_ref, z_ref, zp_ref, zn_ref, bg_ref, ga_ref, gc_ref, cw_ref,
                  w3_ref, g1_ref, sh2_ref, sc2_ref, n2_ref, wr_ref,
                  x1_ref, h2_ref, lg_ref):
    t = pl.program_id(1)
    nt = pl.num_programs(1)
    z_before = zp_ref[0, BF16_ROWS - 1:BF16_ROWS, :].astype(F32) * jnp.where(t > 0, 1.0, 0.0)
    z_after = zn_ref[0, 0:1, :].astype(F32) * jnp.where(t < nt - 1, 1.0, 0.0)
    row = lax.broadcasted_iota(jnp.int32, (F32_ROWS, 1), 0)
    n_sub = MERGE_TILE // SUB_TILE
    subs = [slice(r * SUB_TILE, (r + 1) * SUB_TILE) for r in range(n_sub)]

    def conv(r):
        lo, hi = subs[r].start, subs[r].stop
        z = z_ref[0, subs[r], :].astype(F32)
        before = z_before if r == 0 else z_ref[0, lo - 1:lo, :].astype(F32)
        after = z_after if r == n_sub - 1 else z_ref[0, hi:hi + 1, :].astype(F32)
        up = pltpu.roll(z, 1, 0)
        z_prev = jnp.concatenate(
            [jnp.where(row == 0, before, up[:F32_ROWS]), up[F32_ROWS:]], axis=0)
        dn = pltpu.roll(z, SUB_TILE - 1, 0)
        z_next = jnp.concatenate(
            [dn[:-F32_ROWS], jnp.where(row == F32_ROWS - 1, after, dn[-F32_ROWS:])], axis=0)
        y = bg_ref[0, subs[r], :].astype(F32) * (
            cw_ref[0:1, :] * z_prev + cw_ref[1:2, :] * z + cw_ref[2:3, :] * z_next)
        return y.astype(BF16)

    a, s = [], []
    for r in range(n_sub):
        a.append(_dot(oa_ref[0, subs[r], :], w3_ref[:, 0:D_MODEL]))
        s.append(_dot(conv(r), w3_ref[:, D_MODEL:2 * D_MODEL]))
    branch = []
    for r, rs in enumerate(subs):
        merged = ga_ref[0, rs, :].astype(F32) * a[r] + gc_ref[0, rs, :].astype(F32) * s[r]
        branch.append(_dot(merged.astype(BF16), w3_ref[:, 2 * D_MODEL:3 * D_MODEL]))
    for r, rs in enumerate(subs):
        x1 = x_ref[0, rs, :] + g1_ref[0] * branch[r]
        x1_ref[0, rs, :] = x1
        h2 = _norm_mod(x1, n2_ref[...], sh2_ref[0], sc2_ref[0])
        h_hi, h_lo = _split_bf16(h2)
        h2_ref[0, rs, :] = h_hi
        both = _dot_nt(wr_ref[...], h_hi)
        lg_ref[0, :, rs] = (both[:N_EXPERTS] + both[N_EXPERTS:]
                            + _dot_nt(wr_ref[:N_EXPERTS], h_lo))


def _merge(x, oa, z, bg, ga, gc, conv_w, w3, g1, sh2, sc2, n2, wr):
    nt = SEQ // MERGE_TILE
    hb = MERGE_TILE // BF16_ROWS
    n_hb = SEQ // BF16_ROWS
    tile = pl.BlockSpec((1, MERGE_TILE, D_MODEL), lambda b, t: (b, t, 0))
    halo_p = pl.BlockSpec((1, BF16_ROWS, D_MODEL), lambda b, t: (b, jnp.maximum(t * hb - 1, 0), 0))
    halo_n = pl.BlockSpec((1, BF16_ROWS, D_MODEL),
                          lambda b, t: (b, jnp.minimum((t + 1) * hb, n_hb - 1), 0))
    per_b = pl.BlockSpec((1, 1, D_MODEL), lambda b, t: (b, 0, 0))
    full = lambda r, c: pl.BlockSpec((r, c), lambda b, t: (0, 0))
    return pl.pallas_call(
        _merge_kernel,
        grid=(BATCH, nt),
        in_specs=[tile, tile, tile, halo_p, halo_n, tile, tile, tile,
                  full(3, D_MODEL), full(D_MODEL, W3_WIDTH),
                  per_b, per_b, per_b, full(1, D_MODEL), full(2 * N_EXPERTS, D_MODEL)],
        out_specs=[tile, tile, pl.BlockSpec((1, N_EXPERTS, MERGE_TILE), lambda b, t: (b, 0, t))],
        out_shape=[jax.ShapeDtypeStruct((BATCH, SEQ, D_MODEL), F32),
                   jax.ShapeDtypeStruct((BATCH, SEQ, D_MODEL), BF16),
                   jax.ShapeDtypeStruct((BATCH, N_EXPERTS, SEQ), F32)],
        compiler_params=_params("arbitrary", "arbitrary"),
        name="merge",
    )(x, oa, z, z, z, bg, ga, gc, conv_w, w3, g1, sh2, sc2, n2, wr)


def _cumsum_excl(mf, upper):
    nblk = SEQ // LANES
    rows = mf.shape[0]
    stacked = jnp.concatenate([mf[:, j * LANES:(j + 1) * LANES] for j in range(nblk)], axis=0)
    within = _dot(stacked.astype(BF16), upper)
    tot = jnp.sum(stacked, axis=1, keepdims=True)
    off = jnp.zeros((rows, 1), F32)
    out = []
    for j in range(nblk):
        rs = slice(j * rows, (j + 1) * rows)
        out.append(within[rs] + off)
        off = off + tot[rs]
    return jnp.concatenate(out, axis=1)


def _route_kernel(lg_ref, pos_ref, gate_ref, band_ref):
    rows = ROUTE_B * N_EXPERTS
    lg = lg_ref[...]
    ex = jnp.exp(lg - jnp.max(lg, axis=1, keepdims=True))
    aff = (ex / jnp.sum(ex, axis=1, keepdims=True)).reshape(rows, SEQ)

    def body(_, carry):
        lo, hi = carry
        mid = lo + ((hi - lo) >> 1)
        cnt = jnp.sum(jnp.where(aff >= pltpu.bitcast(mid, F32), 1.0, 0.0), axis=1, keepdims=True)
        ge = cnt >= CAP
        return jnp.where(ge, mid, lo), jnp.where(ge, hi, mid)

    one_bits = 0x3F800000
    lo0 = jnp.zeros((rows, 1), jnp.int32)
    hi0 = jnp.full((rows, 1), one_bits + 1, jnp.int32)
    lo, _ = lax.fori_loop(0, 31, body, (lo0, hi0))
    th = pltpu.bitcast(lo, F32)

    ku = lax.broadcasted_iota(jnp.int32, (LANES, LANES), 0)
    nu = lax.broadcasted_iota(jnp.int32, (LANES, LANES), 1)
    upper = jnp.where(ku < nu, 1.0, 0.0).astype(BF16)
    gtf = jnp.where(aff > th, 1.0, 0.0)
    eqf = jnp.where(aff == th, 1.0, 0.0)
    need = CAP - jnp.sum(gtf, axis=1, keepdims=True)
    sel = gtf + eqf * jnp.where(_cumsum_excl(eqf, upper) < need, 1.0, 0.0)
    pos = _cumsum_excl(sel, upper)
    pos_ref[...] = jnp.where(sel > 0.0, pos, -1.0).reshape(ROUTE_B, N_EXPERTS, SEQ)
    gate_ref[...] = jnp.where(sel > 0.0, aff, 0.0).reshape(ROUTE_B, N_EXPERTS, SEQ)
    count = pos + sel
    tok_last_lo = jnp.sum(jnp.where(count <= CAP // 2 - 1, 1.0, 0.0), axis=1, keepdims=True)
    tok_first_hi = jnp.sum(jnp.where(count <= CAP // 2, 1.0, 0.0), axis=1, keepdims=True)
    banded = jnp.logical_and(tok_last_lo < GATHER_LO_END, tok_first_hi >= GATHER_HI_START)
    band_ref[...] = jnp.where(banded, 1, 0).astype(jnp.int32).reshape(ROUTE_B, N_EXPERTS, 1)


def _route(logits):
    spec = pl.BlockSpec((ROUTE_B, N_EXPERTS, SEQ), lambda b: (b, 0, 0))
    shape = jax.ShapeDtypeStruct((BATCH, N_EXPERTS, SEQ), F32)
    return pl.pallas_call(
        _route_kernel,
        grid=(BATCH // ROUTE_B,),
        in_specs=[spec],
        out_specs=[spec, spec, pl.BlockSpec((ROUTE_B, N_EXPERTS, 1), lambda b: (b, 0, 0))],
        out_shape=[shape, shape, jax.ShapeDtypeStruct((BATCH, N_EXPERTS, 1), jnp.int32)],
        compiler_params=_params("arbitrary"),
        name="route",
    )(logits)


def _ffn_kernel(band_ref, pos_ref, h_ref, wg_ref, wu_ref, wd_ref, yet_ref,
                wg_b, wu_b, wdt_b, dense_ref):
    e, grp = pl.program_id(0), pl.program_id(1)

    @pl.when(grp == 0)
    def _():
        wg_b[...] = wg_ref[0].astype(BF16)
        wu_b[...] = wu_ref[0].astype(BF16)
        wdt_b[...] = wd_ref[0].T.astype(BF16)

    def gather(i, slots, toks):
        n_slots, n_toks = slots.stop - slots.start, toks.stop - toks.start
        slot = (lax.broadcasted_iota(jnp.int32, (n_slots, n_toks), 0) + slots.start).astype(F32)
        onehot = jnp.where(slot == pos_ref[i, 0, :, toks], 1.0, 0.0).astype(BF16)
        return _dot(onehot, h_ref[i, toks, :]).astype(BF16)

    @pl.when(jnp.logical_and(e == 0, grp == 0))
    def _():
        dense_ref[...] = jnp.zeros_like(dense_ref)

    banded = [band_ref[(grp * FFN_B + i) * N_EXPERTS + e] == 1 for i in range(FFN_B)]
    for i in range(FFN_B):
        @pl.when(jnp.logical_not(banded[i]))
        def _(i=i):
            dense_ref[i] = gather(i, slice(0, CAP), slice(0, SEQ))

    xes = []
    for i in range(FFN_B):
        windowed = jnp.concatenate(
            [gather(i, slice(0, CAP // 2), slice(0, GATHER_LO_END)),
             gather(i, slice(CAP // 2, CAP), slice(GATHER_HI_START, SEQ))], axis=0)
        xes.append(jnp.where(banded[i], windowed, dense_ref[i]))
    xe = jnp.concatenate(xes, axis=0)
    act = _dot(xe, wg_b[...])
    up = _dot(xe, wu_b[...])
    hid = (act * _sigmoid(act) * up).astype(BF16)
    ye_t = _dot_nt(wdt_b[...], hid)
    for i in range(FFN_B):
        yet_ref[i, 0] = ye_t[:, i * CAP:i * CAP + CAP // 2].astype(BF16)
        yet_ref[i, 1] = ye_t[:, i * CAP + CAP // 2:(i + 1) * CAP].astype(BF16)


def _ffn(band, pos, h2, wg, wu, wd):
    n_grp = BATCH // FFN_B

    def wspec(lead):
        return pl.BlockSpec(
            (1, D_MODEL, EXPERT_FF),
            lambda e, g: (jnp.minimum(e + jnp.where(g >= n_grp - lead, 1, 0), N_EXPERTS - 1),
                          0, 0))
    wscr = pltpu.VMEM((D_MODEL, EXPERT_FF), BF16)
    return pl.pallas_call(
        _ffn_kernel,
        grid=(N_EXPERTS, n_grp),
        in_specs=[pl.BlockSpec(memory_space=pltpu.SMEM),
                  pl.BlockSpec((FFN_B, 1, 1, SEQ), lambda e, g: (g, e, 0, 0)),
                  pl.BlockSpec((FFN_B, SEQ, D_MODEL), lambda e, g: (g, 0, 0)),
                  wspec(3), wspec(2), wspec(1)],
        out_specs=pl.BlockSpec((FFN_B, 2, D_MODEL, CAP // 2), lambda e, g: (g, 0, 0, e)),
        out_shape=jax.ShapeDtypeStruct((BATCH, 2, D_MODEL, N_EXPERTS * CAP // 2), BF16),
        scratch_shapes=[wscr, wscr, wscr, pltpu.VMEM((FFN_B, CAP, D_MODEL), BF16)],
        compiler_params=_params("arbitrary", "arbitrary"),
        name="ffn",
    )(band.reshape(-1), pos[:, :, None, :], h2, wg, wu, wd)


def _combine_kernel(band_ref, pos_ref, gate_ref, yet_ref, x1_ref, g2_ref, fg_ref, o_ref,
                    other_ref):
    b, step = pl.program_id(0), pl.program_id(1)
    n_banded = band_ref[b * N_EXPERTS]
    for ee in range(1, N_EXPERTS):
        n_banded = n_banded + band_ref[b * N_EXPERTS + ee]
    all_banded = n_banded == N_EXPERTS

    @pl.when(jnp.logical_and(b == 0, step == 0))
    def _():
        other_ref[...] = jnp.zeros_like(other_ref)

    def half(cols, plane):
        slot = (lax.broadcasted_iota(jnp.int32, (CAP // 2, SCAT_TOK), 0)
                + plane * (CAP // 2)).astype(F32)
        scat = jnp.concatenate(
            [jnp.where(slot == pos_ref[0, ee:ee + 1, cols], gate_ref[0, ee:ee + 1, cols], 0.0)
             .astype(BF16) for ee in range(N_EXPERTS)], axis=0)
        return _dot(yet_ref[0, plane], scat)

    chains = []
    for j in range(SCAT_PER_STEP):
        c = step * SCAT_PER_STEP + j
        cols = slice(j * SCAT_TOK, (j + 1) * SCAT_TOK)
        lo_only = jnp.logical_and(all_banded, (c + 1) * SCAT_TOK <= GATHER_HI_START)
        hi_only = jnp.logical_and(all_banded, c * SCAT_TOK >= GATHER_LO_END)
        both = jnp.logical_not(jnp.logical_or(lo_only, hi_only))
        chains.append((j, cols, hi_only, both))

    for j, cols, _, both in chains:
        @pl.when(both)
        def _(j=j, cols=cols):
            other_ref[j] = half(cols, 1)

    moe_ts = [half(cols, jnp.where(hi_only, 1, 0)) + jnp.where(both, other_ref[j], 0.0)
              for j, cols, hi_only, both in chains]
    for (j, cols, _, _), moe_t in zip(chains, moe_ts):
        x2 = x1_ref[0, cols, :] + g2_ref[0] * moe_t.T
        o_ref[0, cols, :] = (x2 * lax.rsqrt(jnp.mean(x2 * x2, axis=-1, keepdims=True) + EPS)
                             * fg_ref[...])


def _combine(band, pos, gate, yet, x1, g2, fg):
    step_tok = SCAT_PER_STEP * SCAT_TOK
    sel_c = pl.BlockSpec((1, N_EXPERTS, step_tok), lambda b, c: (b, 0, c))
    tok_c = pl.BlockSpec((1, step_tok, D_MODEL), lambda b, c: (b, c, 0))
    yet_spec = pl.BlockSpec((1, 2, D_MODEL, N_EXPERTS * CAP // 2), lambda b, c: (b, 0, 0, 0))
    return pl.pallas_call(
        _combine_kernel,
        grid=(BATCH, N_SCAT // SCAT_PER_STEP),
        in_specs=[pl.BlockSpec(memory_space=pltpu.SMEM), sel_c, sel_c, yet_spec,
                  tok_c,
                  pl.BlockSpec((1, 1, D_MODEL), lambda b, c: (b, 0, 0)),
                  pl.BlockSpec((1, D_MODEL), lambda b, c: (0, 0))],
        out_specs=tok_c,
        out_shape=jax.ShapeDtypeStruct((BATCH, SEQ, D_MODEL), F32),
        scratch_shapes=[pltpu.VMEM((SCAT_PER_STEP, D_MODEL, SCAT_TOK), F32)],
        compiler_params=_params("arbitrary", "arbitrary"),
        name="combine",
    )(band.reshape(-1), pos, gate, yet, x1, g2, fg)


def _rope_tables():
    rows = SEQ // GRID_W
    row = np.repeat(np.arange(rows, dtype=np.float32), GRID_W)
    col = np.tile(np.arange(GRID_W, dtype=np.float32), rows)
    n_freq = HEAD_DIM // 4
    inv_freq = (ROPE_BASE ** (-np.arange(n_freq, dtype=np.float32) / n_freq)).astype(np.float32)
    ang_r = row[:, None] * inv_freq[None, :]
    ang_c = col[:, None] * inv_freq[None, :]
    ang = np.concatenate([ang_r, ang_r, ang_c, ang_c], axis=-1)
    sign = np.tile(np.concatenate([-np.ones(n_freq, np.float32), np.ones(n_freq, np.float32)]), 2)
    reps = LANES // HEAD_DIM
    cos = np.tile(np.cos(ang).astype(np.float32), (1, reps))
    sin = np.tile((np.sin(ang) * sign).astype(np.float32), (1, reps))
    return jnp.asarray(cos), jnp.asarray(sin)


def kernel(x, c, ctx, c_ctx, w_mod, b_mod, norm1_g, w_in, attn_sink, conv_w, w_proj_attn,
           w_proj_conv, w_out, norm2_g, w_router, w_exp_gate, w_exp_up, w_exp_down, final_norm_g):
    assert x.shape == (BATCH, SEQ, D_MODEL) and ctx.shape == (BATCH, CTX_LEN, D_MODEL)
    assert w_mod.shape[0] == 1, "single-layer problem"
    layer = 0

    cvec = jnp.concatenate(
        [c, c_ctx[None, :], jnp.zeros((MOD_ROWS - BATCH - 1, D_MODEL), F32)], axis=0)
    mod = _mod(cvec, w_mod[layer], b_mod[layer][None, :])
    chunk = lambda rows, k: rows[:, k * D_MODEL:(k + 1) * D_MODEL]
    mod_x = mod[:BATCH][:, None, :]
    sh1, sc1, g1, sh2, sc2, g2 = (mod_x[..., k * D_MODEL:(k + 1) * D_MODEL] for k in range(N_MOD))
    mod_c = mod[BATCH:BATCH + 1]
    csh1, csc1 = chunk(mod_c, 0), chunk(mod_c, 1)

    n1 = norm1_g[layer][None, :]
    w_in_b = w_in[layer].astype(BF16)
    k_ctx, vt_ctx = _ctx_kv(ctx, n1, csh1, csc1, w_in_b[:, OFF_K:OFF_U])

    cos, sin = _rope_tables()
    q, k, vt, z, bg, ga, gc = _proj(x, n1, sh1, sc1, w_in_b, cos, sin)
    sink_log2 = attn_sink[layer].reshape(-1) * LOG2E
    o_attn = _attn(sink_log2, q, k, vt, k_ctx, vt_ctx)

    wr_hi, wr_lo = _split_bf16(w_router[layer].T)
    w3 = jnp.concatenate(
        [w_proj_attn[layer], w_proj_conv[layer], w_out[layer],
         jnp.zeros((D_MODEL, W3_WIDTH - 3 * D_MODEL), F32)], axis=1).astype(BF16)
    x1, h2, logits = _merge(
        x, o_attn, z, bg, ga, gc, conv_w[layer], w3,
        g1, sh2, sc2, norm2_g[layer][None, :], jnp.concatenate([wr_hi, wr_lo], axis=0))

    pos, gate, band = _route(logits)
    yet = _ffn(band, pos, h2, w_exp_gate[layer], w_exp_up[layer], w_exp_down[layer])
    return _combine(band, pos, gate, yet, x1, g2, final_norm_g[None, :])
```

```python
import jax
import jax.numpy as jnp
import numpy as np
from jax import lax
from jax.experimental import pallas as pl
from jax.experimental.pallas import tpu as pltpu

D_MODEL = 1024
BATCH = 16
SEQ = 2048
CTX_LEN = 256
GRID_W = 64
N_HEADS = 16
N_KV_HEADS = 4
GROUP = N_HEADS // N_KV_HEADS
HEAD_DIM = D_MODEL // N_HEADS
ATTN_WIDTH = N_HEADS * HEAD_DIM
KV_WIDTH = N_KV_HEADS * HEAD_DIM
WINDOW = 128
Q_BLOCK = 128
ROPE_BASE = 10000.0
CONV_WIDTH = D_MODEL
N_EXPERTS = 16
EXPERT_FF = D_MODEL
CAPACITY_FACTOR = 2
N_MOD = 6
EPS = 1e-6
NEG_INF = -1e30
LOG2E = 1.4426950408889634

CAP = CAPACITY_FACTOR * SEQ // N_EXPERTS
N_QBLK = SEQ // Q_BLOCK
LANES = 128
F32_ROWS = 8
BF16_ROWS = 16
W3_WIDTH = 3 * D_MODEL + LANES
TOK_TILE = 1024
PROJ_SUB = 256
ATTN_TILE = 1024
MERGE_TILE = 1024
SUB_TILE = 256
CTX_B = 4
ROUTE_B = 8
FFN_B = 2
GATHER_LO_END = 5 * SEQ // 8
GATHER_HI_START = 3 * SEQ // 8
N_SCAT = 8
SCAT_TOK = SEQ // N_SCAT
SCAT_PER_STEP = 4
MOD_ROWS = 32
VMEM_LIMIT = 56 * 1024 * 1024

OFF_Q = 0
OFF_K = OFF_Q + ATTN_WIDTH
OFF_V = OFF_K + KV_WIDTH
OFF_U = OFF_V + KV_WIDTH
OFF_B = OFF_U + CONV_WIDTH
OFF_C = OFF_B + CONV_WIDTH
OFF_GA = OFF_C + CONV_WIDTH
OFF_GC = OFF_GA + D_MODEL
IN_WIDTH = OFF_GC + D_MODEL

F32 = jnp.float32
BF16 = jnp.bfloat16


def _params(*sem):
    return pltpu.CompilerParams(dimension_semantics=sem, vmem_limit_bytes=VMEM_LIMIT)


def _dot(a, b):
    return jnp.dot(a, b, preferred_element_type=F32)


def _dot_nt(a, b):
    return lax.dot_general(a, b, (((1,), (1,)), ((), ())), preferred_element_type=F32)


def _sigmoid(x):
    return 1.0 / (1.0 + jnp.exp(-x))


def _split_bf16(x):
    hi = x.astype(BF16)
    lo = (x - hi.astype(F32)).astype(BF16)
    return hi, lo


def _norm_mod(x, g, shift, scale):
    y = x * lax.rsqrt(jnp.mean(x * x, axis=-1, keepdims=True) + EPS) * g
    return y * (1.0 + scale) + shift


def _mod_kernel(c_ref, w_ref, b_ref, o_ref):
    cv = c_ref[...]
    s_hi, s_lo = _split_bf16(cv * _sigmoid(cv))
    w_hi, w_lo = _split_bf16(w_ref[...])
    o_ref[...] = _dot(s_hi, w_hi) + _dot(s_lo, w_hi) + _dot(s_hi, w_lo) + b_ref[...]


def _mod(cvec, w_mod, b_mod):
    n_out = N_MOD * D_MODEL
    blk = 2 * D_MODEL
    return pl.pallas_call(
        _mod_kernel,
        grid=(n_out // blk,),
        in_specs=[
            pl.BlockSpec((MOD_ROWS, D_MODEL), lambda j: (0, 0)),
            pl.BlockSpec((D_MODEL, blk), lambda j: (0, j)),
            pl.BlockSpec((1, blk), lambda j: (0, j)),
        ],
        out_specs=pl.BlockSpec((MOD_ROWS, blk), lambda j: (0, j)),
        out_shape=jax.ShapeDtypeStruct((MOD_ROWS, n_out), F32),
        compiler_params=_params("arbitrary"),
        name="mod",
    )(cvec, w_mod, b_mod)


def _ctx_kernel(ctx_ref, g_ref, sh_ref, sc_ref, w_ref, k_ref, vt_ref):
    x = ctx_ref[...].reshape(CTX_B * CTX_LEN, D_MODEL)
    h = _norm_mod(x, g_ref[...], sh_ref[...], sc_ref[...]).astype(BF16)
    kv = _dot(h, w_ref[...])
    k_ref[...] = kv[:, :KV_WIDTH].astype(BF16).reshape(CTX_B, CTX_LEN, KV_WIDTH)
    for i in range(CTX_B):
        vt_ref[i] = kv[i * CTX_LEN:(i + 1) * CTX_LEN, KV_WIDTH:].T.astype(BF16)


def _ctx_kv(ctx, g1, csh1, csc1, w_in):
    row = pl.BlockSpec((1, D_MODEL), lambda b: (0, 0))
    assert OFF_K % (2 * KV_WIDTH) == 0
    return pl.pallas_call(
        _ctx_kernel,
        grid=(BATCH // CTX_B,),
        in_specs=[
            pl.BlockSpec((CTX_B, CTX_LEN, D_MODEL), lambda b: (b, 0, 0)),
            row, row, row,
            pl.BlockSpec((D_MODEL, 2 * KV_WIDTH), lambda b: (0, OFF_K // (2 * KV_WIDTH))),
        ],
        out_specs=[pl.BlockSpec((CTX_B, CTX_LEN, KV_WIDTH), lambda b: (b, 0, 0)),
                   pl.BlockSpec((CTX_B, KV_WIDTH, CTX_LEN), lambda b: (b, 0, 0))],
        out_shape=[jax.ShapeDtypeStruct((BATCH, CTX_LEN, KV_WIDTH), BF16),
                   jax.ShapeDtypeStruct((BATCH, KV_WIDTH, CTX_LEN), BF16)],
        compiler_params=_params("arbitrary"),
        name="ctx_kv",
    )(ctx, g1, csh1, csc1, w_in)


def _proj_kernel(x_ref, g_ref, sh_ref, sc_ref, w_ref, cos_ref, sin_ref,
                 q_ref, k_ref, vt_ref, z_ref, bg_ref, ga_ref, gc_ref):
    lane = lax.broadcasted_iota(jnp.int32, (PROJ_SUB, LANES), 1)
    first_half = (lane & (HEAD_DIM // 4)) == 0
    scale = HEAD_DIM ** -0.5 * LOG2E

    for r in range(TOK_TILE // PROJ_SUB):
        rs = slice(r * PROJ_SUB, (r + 1) * PROJ_SUB)
        h = _norm_mod(x_ref[0, rs, :], g_ref[...], sh_ref[0], sc_ref[0]).astype(BF16)
        cos = cos_ref[rs, :]
        sin = sin_ref[rs, :]

        def rope(a):
            rot = jnp.where(first_half,
                            pltpu.roll(a, LANES - HEAD_DIM // 4, 1),
                            pltpu.roll(a, HEAD_DIM // 4, 1))
            return a * cos + rot * sin

        def proj(off, width):
            return _dot(h, w_ref[:, off:off + width])

        qa = proj(OFF_Q, ATTN_WIDTH)
        for j in range(ATTN_WIDTH // LANES):
            sl = slice(j * LANES, (j + 1) * LANES)
            q_ref[0, rs, sl] = (rope(qa[:, sl]) * scale).astype(BF16)
        ka = proj(OFF_K, KV_WIDTH)
        for j in range(KV_WIDTH // LANES):
            sl = slice(j * LANES, (j + 1) * LANES)
            k_ref[0, rs, sl] = rope(ka[:, sl]).astype(BF16)
        vt_ref[0, :, rs] = proj(OFF_V, KV_WIDTH).T.astype(BF16)
        z_ref[0, rs, :] = (proj(OFF_C, CONV_WIDTH) * proj(OFF_U, CONV_WIDTH)).astype(BF16)
        bg_ref[0, rs, :] = proj(OFF_B, CONV_WIDTH).astype(BF16)
        ga_ref[0, rs, :] = _sigmoid(proj(OFF_GA, D_MODEL)).astype(BF16)
        gc_ref[0, rs, :] = _sigmoid(proj(OFF_GC, D_MODEL)).astype(BF16)


def _proj(x, g1, sh1, sc1, w_in, cos, sin):
    nt = SEQ // TOK_TILE
    tile = lambda w: pl.BlockSpec((1, TOK_TILE, w), lambda b, t: (b, t, 0))
    per_b = pl.BlockSpec((1, 1, D_MODEL), lambda b, t: (b, 0, 0))
    tab = pl.BlockSpec((TOK_TILE, LANES), lambda b, t: (t, 0))
    shp = lambda w: jax.ShapeDtypeStruct((BATCH, SEQ, w), BF16)
    vt_spec = pl.BlockSpec((1, KV_WIDTH, TOK_TILE), lambda b, t: (b, 0, t))
    vt_shape = jax.ShapeDtypeStruct((BATCH, KV_WIDTH, SEQ), BF16)
    return pl.pallas_call(
        _proj_kernel,
        grid=(BATCH, nt),
        in_specs=[
            tile(D_MODEL),
            pl.BlockSpec((1, D_MODEL), lambda b, t: (0, 0)),
            per_b, per_b,
            pl.BlockSpec((D_MODEL, IN_WIDTH), lambda b, t: (0, 0), pipeline_mode=pl.Buffered(1)),
            tab, tab,
        ],
        out_specs=[tile(ATTN_WIDTH), tile(KV_WIDTH), vt_spec, tile(CONV_WIDTH), tile(CONV_WIDTH),
                   tile(D_MODEL), tile(D_MODEL)],
        out_shape=[shp(ATTN_WIDTH), shp(KV_WIDTH), vt_shape, shp(CONV_WIDTH), shp(CONV_WIDTH),
                   shp(D_MODEL), shp(D_MODEL)],
        compiler_params=_params("arbitrary", "arbitrary"),
        name="proj",
    )(x, g1, sh1, sc1, w_in, cos, sin)


def _attn_kernel(sink_ref, q_ref, k_ref, vt_ref, kx_ref, vx_ref, o_ref):
    cols = GROUP * Q_BLOCK
    key = lax.broadcasted_iota(jnp.int32, (Q_BLOCK, cols), 0)
    qry = lax.broadcasted_iota(jnp.int32, (Q_BLOCK, cols), 1) & (Q_BLOCK - 1)
    lane_head = lax.broadcasted_iota(jnp.int32, (1, cols), 1) >> 7
    n_keys = Q_BLOCK + 2 * WINDOW + CTX_LEN
    ones = jnp.ones((BF16_ROWS, n_keys), BF16)

    blocks = []
    for sb in range(ATTN_TILE // Q_BLOCK):
        i = pl.program_id(1) * (ATTN_TILE // Q_BLOCK) + sb
        blocks.append(dict(
            rows=slice(sb * Q_BLOCK, (sb + 1) * Q_BLOCK),
            p0=pl.multiple_of(jnp.maximum(i - 1, 0) * Q_BLOCK, Q_BLOCK),
            c0=pl.multiple_of(i * Q_BLOCK, Q_BLOCK),
            n0=pl.multiple_of(jnp.minimum(i + 1, N_QBLK - 1) * Q_BLOCK, Q_BLOCK),
            ok_prev=key >= qry + jnp.where(i > 0, 0, Q_BLOCK),
            ok_next=key <= qry - jnp.where(i < N_QBLK - 1, 0, Q_BLOCK)))

    def scores(blk, g):
        ks = slice(g * HEAD_DIM, (g + 1) * HEAD_DIM)
        q4 = jnp.concatenate(
            [q_ref[0, blk["rows"], (g * GROUP + j) * HEAD_DIM:(g * GROUP + j + 1) * HEAD_DIM]
             for j in range(GROUP)], axis=0)
        k_all = jnp.concatenate(
            [k_ref[0, pl.ds(blk["p0"], Q_BLOCK), ks], k_ref[0, pl.ds(blk["c0"], Q_BLOCK), ks],
             k_ref[0, pl.ds(blk["n0"], Q_BLOCK), ks], kx_ref[0, :, ks]], axis=0)
        return _dot_nt(k_all, q4)

    def softmax(blk, g, st):
        pieces = [jnp.where(blk["ok_prev"], st[:WINDOW], NEG_INF),
                  st[WINDOW:WINDOW + Q_BLOCK],
                  jnp.where(blk["ok_next"], st[WINDOW + Q_BLOCK:2 * WINDOW + Q_BLOCK], NEG_INF),
                  st[2 * WINDOW + Q_BLOCK:]]
        sink = jnp.zeros((1, cols), F32)
        for j in range(GROUP):
            sink = jnp.where(lane_head == j, sink_ref[g * GROUP + j], sink)
        m = sink
        for piece in pieces:
            m = jnp.maximum(m, jnp.max(piece, axis=0, keepdims=True))
        pt = jnp.concatenate([jnp.exp2((piece - m).astype(BF16)) for piece in pieces], axis=0)
        return pt, jnp.exp2(sink - m)

    def values(blk, g, pt, p_sink):
        ks = slice(g * HEAD_DIM, (g + 1) * HEAD_DIM)
        vt = jnp.concatenate(
            [vt_ref[0, ks, pl.ds(blk["p0"], Q_BLOCK)], vt_ref[0, ks, pl.ds(blk["c0"], Q_BLOCK)],
             vt_ref[0, ks, pl.ds(blk["n0"], Q_BLOCK)], vx_ref[0, ks, :]], axis=1)
        ot = _dot(jnp.concatenate([vt, ones], axis=0), pt)
        denom = ot[HEAD_DIM:HEAD_DIM + 1] + p_sink
        on = ot[:HEAD_DIM] * (1.0 / denom)
        for jj in range(GROUP // 2):
            pair_t = jnp.concatenate(
                [on[:, (2 * jj) * Q_BLOCK:(2 * jj + 1) * Q_BLOCK],
                 on[:, (2 * jj + 1) * Q_BLOCK:(2 * jj + 2) * Q_BLOCK]], axis=0)
            col = (g * GROUP + 2 * jj) * HEAD_DIM
            o_ref[0, blk["rows"], col:col + 2 * HEAD_DIM] = pair_t.T.astype(BF16)

    chains = [(blk, g) for blk in blocks for g in range(N_KV_HEADS)]
    st_next = scores(*chains[0])
    for n, chain in enumerate(chains):
        st = st_next
        if n + 1 < len(chains):
            st_next = scores(*chains[n + 1])
        values(*chain, *softmax(*chain, st))


def _attn(sink, q, k, vt, k_ctx, vt_ctx):
    qspec = pl.BlockSpec((1, ATTN_TILE, ATTN_WIDTH), lambda b, i: (b, i, 0))
    return pl.pallas_call(
        _attn_kernel,
        grid=(BATCH, SEQ // ATTN_TILE),
        in_specs=[pl.BlockSpec(memory_space=pltpu.SMEM), qspec,
                  pl.BlockSpec((1, SEQ, KV_WIDTH), lambda b, i: (b, 0, 0)),
                  pl.BlockSpec((1, KV_WIDTH, SEQ), lambda b, i: (b, 0, 0)),
                  pl.BlockSpec((1, CTX_LEN, KV_WIDTH), lambda b, i: (b, 0, 0)),
                  pl.BlockSpec((1, KV_WIDTH, CTX_LEN), lambda b, i: (b, 0, 0))],
        out_specs=qspec,
        out_shape=jax.ShapeDtypeStruct((BATCH, SEQ, ATTN_WIDTH), BF16),
        compiler_params=_params("arbitrary", "arbitrary"),
        name="attn",
    )(sink, q, k, vt, k_ctx, vt_ctx)


def _merge_kernel(x_ref, oa_ref, z_ref, zp_ref, zn_ref, bg_ref, ga_ref, gc_ref, cw_ref,
                  w3_ref, g1_ref, sh2_ref, sc2_ref, n2_ref, wr_ref,
                  x1_ref, h2_ref, lg_ref):
    t = pl.program_id(1)
    nt = pl.num_programs(1)
    z_before = zp_ref[0, BF16_ROWS - 1:BF16_ROWS, :].astype(F32) * jnp.where(t > 0, 1.0, 0.0)
    z_after = zn_ref[0, 0:1, :].astype(F32) * jnp.where(t < nt - 1, 1.0, 0.0)
    row = lax.broadcasted_iota(jnp.int32, (F32_ROWS, 1), 0)
    n_sub = MERGE_TILE // SUB_TILE
    subs = [slice(r * SUB_TILE, (r + 1) * SUB_TILE) for r in range(n_sub)]

    def conv(r):
        lo, hi = subs[r].start, subs[r].stop
        z = z_ref[0, subs[r], :].astype(F32)
        before = z_before if r == 0 else z_ref[0, lo - 1:lo, :].astype(F32)
        after = z_after if r == n_sub - 1 else z_ref[0, hi:hi + 1, :].astype(F32)
        up = pltpu.roll(z, 1, 0)
        z_prev = jnp.concatenate(
            [jnp.where(row == 0, before, up[:F32_ROWS]), up[F32_ROWS:]], axis=0)
        dn = pltpu.roll(z, SUB_TILE - 1, 0)
        z_next = jnp.concatenate(
            [dn[:-F32_ROWS], jnp.where(row == F32_ROWS - 1, after, dn[-F32_ROWS:])], axis=0)
        y = bg_ref[0, subs[r], :].astype(F32) * (
            cw_ref[0:1, :] * z_prev + cw_ref[1:2, :] * z + cw_ref[2:3, :] * z_next)
        return y.astype(BF16)

    a, s = [], []
    for r in range(n_sub):
        a.append(_dot(oa_ref[0, subs[r], :], w3_ref[:, 0:D_MODEL]))
        s.append(_dot(conv(r), w3_ref[:, D_MODEL:2 * D_MODEL]))
    branch = []
    for r, rs in enumerate(subs):
        merged = ga_ref[0, rs, :].astype(F32) * a[r] + gc_ref[0, rs, :].astype(F32) * s[r]
        branch.append(_dot(merged.astype(BF16), w3_ref[:, 2 * D_MODEL:3 * D_MODEL]))
    for r, rs in enumerate(subs):
        x1 = x_ref[0, rs, :] + g1_ref[0] * branch[r]
        x1_ref[0, rs, :] = x1
        h2 = _norm_mod(x1, n2_ref[...], sh2_ref[0], sc2_ref[0])
        h_hi, h_lo = _split_bf16(h2)
        h2_ref[0, rs, :] = h_hi
        both = _dot_nt(wr_ref[...], h_hi)
        lg_ref[0, :, rs] = (both[:N_EXPERTS] + both[N_EXPERTS:]
                            + _dot_nt(wr_ref[:N_EXPERTS], h_lo))


def _merge(x, oa, z, bg, ga, gc, conv_w, w3, g1, sh2, sc2, n2, wr):
    nt = SEQ // MERGE_TILE
    hb = MERGE_TILE // BF16_ROWS
    n_hb = SEQ // BF16_ROWS
    tile = pl.BlockSpec((1, MERGE_TILE, D_MODEL), lambda b, t: (b, t, 0))
    halo_p = pl.BlockSpec((1, BF16_ROWS, D_MODEL), lambda b, t: (b, jnp.maximum(t * hb - 1, 0), 0))
    halo_n = pl.BlockSpec((1, BF16_ROWS, D_MODEL),
                          lambda b, t: (b, jnp.minimum((t + 1) * hb, n_hb - 1), 0))
    per_b = pl.BlockSpec((1, 1, D_MODEL), lambda b, t: (b, 0, 0))
    full = lambda r, c: pl.BlockSpec((r, c), lambda b, t: (0, 0))
    return pl.pallas_call(
        _merge_kernel,
        grid=(BATCH, nt),
        in_specs=[tile, tile, tile, halo_p, halo_n, tile, tile, tile,
                  full(3, D_MODEL), full(D_MODEL, W3_WIDTH),
                  per_b, per_b, per_b, full(1, D_MODEL), full(2 * N_EXPERTS, D_MODEL)],
        out_specs=[tile, tile, pl.BlockSpec((1, N_EXPERTS, MERGE_TILE), lambda b, t: (b, 0, t))],
        out_shape=[jax.ShapeDtypeStruct((BATCH, SEQ, D_MODEL), F32),
                   jax.ShapeDtypeStruct((BATCH, SEQ, D_MODEL), BF16),
                   jax.ShapeDtypeStruct((BATCH, N_EXPERTS, SEQ), F32)],
        compiler_params=_params("arbitrary", "arbitrary"),
        name="merge",
    )(x, oa, z, z, z, bg, ga, gc, conv_w, w3, g1, sh2, sc2, n2, wr)


def _cumsum_excl(mf, upper):
    nblk = SEQ // LANES
    rows = mf.shape[0]
    stacked = jnp.concatenate([mf[:, j * LANES:(j + 1) * LANES] for j in range(nblk)], axis=0)
    within = _dot(stacked.astype(BF16), upper)
    tot = jnp.sum(stacked, axis=1, keepdims=True)
    off = jnp.zeros((rows, 1), F32)
    out = []
    for j in range(nblk):
        rs = slice(j * rows, (j + 1) * rows)
        out.append(within[rs] + off)
        off = off + tot[rs]
    return jnp.concatenate(out, axis=1)


def _route_kernel(lg_ref, pos_ref, gate_ref, band_ref):
    rows = ROUTE_B * N_EXPERTS
    lg = lg_ref[...]
    ex = jnp.exp(lg - jnp.max(lg, axis=1, keepdims=True))
    aff = (ex / jnp.sum(ex, axis=1, keepdims=True)).reshape(rows, SEQ)

    def body(_, carry):
        lo, hi = carry
        mid = lo + ((hi - lo) >> 1)
        cnt = jnp.sum(jnp.where(aff >= pltpu.bitcast(mid, F32), 1.0, 0.0), axis=1, keepdims=True)
        ge = cnt >= CAP
        return jnp.where(ge, mid, lo), jnp.where(ge, hi, mid)

    one_bits = 0x3F800000
    lo0 = jnp.zeros((rows, 1), jnp.int32)
    hi0 = jnp.full((rows, 1), one_bits + 1, jnp.int32)
    lo, _ = lax.fori_loop(0, 31, body, (lo0, hi0))
    th = pltpu.bitcast(lo, F32)

    ku = lax.broadcasted_iota(jnp.int32, (LANES, LANES), 0)
    nu = lax.broadcasted_iota(jnp.int32, (LANES, LANES), 1)
    upper = jnp.where(ku < nu, 1.0, 0.0).astype(BF16)
    gtf = jnp.where(aff > th, 1.0, 0.0)
    eqf = jnp.where(aff == th, 1.0, 0.0)
    need = CAP - jnp.sum(gtf, axis=1, keepdims=True)
    sel = gtf + eqf * jnp.where(_cumsum_excl(eqf, upper) < need, 1.0, 0.0)
    pos = _cumsum_excl(sel, upper)
    pos_ref[...] = jnp.where(sel > 0.0, pos, -1.0).reshape(ROUTE_B, N_EXPERTS, SEQ)
    gate_ref[...] = jnp.where(sel > 0.0, aff, 0.0).reshape(ROUTE_B, N_EXPERTS, SEQ)
    count = pos + sel
    tok_last_lo = jnp.sum(jnp.where(count <= CAP // 2 - 1, 1.0, 0.0), axis=1, keepdims=True)
    tok_first_hi = jnp.sum(jnp.where(count <= CAP // 2, 1.0, 0.0), axis=1, keepdims=True)
    banded = jnp.logical_and(tok_last_lo < GATHER_LO_END, tok_first_hi >= GATHER_HI_START)
    band_ref[...] = jnp.where(banded, 1, 0).astype(jnp.int32).reshape(ROUTE_B, N_EXPERTS, 1)


def _route(logits):
    spec = pl.BlockSpec((ROUTE_B, N_EXPERTS, SEQ), lambda b: (b, 0, 0))
    shape = jax.ShapeDtypeStruct((BATCH, N_EXPERTS, SEQ), F32)
    return pl.pallas_call(
        _route_kernel,
        grid=(BATCH // ROUTE_B,),
        in_specs=[spec],
        out_specs=[spec, spec, pl.BlockSpec((ROUTE_B, N_EXPERTS, 1), lambda b: (b, 0, 0))],
        out_shape=[shape, shape, jax.ShapeDtypeStruct((BATCH, N_EXPERTS, 1), jnp.int32)],
        compiler_params=_params("arbitrary"),
        name="route",
    )(logits)


def _ffn_kernel(band_ref, pos_ref, h_ref, wg_ref, wu_ref, wd_ref, yet_ref,
                wg_b, wu_b, wdt_b, dense_ref):
    e, grp = pl.program_id(0), pl.program_id(1)

    @pl.when(grp == 0)
    def _():
        wg_b[...] = wg_ref[0].astype(BF16)
        wu_b[...] = wu_ref[0].astype(BF16)
        wdt_b[...] = wd_ref[0].T.astype(BF16)

    def gather(i, slots, toks):
        n_slots, n_toks = slots.stop - slots.start, toks.stop - toks.start
        slot = (lax.broadcasted_iota(jnp.int32, (n_slots, n_toks), 0) + slots.start).astype(F32)
        onehot = jnp.where(slot == pos_ref[i, 0, :, toks], 1.0, 0.0).astype(BF16)
        return _dot(onehot, h_ref[i, toks, :]).astype(BF16)

    @pl.when(jnp.logical_and(e == 0, grp == 0))
    def _():
        dense_ref[...] = jnp.zeros_like(dense_ref)

    banded = [band_ref[(grp * FFN_B + i) * N_EXPERTS + e] == 1 for i in range(FFN_B)]
    for i in range(FFN_B):
        @pl.when(jnp.logical_not(banded[i]))
        def _(i=i):
            dense_ref[i] = gather(i, slice(0, CAP), slice(0, SEQ))

    xes = []
    for i in range(FFN_B):
        windowed = jnp.concatenate(
            [gather(i, slice(0, CAP // 2), slice(0, GATHER_LO_END)),
             gather(i, slice(CAP // 2, CAP), slice(GATHER_HI_START, SEQ))], axis=0)
        xes.append(jnp.where(banded[i], windowed, dense_ref[i]))
    xe = jnp.concatenate(xes, axis=0)
    act = _dot(xe, wg_b[...])
    up = _dot(xe, wu_b[...])
    hid = (act * _sigmoid(act) * up).astype(BF16)
    ye_t = _dot_nt(wdt_b[...], hid)
    for i in range(FFN_B):
        yet_ref[i, 0] = ye_t[:, i * CAP:i * CAP + CAP // 2].astype(BF16)
        yet_ref[i, 1] = ye_t[:, i * CAP + CAP // 2:(i + 1) * CAP].astype(BF16)


def _ffn(band, pos, h2, wg, wu, wd):
    n_grp = BATCH // FFN_B

    def wspec(lead):
        return pl.BlockSpec(
            (1, D_MODEL, EXPERT_FF),
            lambda e, g: (jnp.minimum(e + jnp.where(g >= n_grp - lead, 1, 0), N_EXPERTS - 1),
                          0, 0))
    wscr = pltpu.VMEM((D_MODEL, EXPERT_FF), BF16)
    return pl.pallas_call(
        _ffn_kernel,
        grid=(N_EXPERTS, n_grp),
        in_specs=[pl.BlockSpec(memory_space=pltpu.SMEM),
                  pl.BlockSpec((FFN_B, 1, 1, SEQ), lambda e, g: (g, e, 0, 0)),
                  pl.BlockSpec((FFN_B, SEQ, D_MODEL), lambda e, g: (g, 0, 0)),
                  wspec(3), wspec(2), wspec(1)],
        out_specs=pl.BlockSpec((FFN_B, 2, D_MODEL, CAP // 2), lambda e, g: (g, 0, 0, e)),
        out_shape=jax.ShapeDtypeStruct((BATCH, 2, D_MODEL, N_EXPERTS * CAP // 2), BF16),
        scratch_shapes=[wscr, wscr, wscr, pltpu.VMEM((FFN_B, CAP, D_MODEL), BF16)],
        compiler_params=_params("arbitrary", "arbitrary"),
        name="ffn",
    )(band.reshape(-1), pos[:, :, None, :], h2, wg, wu, wd)


def _combine_kernel(band_ref, pos_ref, gate_ref, yet_ref, x1_ref, g2_ref, fg_ref, o_ref,
                    other_ref):
    b, step = pl.program_id(0), pl.program_id(1)
    n_banded = band_ref[b * N_EXPERTS]
    for ee in range(1, N_EXPERTS):
        n_banded = n_banded + band_ref[b * N_EXPERTS + ee]
    all_banded = n_banded == N_EXPERTS

    @pl.when(jnp.logical_and(b == 0, step == 0))
    def _():
        other_ref[...] = jnp.zeros_like(other_ref)

    def half(cols, plane):
        slot = (lax.broadcasted_iota(jnp.int32, (CAP // 2, SCAT_TOK), 0)
                + plane * (CAP // 2)).astype(F32)
        scat = jnp.concatenate(
            [jnp.where(slot == pos_ref[0, ee:ee + 1, cols], gate_ref[0, ee:ee + 1, cols], 0.0)
             .astype(BF16) for ee in range(N_EXPERTS)], axis=0)
        return _dot(yet_ref[0, plane], scat)

    chains = []
    for j in range(SCAT_PER_STEP):
        c = step * SCAT_PER_STEP + j
        cols = slice(j * SCAT_TOK, (j + 1) * SCAT_TOK)
        lo_only = jnp.logical_and(all_banded, (c + 1) * SCAT_TOK <= GATHER_HI_START)
        hi_only = jnp.logical_and(all_banded, c * SCAT_TOK >= GATHER_LO_END)
        both = jnp.logical_not(jnp.logical_or(lo_only, hi_only))
        chains.append((j, cols, hi_only, both))

    for j, cols, _, both in chains:
        @pl.when(both)
        def _(j=j, cols=cols):
            other_ref[j] = half(cols, 1)

    moe_ts = [half(cols, jnp.where(hi_only, 1, 0)) + jnp.where(both, other_ref[j], 0.0)
              for j, cols, hi_only, both in chains]
    for (j, cols, _, _), moe_t in zip(chains, moe_ts):
        x2 = x1_ref[0, cols, :] + g2_ref[0] * moe_t.T
        o_ref[0, cols, :] = (x2 * lax.rsqrt(jnp.mean(x2 * x2, axis=-1, keepdims=True) + EPS)
                             * fg_ref[...])


def _combine(band, pos, gate, yet, x1, g2, fg):
    step_tok = SCAT_PER_STEP * SCAT_TOK
    sel_c = pl.BlockSpec((1, N_EXPERTS, step_tok), lambda b, c: (b, 0, c))
    tok_c = pl.BlockSpec((1, step_tok, D_MODEL), lambda b, c: (b, c, 0))
    yet_spec = pl.BlockSpec((1, 2, D_MODEL, N_EXPERTS * CAP // 2), lambda b, c: (b, 0, 0, 0))
    return pl.pallas_call(
        _combine_kernel,
        grid=(BATCH, N_SCAT // SCAT_PER_STEP),
        in_specs=[pl.BlockSpec(memory_space=pltpu.SMEM), sel_c, sel_c, yet_spec,
                  tok_c,
                  pl.BlockSpec((1, 1, D_MODEL), lambda b, c: (b, 0, 0)),
                  pl.BlockSpec((1, D_MODEL), lambda b, c: (0, 0))],
        out_specs=tok_c,
        out_shape=jax.ShapeDtypeStruct((BATCH, SEQ, D_MODEL), F32),
        scratch_shapes=[pltpu.VMEM((SCAT_PER_STEP, D_MODEL, SCAT_TOK), F32)],
        compiler_params=_params("arbitrary", "arbitrary"),
        name="combine",
    )(band.reshape(-1), pos, gate, yet, x1, g2, fg)


def _rope_tables():
    rows = SEQ // GRID_W
    row = np.repeat(np.arange(rows, dtype=np.float32), GRID_W)
    col = np.tile(np.arange(GRID_W, dtype=np.float32), rows)
    n_freq = HEAD_DIM // 4
    inv_freq = (ROPE_BASE ** (-np.arange(n_freq, dtype=np.float32) / n_freq)).astype(np.float32)
    ang_r = row[:, None] * inv_freq[None, :]
    ang_c = col[:, None] * inv_freq[None, :]
    ang = np.concatenate([ang_r, ang_r, ang_c, ang_c], axis=-1)
    sign = np.tile(np.concatenate([-np.ones(n_freq, np.float32), np.ones(n_freq, np.float32)]), 2)
    reps = LANES // HEAD_DIM
    cos = np.tile(np.cos(ang).astype(np.float32), (1, reps))
    sin = np.tile((np.sin(ang) * sign).astype(np.float32), (1, reps))
    return jnp.asarray(cos), jnp.asarray(sin)


def kernel(x, c, ctx, c_ctx, w_mod, b_mod, norm1_g, w_in, attn_sink, conv_w, w_proj_attn,
           w_proj_conv, w_out, norm2_g, w_router, w_exp_gate, w_exp_up, w_exp_down, final_norm_g):
    assert x.shape == (BATCH, SEQ, D_MODEL) and ctx.shape == (BATCH, CTX_LEN, D_MODEL)
    assert w_mod.shape[0] == 1, "single-layer problem"
    layer = 0

    cvec = jnp.concatenate(
        [c, c_ctx[None, :], jnp.zeros((MOD_ROWS - BATCH - 1, D_MODEL), F32)], axis=0)
    mod = _mod(cvec, w_mod[layer], b_mod[layer][None, :])
    chunk = lambda rows, k: rows[:, k * D_MODEL:(k + 1) * D_MODEL]
    mod_x = mod[:BATCH][:, None, :]
    sh1, sc1, g1, sh2, sc2, g2 = (mod_x[..., k * D_MODEL:(k + 1) * D_MODEL] for k in range(N_MOD))
    mod_c = mod[BATCH:BATCH + 1]
    csh1, csc1 = chunk(mod_c, 0), chunk(mod_c, 1)

    n1 = norm1_g[layer][None, :]
    w_in_b = w_in[layer].astype(BF16)
    k_ctx, vt_ctx = _ctx_kv(ctx, n1, csh1, csc1, w_in_b)

    cos, sin = _rope_tables()
    q, k, vt, z, bg, ga, gc = _proj(x, n1, sh1, sc1, w_in_b, cos, sin)
    sink_log2 = attn_sink[layer].reshape(-1) * LOG2E
    o_attn = _attn(sink_log2, q, k, vt, k_ctx, vt_ctx)

    wr_hi, wr_lo = _split_bf16(w_router[layer].T)
    w3 = jnp.concatenate(
        [w_proj_attn[layer], w_proj_conv[layer], w_out[layer],
         jnp.zeros((D_MODEL, W3_WIDTH - 3 * D_MODEL), F32)], axis=1).astype(BF16)
    x1, h2, logits = _merge(
        x, o_attn, z, bg, ga, gc, conv_w[layer], w3,
        g1, sh2, sc2, norm2_g[layer][None, :], jnp.concatenate([wr_hi, wr_lo], axis=0))

    pos, gate, band = _route(logits)
    yet = _ffn(band, pos, h2, w_exp_gate[layer], w_exp_up[layer], w_exp_down[layer])
    return _combine(band, pos, gate, yet, x1, g2, final_norm_g[None, :])
```

```python
import jax
import jax.numpy as jnp
import numpy as np
from jax import lax
from jax.experimental import pallas as pl
from jax.experimental.pallas import tpu as pltpu

D_MODEL = 1024
BATCH = 16
SEQ = 2048
CTX_LEN = 256
GRID_W = 64
N_HEADS = 16
N_KV_HEADS = 4
GROUP = N_HEADS // N_KV_HEADS
HEAD_DIM = D_MODEL // N_HEADS
ATTN_WIDTH = N_HEADS * HEAD_DIM
KV_WIDTH = N_KV_HEADS * HEAD_DIM
WINDOW = 128
Q_BLOCK = 128
ROPE_BASE = 10000.0
CONV_WIDTH = D_MODEL
N_EXPERTS = 16
EXPERT_FF = D_MODEL
CAPACITY_FACTOR = 2
N_MOD = 6
EPS = 1e-6
NEG_INF = -1e30
LOG2E = 1.4426950408889634

CAP = CAPACITY_FACTOR * SEQ // N_EXPERTS
N_QBLK = SEQ // Q_BLOCK
LANES = 128
F32_ROWS = 8
BF16_ROWS = 16
W3_WIDTH = 3 * D_MODEL + LANES
TOK_TILE = 1024
PROJ_SUB = 256
ATTN_TILE = 1024
MERGE_TILE = 1024
SUB_TILE = 256
CTX_B = 4
ROUTE_B = 8
FFN_B = 2
GATHER_LO_END = 5 * SEQ // 8
GATHER_HI_START = 3 * SEQ // 8
N_PLANES = 3
N_SCAT = 8
SCAT_TOK = SEQ // N_SCAT
SCAT_PER_STEP = 4
MOD_ROWS = 32
VMEM_LIMIT = 56 * 1024 * 1024

OFF_Q = 0
OFF_K = OFF_Q + ATTN_WIDTH
OFF_V = OFF_K + KV_WIDTH
OFF_U = OFF_V + KV_WIDTH
OFF_B = OFF_U + CONV_WIDTH
OFF_C = OFF_B + CONV_WIDTH
OFF_GA = OFF_C + CONV_WIDTH
OFF_GC = OFF_GA + D_MODEL
IN_WIDTH = OFF_GC + D_MODEL

F32 = jnp.float32
BF16 = jnp.bfloat16


def _params(*sem):
    return pltpu.CompilerParams(dimension_semantics=sem, vmem_limit_bytes=VMEM_LIMIT)


def _dot(a, b):
    return jnp.dot(a, b, preferred_element_type=F32)


def _dot_nt(a, b):
    return lax.dot_general(a, b, (((1,), (1,)), ((), ())), preferred_element_type=F32)


def _sigmoid(x):
    return 1.0 / (1.0 + jnp.exp(-x))


def _split_bf16(x):
    hi = x.astype(BF16)
    lo = (x - hi.astype(F32)).astype(BF16)
    return hi, lo


def _norm_mod(x, g, shift, scale):
    y = x * lax.rsqrt(jnp.mean(x * x, axis=-1, keepdims=True) + EPS) * g
    return y * (1.0 + scale) + shift


def _mod_kernel(c_ref, w_ref, b_ref, o_ref):
    cv = c_ref[...]
    s_hi, s_lo = _split_bf16(cv * _sigmoid(cv))
    w_hi, w_lo = _split_bf16(w_ref[...])
    o_ref[...] = _dot(s_hi, w_hi) + _dot(s_lo, w_hi) + _dot(s_hi, w_lo) + b_ref[...]


def _mod(cvec, w_mod, b_mod):
    n_out = N_MOD * D_MODEL
    blk = D_MODEL
    return pl.pallas_call(
        _mod_kernel,
        grid=(n_out // blk,),
        in_specs=[
            pl.BlockSpec((MOD_ROWS, D_MODEL), lambda j: (0, 0)),
            pl.BlockSpec((D_MODEL, blk), lambda j: (0, j)),
            pl.BlockSpec((1, blk), lambda j: (0, j)),
        ],
        out_specs=pl.BlockSpec((MOD_ROWS, blk), lambda j: (0, j)),
        out_shape=jax.ShapeDtypeStruct((MOD_ROWS, n_out), F32),
        compiler_params=_params("arbitrary"),
        name="mod",
    )(cvec, w_mod, b_mod)


def _ctx_kernel(ctx_ref, g_ref, sh_ref, sc_ref, w_ref, k_ref, vt_ref):
    x = ctx_ref[...].reshape(CTX_B * CTX_LEN, D_MODEL)
    h = _norm_mod(x, g_ref[...], sh_ref[...], sc_ref[...]).astype(BF16)
    kv = _dot(h, w_ref[...])
    k_ref[...] = kv[:, :KV_WIDTH].astype(BF16).reshape(CTX_B, CTX_LEN, KV_WIDTH)
    for i in range(CTX_B):
        vt_ref[i] = kv[i * CTX_LEN:(i + 1) * CTX_LEN, KV_WIDTH:].T.astype(BF16)


def _ctx_kv(ctx, g1, csh1, csc1, w_kv):
    row = pl.BlockSpec((1, D_MODEL), lambda b: (0, 0))
    return pl.pallas_call(
        _ctx_kernel,
        grid=(BATCH // CTX_B,),
        in_specs=[
            pl.BlockSpec((CTX_B, CTX_LEN, D_MODEL), lambda b: (b, 0, 0)),
            row, row, row,
            pl.BlockSpec((D_MODEL, 2 * KV_WIDTH), lambda b: (0, 0)),
        ],
        out_specs=[pl.BlockSpec((CTX_B, CTX_LEN, KV_WIDTH), lambda b: (b, 0, 0)),
                   pl.BlockSpec((CTX_B, KV_WIDTH, CTX_LEN), lambda b: (b, 0, 0))],
        out_shape=[jax.ShapeDtypeStruct((BATCH, CTX_LEN, KV_WIDTH), BF16),
                   jax.ShapeDtypeStruct((BATCH, KV_WIDTH, CTX_LEN), BF16)],
        compiler_params=_params("arbitrary"),
        name="ctx_kv",
    )(ctx, g1, csh1, csc1, w_kv)


def _proj_kernel(x_ref, g_ref, sh_ref, sc_ref, w_ref, cos_ref, sin_ref,
                 q_ref, k_ref, vt_ref, z_ref, bg_ref, ga_ref, gc_ref):
    lane = lax.broadcasted_iota(jnp.int32, (PROJ_SUB, LANES), 1)
    first_half = (lane & (HEAD_DIM // 4)) == 0
    scale = HEAD_DIM ** -0.5 * LOG2E

    for r in range(TOK_TILE // PROJ_SUB):
        rs = slice(r * PROJ_SUB, (r + 1) * PROJ_SUB)
        h = _norm_mod(x_ref[0, rs, :], g_ref[...], sh_ref[0], sc_ref[0]).astype(BF16)
        cos = cos_ref[rs, :]
        sin = sin_ref[rs, :]

        def rope(a):
            rot = jnp.where(first_half,
                            pltpu.roll(a, LANES - HEAD_DIM // 4, 1),
                            pltpu.roll(a, HEAD_DIM // 4, 1))
            return a * cos + rot * sin

        def proj(off, width):
            return _dot(h, w_ref[:, off:off + width])

        qa = proj(OFF_Q, ATTN_WIDTH)
        for j in range(ATTN_WIDTH // LANES):
            sl = slice(j * LANES, (j + 1) * LANES)
            q_ref[0, rs, sl] = (rope(qa[:, sl]) * scale).astype(BF16)
        ka = proj(OFF_K, KV_WIDTH)
        for j in range(KV_WIDTH // LANES):
            sl = slice(j * LANES, (j + 1) * LANES)
            k_ref[0, rs, sl] = rope(ka[:, sl]).astype(BF16)
        vt_ref[0, :, rs] = proj(OFF_V, KV_WIDTH).T.astype(BF16)
        z_ref[0, rs, :] = (proj(OFF_C, CONV_WIDTH) * proj(OFF_U, CONV_WIDTH)).astype(BF16)
        bg_ref[0, rs, :] = proj(OFF_B, CONV_WIDTH).astype(BF16)
        ga_ref[0, rs, :] = _sigmoid(proj(OFF_GA, D_MODEL)).astype(BF16)
        gc_ref[0, rs, :] = _sigmoid(proj(OFF_GC, D_MODEL)).astype(BF16)


def _proj(x, g1, sh1, sc1, w_in, cos, sin):
    nt = SEQ // TOK_TILE
    tile = lambda w: pl.BlockSpec((1, TOK_TILE, w), lambda b, t: (b, t, 0))
    per_b = pl.BlockSpec((1, 1, D_MODEL), lambda b, t: (b, 0, 0))
    tab = pl.BlockSpec((TOK_TILE, LANES), lambda b, t: (t, 0))
    shp = lambda w: jax.ShapeDtypeStruct((BATCH, SEQ, w), BF16)
    vt_spec = pl.BlockSpec((1, KV_WIDTH, TOK_TILE), lambda b, t: (b, 0, t))
    vt_shape = jax.ShapeDtypeStruct((BATCH, KV_WIDTH, SEQ), BF16)
    return pl.pallas_call(
        _proj_kernel,
        grid=(BATCH, nt),
        in_specs=[
            tile(D_MODEL),
            pl.BlockSpec((1, D_MODEL), lambda b, t: (0, 0)),
            per_b, per_b,
            pl.BlockSpec((D_MODEL, IN_WIDTH), lambda b, t: (0, 0), pipeline_mode=pl.Buffered(1)),
            tab, tab,
        ],
        out_specs=[tile(ATTN_WIDTH), tile(KV_WIDTH), vt_spec, tile(CONV_WIDTH), tile(CONV_WIDTH),
                   tile(D_MODEL), tile(D_MODEL)],
        out_shape=[shp(ATTN_WIDTH), shp(KV_WIDTH), vt_shape, shp(CONV_WIDTH), shp(CONV_WIDTH),
                   shp(D_MODEL), shp(D_MODEL)],
        compiler_params=_params("arbitrary", "arbitrary"),
        name="proj",
    )(x, g1, sh1, sc1, w_in, cos, sin)


def _attn_kernel(sink_ref, q_ref, k_ref, vt_ref, kx_ref, vx_ref, o_ref):
    cols = GROUP * Q_BLOCK
    key = lax.broadcasted_iota(jnp.int32, (Q_BLOCK, cols), 0)
    qry = lax.broadcasted_iota(jnp.int32, (Q_BLOCK, cols), 1) & (Q_BLOCK - 1)
    lane_head = lax.broadcasted_iota(jnp.int32, (1, cols), 1) >> 7
    n_keys = Q_BLOCK + 2 * WINDOW + CTX_LEN
    ones = jnp.ones((BF16_ROWS, n_keys), BF16)

    blocks = []
    for sb in range(ATTN_TILE // Q_BLOCK):
        i = pl.program_id(1) * (ATTN_TILE // Q_BLOCK) + sb
        blocks.append(dict(
            rows=slice(sb * Q_BLOCK, (sb + 1) * Q_BLOCK),
            p0=pl.multiple_of(jnp.maximum(i - 1, 0) * Q_BLOCK, Q_BLOCK),
            c0=pl.multiple_of(i * Q_BLOCK, Q_BLOCK),
            n0=pl.multiple_of(jnp.minimum(i + 1, N_QBLK - 1) * Q_BLOCK, Q_BLOCK),
            ok_prev=key >= qry + jnp.where(i > 0, 0, Q_BLOCK),
            ok_next=key <= qry - jnp.where(i < N_QBLK - 1, 0, Q_BLOCK)))

    def scores(blk, g):
        ks = slice(g * HEAD_DIM, (g + 1) * HEAD_DIM)
        q4 = jnp.concatenate(
            [q_ref[0, blk["rows"], (g * GROUP + j) * HEAD_DIM:(g * GROUP + j + 1) * HEAD_DIM]
             for j in range(GROUP)], axis=0)
        k_all = jnp.concatenate(
            [k_ref[0, pl.ds(blk["p0"], Q_BLOCK), ks], k_ref[0, pl.ds(blk["c0"], Q_BLOCK), ks],
             k_ref[0, pl.ds(blk["n0"], Q_BLOCK), ks], kx_ref[0, :, ks]], axis=0)
        return _dot_nt(k_all, q4)

    def softmax(blk, g, st):
        pieces = [jnp.where(blk["ok_prev"], st[:WINDOW], NEG_INF),
                  st[WINDOW:WINDOW + Q_BLOCK],
                  jnp.where(blk["ok_next"], st[WINDOW + Q_BLOCK:2 * WINDOW + Q_BLOCK], NEG_INF),
                  st[2 * WINDOW + Q_BLOCK:]]
        sink = jnp.zeros((1, cols), F32)
        for j in range(GROUP):
            sink = jnp.where(lane_head == j, sink_ref[g * GROUP + j], sink)
        m = sink
        for piece in pieces:
            m = jnp.maximum(m, jnp.max(piece, axis=0, keepdims=True))
        pt = jnp.concatenate([jnp.exp2((piece - m).astype(BF16)) for piece in pieces], axis=0)
        return pt, jnp.exp2(sink - m)

    def values(blk, g, pt, p_sink):
        ks = slice(g * HEAD_DIM, (g + 1) * HEAD_DIM)
        vt = jnp.concatenate(
            [vt_ref[0, ks, pl.ds(blk["p0"], Q_BLOCK)], vt_ref[0, ks, pl.ds(blk["c0"], Q_BLOCK)],
             vt_ref[0, ks, pl.ds(blk["n0"], Q_BLOCK)], vx_ref[0, ks, :]], axis=1)
        ot = _dot(jnp.concatenate([vt, ones], axis=0), pt)
        denom = ot[HEAD_DIM:HEAD_DIM + 1] + p_sink
        on = ot[:HEAD_DIM] * (1.0 / denom)
        for jj in range(GROUP // 2):
            pair_t = jnp.concatenate(
                [on[:, (2 * jj) * Q_BLOCK:(2 * jj + 1) * Q_BLOCK],
                 on[:, (2 * jj + 1) * Q_BLOCK:(2 * jj + 2) * Q_BLOCK]], axis=0)
            col = (g * GROUP + 2 * jj) * HEAD_DIM
            o_ref[0, blk["rows"], col:col + 2 * HEAD_DIM] = pair_t.T.astype(BF16)

    chains = [(blk, g) for blk in blocks for g in range(N_KV_HEADS)]
    st_next = scores(*chains[0])
    for n, chain in enumerate(chains):
        st = st_next
        if n + 1 < len(chains):
            st_next = scores(*chains[n + 1])
        values(*chain, *softmax(*chain, st))


def _attn(sink, q, k, vt, k_ctx, vt_ctx):
    qspec = pl.BlockSpec((1, ATTN_TILE, ATTN_WIDTH), lambda b, i: (b, i, 0))
    return pl.pallas_call(
        _attn_kernel,
        grid=(BATCH, SEQ // ATTN_TILE),
        in_specs=[pl.BlockSpec(memory_space=pltpu.SMEM), qspec,
                  pl.BlockSpec((1, SEQ, KV_WIDTH), lambda b, i: (b, 0, 0)),
                  pl.BlockSpec((1, KV_WIDTH, SEQ), lambda b, i: (b, 0, 0)),
                  pl.BlockSpec((1, CTX_LEN, KV_WIDTH), lambda b, i: (b, 0, 0)),
                  pl.BlockSpec((1, KV_WIDTH, CTX_LEN), lambda b, i: (b, 0, 0))],
        out_specs=qspec,
        out_shape=jax.ShapeDtypeStruct((BATCH, SEQ, ATTN_WIDTH), BF16),
        compiler_params=_params("arbitrary", "arbitrary"),
        name="attn",
    )(sink, q, k, vt, k_ctx, vt_ctx)


def _merge_kernel(x_ref, oa_ref, z_ref, zp_ref, zn_ref, bg_ref, ga_ref, gc_ref, cw_ref,
                  w3_ref, g1_ref, sh2_ref, sc2_ref, n2_ref, wr_ref,
                  x1_ref, h2_ref, lg_ref):
    t = pl.program_id(1)
    nt = pl.num_programs(1)
    z_before = zp_ref[0, BF16_ROWS - 1:BF16_ROWS, :].astype(F32) * jnp.where(t > 0, 1.0, 0.0)
    z_after = zn_ref[0, 0:1, :].astype(F32) * jnp.where(t < nt - 1, 1.0, 0.0)
    row = lax.broadcasted_iota(jnp.int32, (F32_ROWS, 1), 0)
    n_sub = MERGE_TILE // SUB_TILE
    subs = [slice(r * SUB_TILE, (r + 1) * SUB_TILE) for r in range(n_sub)]

    def conv(r):
        lo, hi = subs[r].start, subs[r].stop
        z = z_ref[0, subs[r], :].astype(F32)
        before = z_before if r == 0 else z_ref[0, lo - 1:lo, :].astype(F32)
        after = z_after if r == n_sub - 1 else z_ref[0, hi:hi + 1, :].astype(F32)
        up = pltpu.roll(z, 1, 0)
        z_prev = jnp.concatenate(
            [jnp.where(row == 0, before, up[:F32_ROWS]), up[F32_ROWS:]], axis=0)
        dn = pltpu.roll(z, SUB_TILE - 1, 0)
        z_next = jnp.concatenate(
            [dn[:-F32_ROWS], jnp.where(row == F32_ROWS - 1, after, dn[-F32_ROWS:])], axis=0)
        y = bg_ref[0, subs[r], :].astype(F32) * (
            cw_ref[0:1, :] * z_prev + cw_ref[1:2, :] * z + cw_ref[2:3, :] * z_next)
        return y.astype(BF16)

    a, s = [], []
    for r in range(n_sub):
        a.append(_dot(oa_ref[0, subs[r], :], w3_ref[:, 0:D_MODEL]))
        s.append(_dot(conv(r), w3_ref[:, D_MODEL:2 * D_MODEL]))
    branch = []
    for r, rs in enumerate(subs):
        merged = ga_ref[0, rs, :].astype(F32) * a[r] + gc_ref[0, rs, :].astype(F32) * s[r]
        branch.append(_dot(merged.astype(BF16), w3_ref[:, 2 * D_MODEL:3 * D_MODEL]))
    for r, rs in enumerate(subs):
        x1 = x_ref[0, rs, :] + g1_ref[0] * branch[r]
        x1_ref[0, rs, :] = x1
        h2 = _norm_mod(x1, n2_ref[...], sh2_ref[0], sc2_ref[0])
        h_hi, h_lo = _split_bf16(h2)
        h2_ref[0, rs, :] = h_hi
        both = _dot_nt(wr_ref[...], h_hi)
        lg_ref[0, :, rs] = (both[:N_EXPERTS] + both[N_EXPERTS:]
                            + _dot_nt(wr_ref[:N_EXPERTS], h_lo))


def _merge(x, oa, z, bg, ga, gc, conv_w, w3, g1, sh2, sc2, n2, wr):
    nt = SEQ // MERGE_TILE
    hb = MERGE_TILE // BF16_ROWS
    n_hb = SEQ // BF16_ROWS
    tile = pl.BlockSpec((1, MERGE_TILE, D_MODEL), lambda b, t: (b, t, 0))
    halo_p = pl.BlockSpec((1, BF16_ROWS, D_MODEL), lambda b, t: (b, jnp.maximum(t * hb - 1, 0), 0))
    halo_n = pl.BlockSpec((1, BF16_ROWS, D_MODEL),
                          lambda b, t: (b, jnp.minimum((t + 1) * hb, n_hb - 1), 0))
    per_b = pl.BlockSpec((1, 1, D_MODEL), lambda b, t: (b, 0, 0))
    full = lambda r, c: pl.BlockSpec((r, c), lambda b, t: (0, 0))
    return pl.pallas_call(
        _merge_kernel,
        grid=(BATCH, nt),
        in_specs=[tile, tile, tile, halo_p, halo_n, tile, tile, tile,
                  full(3, D_MODEL), full(D_MODEL, W3_WIDTH),
                  per_b, per_b, per_b, full(1, D_MODEL), full(2 * N_EXPERTS, D_MODEL)],
        out_specs=[tile, tile, pl.BlockSpec((1, N_EXPERTS, MERGE_TILE), lambda b, t: (b, 0, t))],
        out_shape=[jax.ShapeDtypeStruct((BATCH, SEQ, D_MODEL), F32),
                   jax.ShapeDtypeStruct((BATCH, SEQ, D_MODEL), BF16),
                   jax.ShapeDtypeStruct((BATCH, N_EXPERTS, SEQ), F32)],
        compiler_params=_params("arbitrary", "arbitrary"),
        name="merge",
    )(x, oa, z, z, z, bg, ga, gc, conv_w, w3, g1, sh2, sc2, n2, wr)


def _cumsum_excl(mf, upper):
    nblk = SEQ // LANES
    rows = mf.shape[0]
    stacked = jnp.concatenate([mf[:, j * LANES:(j + 1) * LANES] for j in range(nblk)], axis=0)
    within = _dot(stacked.astype(BF16), upper)
    tot = jnp.sum(stacked, axis=1, keepdims=True)
    off = jnp.zeros((rows, 1), F32)
    out = []
    for j in range(nblk):
        rs = slice(j * rows, (j + 1) * rows)
        out.append(within[rs] + off)
        off = off + tot[rs]
    return jnp.concatenate(out, axis=1)


def _route_kernel(lg_ref, pos_ref, gate_ref, band_ref):
    rows = ROUTE_B * N_EXPERTS
    lg = lg_ref[...]
    ex = jnp.exp(lg - jnp.max(lg, axis=1, keepdims=True))
    aff = (ex / jnp.sum(ex, axis=1, keepdims=True)).reshape(rows, SEQ)

    def body(_, carry):
        lo, hi = carry
        mid = lo + ((hi - lo) >> 1)
        cnt = jnp.sum(jnp.where(aff >= pltpu.bitcast(mid, F32), 1.0, 0.0), axis=1, keepdims=True)
        ge = cnt >= CAP
        return jnp.where(ge, mid, lo), jnp.where(ge, hi, mid)

    one_bits = 0x3F800000
    lo0 = jnp.zeros((rows, 1), jnp.int32)
    hi0 = jnp.full((rows, 1), one_bits + 1, jnp.int32)
    lo, _ = lax.fori_loop(0, 31, body, (lo0, hi0))
    th = pltpu.bitcast(lo, F32)

    ku = lax.broadcasted_iota(jnp.int32, (LANES, LANES), 0)
    nu = lax.broadcasted_iota(jnp.int32, (LANES, LANES), 1)
    upper = jnp.where(ku < nu, 1.0, 0.0).astype(BF16)
    gtf = jnp.where(aff > th, 1.0, 0.0)
    eqf = jnp.where(aff == th, 1.0, 0.0)
    need = CAP - jnp.sum(gtf, axis=1, keepdims=True)
    sel = gtf + eqf * jnp.where(_cumsum_excl(eqf, upper) < need, 1.0, 0.0)
    pos = _cumsum_excl(sel, upper)
    pos_ref[...] = jnp.where(sel > 0.0, pos, -1.0).reshape(ROUTE_B, N_EXPERTS, SEQ)
    gate_ref[...] = jnp.where(sel > 0.0, aff, 0.0).reshape(ROUTE_B, N_EXPERTS, SEQ)
    count = pos + sel
    tok_last_lo = jnp.sum(jnp.where(count <= CAP // 2 - 1, 1.0, 0.0), axis=1, keepdims=True)
    tok_first_hi = jnp.sum(jnp.where(count <= CAP // 2, 1.0, 0.0), axis=1, keepdims=True)
    banded = jnp.logical_and(tok_last_lo < GATHER_LO_END, tok_first_hi >= GATHER_HI_START)
    tok_mid_lo = jnp.sum(jnp.where(count <= CAP // 4 - 1, 1.0, 0.0), axis=1, keepdims=True)
    tok_mid_hi = jnp.sum(jnp.where(count <= 3 * CAP // 4, 1.0, 0.0), axis=1, keepdims=True)
    mid = jnp.logical_and(tok_mid_lo < GATHER_HI_START, tok_mid_hi >= GATHER_LO_END)
    flags = jnp.where(banded, 1, 0) + jnp.where(mid, 2, 0)
    band_ref[...] = flags.astype(jnp.int32).reshape(ROUTE_B, N_EXPERTS, 1)


def _route(logits):
    spec = pl.BlockSpec((ROUTE_B, N_EXPERTS, SEQ), lambda b: (b, 0, 0))
    shape = jax.ShapeDtypeStruct((BATCH, N_EXPERTS, SEQ), F32)
    return pl.pallas_call(
        _route_kernel,
        grid=(BATCH // ROUTE_B,),
        in_specs=[spec],
        out_specs=[spec, spec, pl.BlockSpec((ROUTE_B, N_EXPERTS, 1), lambda b: (b, 0, 0))],
        out_shape=[shape, shape, jax.ShapeDtypeStruct((BATCH, N_EXPERTS, 1), jnp.int32)],
        compiler_params=_params("arbitrary"),
        name="route",
    )(logits)


def _ffn_kernel(band_ref, pos_ref, h_ref, wg_ref, wu_ref, wd_ref, yet_ref,
                wg_b, wu_b, wdt_b, dense_ref):
    e, grp = pl.program_id(0), pl.program_id(1)

    @pl.when(grp == 0)
    def _():
        wg_b[...] = wg_ref[0].astype(BF16)
        wu_b[...] = wu_ref[0].astype(BF16)
        wdt_b[...] = wd_ref[0].T.astype(BF16)

    def gather(i, slots, toks):
        n_slots, n_toks = slots.stop - slots.start, toks.stop - toks.start
        slot = (lax.broadcasted_iota(jnp.int32, (n_slots, n_toks), 0) + slots.start).astype(F32)
        onehot = jnp.where(slot == pos_ref[i, 0, :, toks], 1.0, 0.0).astype(BF16)
        return _dot(onehot, h_ref[i, toks, :]).astype(BF16)

    @pl.when(jnp.logical_and(e == 0, grp == 0))
    def _():
        dense_ref[...] = jnp.zeros_like(dense_ref)

    banded = [(band_ref[(grp * FFN_B + i) * N_EXPERTS + e] & 1) == 1 for i in range(FFN_B)]
    for i in range(FFN_B):
        @pl.when(jnp.logical_not(banded[i]))
        def _(i=i):
            dense_ref[i] = gather(i, slice(0, CAP), slice(0, SEQ))

    xes = []
    for i in range(FFN_B):
        windowed = jnp.concatenate(
            [gather(i, slice(0, CAP // 2), slice(0, GATHER_LO_END)),
             gather(i, slice(CAP // 2, CAP), slice(GATHER_HI_START, SEQ))], axis=0)
        xes.append(jnp.where(banded[i], windowed, dense_ref[i]))
    xe = jnp.concatenate(xes, axis=0)
    act = _dot(xe, wg_b[...])
    up = _dot(xe, wu_b[...])
    hid = (act * _sigmoid(act) * up).astype(BF16)
    ye_t = _dot_nt(wdt_b[...], hid)
    for i in range(FFN_B):
        yet_ref[i, 0] = ye_t[:, i * CAP:i * CAP + CAP // 2].astype(BF16)
        yet_ref[i, 1] = ye_t[:, i * CAP + CAP // 2:(i + 1) * CAP].astype(BF16)
        yet_ref[i, 2] = ye_t[:, i * CAP + CAP // 4:i * CAP + 3 * CAP // 4].astype(BF16)


def _ffn(band, pos, h2, wg, wu, wd):
    n_grp = BATCH // FFN_B

    def wspec(lead):
        return pl.BlockSpec(
            (1, D_MODEL, EXPERT_FF),
            lambda e, g: (jnp.minimum(e + jnp.where(g >= n_grp - lead, 1, 0), N_EXPERTS - 1),
                          0, 0))
    wscr = pltpu.VMEM((D_MODEL, EXPERT_FF), BF16)
    return pl.pallas_call(
        _ffn_kernel,
        grid=(N_EXPERTS, n_grp),
        in_specs=[pl.BlockSpec(memory_space=pltpu.SMEM),
                  pl.BlockSpec((FFN_B, 1, 1, SEQ), lambda e, g: (g, e, 0, 0)),
                  pl.BlockSpec((FFN_B, SEQ, D_MODEL), lambda e, g: (g, 0, 0)),
                  wspec(3), wspec(2), wspec(1)],
        out_specs=pl.BlockSpec((FFN_B, N_PLANES, D_MODEL, CAP // 2), lambda e, g: (g, 0, 0, e)),
        out_shape=jax.ShapeDtypeStruct((BATCH, N_PLANES, D_MODEL, N_EXPERTS * CAP // 2), BF16),
        scratch_shapes=[wscr, wscr, wscr, pltpu.VMEM((FFN_B, CAP, D_MODEL), BF16)],
        compiler_params=_params("arbitrary", "arbitrary"),
        name="ffn",
    )(band.reshape(-1), pos[:, :, None, :], h2, wg, wu, wd)


def _combine_kernel(band_ref, pos_ref, gate_ref, yet_ref, x1_ref, g2_ref, fg_ref, o_ref,
                    other_ref):
    b, step = pl.program_id(0), pl.program_id(1)
    n_half = n_mid = 0
    for ee in range(N_EXPERTS):
        flags = band_ref[b * N_EXPERTS + ee]
        n_half, n_mid = n_half + (flags & 1), n_mid + (flags >> 1)
    all_banded, all_mid = n_half == N_EXPERTS, n_mid == N_EXPERTS

    @pl.when(jnp.logical_and(b == 0, step == 0))
    def _():
        other_ref[...] = jnp.zeros_like(other_ref)

    def half(cols, plane, first_slot):
        slot = (lax.broadcasted_iota(jnp.int32, (CAP // 2, SCAT_TOK), 0) + first_slot).astype(F32)
        scat = jnp.concatenate(
            [jnp.where(slot == pos_ref[0, ee:ee + 1, cols], gate_ref[0, ee:ee + 1, cols], 0.0)
             .astype(BF16) for ee in range(N_EXPERTS)], axis=0)
        return _dot(yet_ref[0, plane], scat)

    chains = []
    for j in range(SCAT_PER_STEP):
        c = step * SCAT_PER_STEP + j
        cols = slice(j * SCAT_TOK, (j + 1) * SCAT_TOK)
        tok0, tok1 = c * SCAT_TOK, (c + 1) * SCAT_TOK
        lo_only = jnp.logical_and(all_banded, tok1 <= GATHER_HI_START)
        hi_only = jnp.logical_and(all_banded, tok0 >= GATHER_LO_END)
        mid_only = jnp.logical_and(
            all_mid, jnp.logical_and(tok0 >= GATHER_HI_START, tok1 <= GATHER_LO_END))
        both = jnp.logical_not(jnp.logical_or(jnp.logical_or(lo_only, hi_only), mid_only))
        plane = jnp.where(hi_only, 1, jnp.where(mid_only, 2, 0))
        first_slot = jnp.where(hi_only, CAP // 2, jnp.where(mid_only, CAP // 4, 0))
        chains.append((j, cols, plane, first_slot, both))

    for j, cols, _, _, both in chains:
        @pl.when(both)
        def _(j=j, cols=cols):
            other_ref[j] = half(cols, 1, CAP // 2)

    moe_ts = [half(cols, plane, first_slot) + jnp.where(both, other_ref[j], 0.0)
              for j, cols, plane, first_slot, both in chains]
    for (j, cols, _, _, _), moe_t in zip(chains, moe_ts):
        x2 = x1_ref[0, cols, :] + g2_ref[0] * moe_t.T
        o_ref[0, cols, :] = (x2 * lax.rsqrt(jnp.mean(x2 * x2, axis=-1, keepdims=True) + EPS)
                             * fg_ref[...])


def _combine(band, pos, gate, yet, x1, g2, fg):
    step_tok = SCAT_PER_STEP * SCAT_TOK
    sel_c = pl.BlockSpec((1, N_EXPERTS, step_tok), lambda b, c: (b, 0, c))
    tok_c = pl.BlockSpec((1, step_tok, D_MODEL), lambda b, c: (b, c, 0))
    yet_spec = pl.BlockSpec((1, N_PLANES, D_MODEL, N_EXPERTS * CAP // 2),
                            lambda b, c: (b, 0, 0, 0))
    return pl.pallas_call(
        _combine_kernel,
        grid=(BATCH, N_SCAT // SCAT_PER_STEP),
        in_specs=[pl.BlockSpec(memory_space=pltpu.SMEM), sel_c, sel_c, yet_spec,
                  tok_c,
                  pl.BlockSpec((1, 1, D_MODEL), lambda b, c: (b, 0, 0)),
                  pl.BlockSpec((1, D_MODEL), lambda b, c: (0, 0))],
        out_specs=tok_c,
        out_shape=jax.ShapeDtypeStruct((BATCH, SEQ, D_MODEL), F32),
        scratch_shapes=[pltpu.VMEM((SCAT_PER_STEP, D_MODEL, SCAT_TOK), F32)],
        compiler_params=_params("arbitrary", "arbitrary"),
        name="combine",
    )(band.reshape(-1), pos, gate, yet, x1, g2, fg)


def _rope_tables():
    rows = SEQ // GRID_W
    row = np.repeat(np.arange(rows, dtype=np.float32), GRID_W)
    col = np.tile(np.arange(GRID_W, dtype=np.float32), rows)
    n_freq = HEAD_DIM // 4
    inv_freq = (ROPE_BASE ** (-np.arange(n_freq, dtype=np.float32) / n_freq)).astype(np.float32)
    ang_r = row[:, None] * inv_freq[None, :]
    ang_c = col[:, None] * inv_freq[None, :]
    ang = np.concatenate([ang_r, ang_r, ang_c, ang_c], axis=-1)
    sign = np.tile(np.concatenate([-np.ones(n_freq, np.float32), np.ones(n_freq, np.float32)]), 2)
    reps = LANES // HEAD_DIM
    cos = np.tile(np.cos(ang).astype(np.float32), (1, reps))
    sin = np.tile((np.sin(ang) * sign).astype(np.float32), (1, reps))
    return jnp.asarray(cos), jnp.asarray(sin)


def kernel(x, c, ctx, c_ctx, w_mod, b_mod, norm1_g, w_in, attn_sink, conv_w, w_proj_attn,
           w_proj_conv, w_out, norm2_g, w_router, w_exp_gate, w_exp_up, w_exp_down, final_norm_g):
    assert x.shape == (BATCH, SEQ, D_MODEL) and ctx.shape == (BATCH, CTX_LEN, D_MODEL)
    assert w_mod.shape[0] == 1, "single-layer problem"
    layer = 0

    cvec = jnp.concatenate(
        [c, c_ctx[None, :], jnp.zeros((MOD_ROWS - BATCH - 1, D_MODEL), F32)], axis=0)
    mod = _mod(cvec, w_mod[layer], b_mod[layer][None, :])
    chunk = lambda rows, k: rows[:, k * D_MODEL:(k + 1) * D_MODEL]
    mod_x = mod[:BATCH][:, None, :]
    sh1, sc1, g1, sh2, sc2, g2 = (mod_x[..., k * D_MODEL:(k + 1) * D_MODEL] for k in range(N_MOD))
    mod_c = mod[BATCH:BATCH + 1]
    csh1, csc1 = chunk(mod_c, 0), chunk(mod_c, 1)

    n1 = norm1_g[layer][None, :]
    w_in_b = w_in[layer].astype(BF16)
    k_ctx, vt_ctx = _ctx_kv(ctx, n1, csh1, csc1, w_in_b[:, OFF_K:OFF_U])

    cos, sin = _rope_tables()
    q, k, vt, z, bg, ga, gc = _proj(x, n1, sh1, sc1, w_in_b, cos, sin)
    sink_log2 = attn_sink[layer].reshape(-1) * LOG2E
    o_attn = _attn(sink_log2, q, k, vt, k_ctx, vt_ctx)

    wr_hi, wr_lo = _split_bf16(w_router[layer].T)
    w3 = jnp.concatenate(
        [w_proj_attn[layer], w_proj_conv[layer], w_out[layer],
         jnp.zeros((D_MODEL, W3_WIDTH - 3 * D_MODEL), F32)], axis=1).astype(BF16)
    x1, h2, logits = _merge(
        x, o_attn, z, bg, ga, gc, conv_w[layer], w3,
        g1, sh2, sc2, norm2_g[layer][None, :], jnp.concatenate([wr_hi, wr_lo], axis=0))

    pos, gate, band = _route(logits)
    yet = _ffn(band, pos, h2, w_exp_gate[layer], w_exp_up[layer], w_exp_down[layer])
    return _combine(band, pos, gate, yet, x1, g2, final_norm_g[None, :])
```
